```python
import jax, jax.numpy as jnp
from jax import lax
import numpy as np

D_MODEL = 1024
BATCH = 2
SEQ = 8192
DEPTH = 2

N_META = 16
CHUNK = 64
META_PAD = CHUNK - N_META
D_MIX = D_MODEL
LRU_WIDTH = D_MIX // 2
LRU_HEADS = 8
LRU_HEAD_DIM = LRU_WIDTH // LRU_HEADS
LRU_C = 8.0
CONV_WIDTH = 4
CONV_LEFT = 2
CONV_RIGHT = CONV_WIDTH - 1 - CONV_LEFT
GLA_WIDTH = D_MIX - LRU_WIDTH
GLA_HEADS = 4
GLA_DV = GLA_WIDTH // GLA_HEADS
GLA_DK = GLA_DV // 2
GLA_RANK = 16
GLA_GATE_NORM = 16.0
D_FF = 4 * D_MODEL
EPS = 1e-6

IN_WIDTHS = [LRU_WIDTH, LRU_WIDTH,
             GLA_HEADS * GLA_DK, GLA_HEADS * GLA_DK,
             GLA_WIDTH, GLA_WIDTH,
             GLA_RANK, GLA_RANK]
D_IN = int(sum(IN_WIDTHS))
IN_SPLITS = [int(s) for s in np.cumsum(IN_WIDTHS)[:-1]]

kernel_name = "bidir_hymba_rglru_gla_block"


def rmsnorm(x, g):
    xf = x.astype(jnp.float32)
    y = xf * lax.rsqrt(jnp.mean(xf * xf, axis=-1, keepdims=True) + EPS)
    return y.astype(x.dtype) * g


def centred_dwconv(x, w, b):
    y = lax.conv_general_dilated(x, w[:, None, :], window_strides=(1,),
                                 padding=[(CONV_LEFT, CONV_RIGHT)],
                                 dimension_numbers=('NWC', 'WIO', 'NWC'),
                                 feature_group_count=x.shape[-1])
    return y + b


def _lin_combine(c1, c2):
    a1, b1 = c1
    a2, b2 = c2
    return a1 * a2, a2 * b1 + b2


def rglru_direction(xc, wa, ba, wx, bx, lam, reverse):
    B, T, _ = xc.shape
    xh = xc.reshape(B, T, LRU_HEADS, LRU_HEAD_DIM)
    r = jax.nn.sigmoid(jnp.einsum('bthi,hij->bthj', xh, wa).reshape(B, T, LRU_WIDTH) + ba)
    i = jax.nn.sigmoid(jnp.einsum('bthi,hij->bthj', xh, wx).reshape(B, T, LRU_WIDTH) + bx)
    log_a = -LRU_C * r * jax.nn.softplus(-lam)
    a = jnp.exp(log_a)
    u = jnp.sqrt(-jnp.expm1(2.0 * log_a)) * (i * xc)
    _, h = lax.associative_scan(_lin_combine, (a, u), axis=1, reverse=reverse)
    return h


def rglru_group(x_br, gate_br, conv_w, conv_b,
                wa_f, ba_f, wx_f, bx_f, lam_f, wa_b, ba_b, wx_b, bx_b, lam_b):
    xc = centred_dwconv(x_br, conv_w, conv_b).astype(jnp.float32)
    h = (rglru_direction(xc, wa_f, ba_f, wx_f, bx_f, lam_f, reverse=False)
         + rglru_direction(xc, wa_b, ba_b, wx_b, bx_b, lam_b, reverse=True))
    return h.astype(x_br.dtype) * jax.nn.gelu(gate_br)


def insert_pad(t):
    z = jnp.zeros((t.shape[0], META_PAD, t.shape[2]), t.dtype)
    return jnp.concatenate([t[:, :N_META], z, t[:, N_META:]], axis=1)


def remove_pad(t):
    return jnp.concatenate([t[:, :N_META], t[:, CHUNK:]], axis=1)


def to_chunks(t, d):
    B, Tp = t.shape[0], t.shape[1]
    return t.reshape(B, Tp // CHUNK, CHUNK, GLA_HEADS, d).transpose(1, 0, 3, 2, 4)


def from_chunks(o):
    N, B = o.shape[0], o.shape[1]
    return o.transpose(1, 0, 3, 2, 4).reshape(B, N * CHUNK, GLA_HEADS * GLA_DV)


def gla_chunk_scan(q, k, v, g, strict):
    B, H = q.shape[1], q.shape[2]
    b = jnp.cumsum(g, axis=3)
    mask = jnp.tril(jnp.ones((CHUNK, CHUNK), dtype=bool), k=-1 if strict else 0)

    def step(S, inp):
        qc, kc, vc, bc = inp
        diff = bc[:, :, :, None, :] - bc[:, :, None, :, :]
        decay = jnp.where(mask[:, :, None], jnp.exp(jnp.minimum(diff, 0.0)), 0.0)
        scores = jnp.einsum('bhtc,bhtsc,bhsc->bhts', qc, decay, kc)
        intra = jnp.einsum('bhts,bhsv->bhtv', scores, vc)
        inter = jnp.einsum('bhtc,bhcv->bhtv', qc * jnp.exp(bc), S)
        b_last = bc[:, :, -1:, :]
        S_new = (jnp.exp(b_last)[:, :, 0, :, None] * S
                 + jnp.einsum('bhsc,bhsv->bhcv', kc * jnp.exp(b_last - bc), vc))
        return S_new, intra + inter

    S0 = jnp.zeros((B, H, GLA_DK, GLA_DV), jnp.float32)
    _, o = lax.scan(step, S0, (q, k, v, b))
    return o


def gla_bidirectional(q, k, v, g_f, g_b):
    qp, kp, vp, gfp, gbp = (insert_pad(t) for t in (q, k, v, g_f, g_b))
    o_f = gla_chunk_scan(to_chunks(qp, GLA_DK), to_chunks(kp, GLA_DK),
                         to_chunks(vp, GLA_DV), to_chunks(gfp, GLA_DK), strict=False)
    flip = lambda t: t[:, ::-1]
    o_b = gla_chunk_scan(to_chunks(flip(qp), GLA_DK), to_chunks(flip(kp), GLA_DK),
                         to_chunks(flip(vp), GLA_DV), to_chunks(flip(gbp), GLA_DK), strict=True)
    o = from_chunks(o_f) + flip(from_chunks(o_b))
    return remove_pad(o)


def gla_group(q, k, v, g_out, zg_f, zg_b, wg_f, bg_f, wg_b, bg_b, head_norm):
    B, T = q.shape[0], q.shape[1]
    in_dtype = q.dtype
    q = q.astype(jnp.float32) * (GLA_DK ** -0.5)
    g_f = jax.nn.log_sigmoid((zg_f @ wg_f + bg_f).astype(jnp.float32)) / GLA_GATE_NORM
    g_b = jax.nn.log_sigmoid((zg_b @ wg_b + bg_b).astype(jnp.float32)) / GLA_GATE_NORM
    o = gla_bidirectional(q, k.astype(jnp.float32), v.astype(jnp.float32), g_f, g_b)
    o = o.reshape(B, T, GLA_HEADS, GLA_DV)
    o = o * lax.rsqrt(jnp.mean(o * o, axis=-1, keepdims=True) + EPS)
    o = o.reshape(B, T, GLA_WIDTH).astype(in_dtype) * head_norm
    return o * jax.nn.silu(g_out)


def hybrid_mixer(h, w_in, conv_w, conv_b,
                 lru_wa_f, lru_ba_f, lru_wx_f, lru_bx_f, lru_lambda_f,
                 lru_wa_b, lru_ba_b, lru_wx_b, lru_bx_b, lru_lambda_b,
                 gla_wg_f, gla_bg_f, gla_wg_b, gla_bg_b, gla_head_norm, w_out):
    z = h @ w_in
    x_br, gate_br, q, k, v, g_out, zg_f, zg_b = jnp.split(z, IN_SPLITS, axis=-1)
    y_lru = rglru_group(x_br, gate_br, conv_w, conv_b,
                        lru_wa_f, lru_ba_f, lru_wx_f, lru_bx_f, lru_lambda_f,
                        lru_wa_b, lru_ba_b, lru_wx_b, lru_bx_b, lru_lambda_b)
    y_gla = gla_group(q, k, v, g_out, zg_f, zg_b,
                      gla_wg_f, gla_bg_f, gla_wg_b, gla_bg_b, gla_head_norm)
    return jnp.concatenate([y_lru, y_gla], axis=-1) @ w_out


def squared_relu_mlp(h, w_up, w_down):
    return jnp.square(jax.nn.relu(h @ w_up)) @ w_down


def setup_inputs(seed: int = 0) -> dict:
    key = jax.random.key(seed)
    ks = iter(jax.random.split(key, 40))
    nrm = lambda shape, scale: jax.random.normal(next(ks), shape, jnp.float32) * scale
    gain = lambda shape: 1.0 + nrm(shape, 0.05)

    def lam(shape):
        u = jax.random.uniform(next(ks), shape, jnp.float32, 0.9, 0.999)
        s = u ** (1.0 / LRU_C)
        return jnp.log(s) - jnp.log1p(-s)

    L = DEPTH
    return {
        "x": nrm((BATCH, SEQ, D_MODEL), 1.0),
        "meta_tokens": nrm((N_META, D_MODEL), 1.0),
        "norm_mix_pre": gain((L, D_MODEL)),
        "norm_mix_post": gain((L, D_MODEL)),
        "norm_mlp_pre": gain((L, D_MODEL)),
        "norm_mlp_post": gain((L, D_MODEL)),
        "w_in": nrm((L, D_MODEL, D_IN), D_MODEL ** -0.5),
        "conv_w": nrm((L, CONV_WIDTH, LRU_WIDTH), CONV_WIDTH ** -0.5),
        "conv_b": nrm((L, LRU_WIDTH), 0.01),
        "lru_wa_f": nrm((L, LRU_HEADS, LRU_HEAD_DIM, LRU_HEAD_DIM), LRU_HEAD_DIM ** -0.5),
        "lru_ba_f": nrm((L, LRU_WIDTH), 0.01),
        "lru_wx_f": nrm((L, LRU_HEADS, LRU_HEAD_DIM, LRU_HEAD_DIM), LRU_HEAD_DIM ** -0.5),
        "lru_bx_f": nrm((L, LRU_WIDTH), 0.01),
        "lru_lambda_f": lam((L, LRU_WIDTH)),
        "lru_wa_b": nrm((L, LRU_HEADS, LRU_HEAD_DIM, LRU_HEAD_DIM), LRU_HEAD_DIM ** -0.5),
        "lru_ba_b": nrm((L, LRU_WIDTH), 0.01),
        "lru_wx_b": nrm((L, LRU_HEADS, LRU_HEAD_DIM, LRU_HEAD_DIM), LRU_HEAD_DIM ** -0.5),
        "lru_bx_b": nrm((L, LRU_WIDTH), 0.01),
        "lru_lambda_b": lam((L, LRU_WIDTH)),
        "gla_wg_f": nrm((L, GLA_RANK, GLA_HEADS * GLA_DK), GLA_RANK ** -0.5),
        "gla_bg_f": nrm((L, GLA_HEADS * GLA_DK), 0.1),
        "gla_wg_b": nrm((L, GLA_RANK, GLA_HEADS * GLA_DK), GLA_RANK ** -0.5),
        "gla_bg_b": nrm((L, GLA_HEADS * GLA_DK), 0.1),
        "gla_head_norm": gain((L, GLA_WIDTH)),
        "w_out": nrm((L, D_MIX, D_MODEL), D_MIX ** -0.5),
        "w_mlp_up": nrm((L, D_MODEL, D_FF), D_MODEL ** -0.5),
        "w_mlp_down": nrm((L, D_FF, D_MODEL), D_FF ** -0.5),
    }


def reference(x, meta_tokens, norm_mix_pre, norm_mix_post, norm_mlp_pre, norm_mlp_post,
              w_in, conv_w, conv_b,
              lru_wa_f, lru_ba_f, lru_wx_f, lru_bx_f, lru_lambda_f,
              lru_wa_b, lru_ba_b, lru_wx_b, lru_bx_b, lru_lambda_b,
              gla_wg_f, gla_bg_f, gla_wg_b, gla_bg_b, gla_head_norm,
              w_out, w_mlp_up, w_mlp_down):
    B = x.shape[0]
    meta = jnp.broadcast_to(meta_tokens.astype(x.dtype)[None], (B, N_META, D_MODEL))
    h = jnp.concatenate([meta, x], axis=1)
    for l in range(DEPTH):
        mix = hybrid_mixer(rmsnorm(h, norm_mix_pre[l]), w_in[l], conv_w[l], conv_b[l],
                           lru_wa_f[l], lru_ba_f[l], lru_wx_f[l], lru_bx_f[l], lru_lambda_f[l],
                           lru_wa_b[l], lru_ba_b[l], lru_wx_b[l], lru_bx_b[l], lru_lambda_b[l],
                           gla_wg_f[l], gla_bg_f[l], gla_wg_b[l], gla_bg_b[l], gla_head_norm[l],
                           w_out[l])
        h = h + rmsnorm(mix, norm_mix_post[l])
        ff = squared_relu_mlp(rmsnorm(h, norm_mlp_pre[l]), w_mlp_up[l], w_mlp_down[l])
        h = h + rmsnorm(ff, norm_mlp_post[l])
    return h[:, N_META:]
```

```python
import functools

import numpy as np
import jax
import jax.numpy as jnp
from jax import lax
from jax.experimental import pallas as pl
from jax.experimental.pallas import tpu as pltpu

D_MODEL = 1024
N_META = 16
CHUNK = 64
LRU_WIDTH = 512
LRU_HEADS = 8
LRU_HEAD_DIM = 64
LRU_C = 8.0
GLA_WIDTH = 512
GLA_HEADS = 4
GLA_DV = 128
GLA_DK = 64
GLA_QK = GLA_HEADS * GLA_DK
GLA_RANK = 16
GLA_GATE_NORM = 16.0
D_FF = 4096
EPS = 1e-6

LEAD_CHUNKS = 4
T0 = LEAD_CHUNKS * CHUNK
PADF = T0 - N_META
ROW_TILE = 512
TIME_TILE = 768
FF_TILE = 1024
SUBLANES = 8
VMEM_LIMIT = 56 * 1024 * 1024

N_LEVELS = 6


def _const_spec(shape):
    nd = len(shape)
    return pl.BlockSpec(shape, lambda *_: (0,) * nd, pipeline_mode=pl.Buffered(1))


def _rms_scale(x):
    return lax.rsqrt(jnp.mean(x * x, axis=-1, keepdims=True) + EPS)


def _sigmoid(x):
    return 1.0 / (1.0 + jnp.exp(-x))


def _softplus(x):
    return jnp.maximum(x, 0.0) + jnp.log1p(jnp.exp(-jnp.abs(x)))


def _split_bf16(x):
    hi = x.astype(jnp.bfloat16)
    lo = (x - hi.astype(jnp.float32)).astype(jnp.bfloat16)
    return hi, lo


def _in_proj_kernel(h_ref, g_ref, w_ref, wzg_ref, wg_ref, bg_ref,
                    xg_ref, qk_ref, v_ref, go_ref, gfb_ref):
    x = h_ref[...]
    xn = (x * _rms_scale(x) * g_ref[...]).astype(jnp.bfloat16)
    z = jnp.dot(xn, w_ref[...], preferred_element_type=jnp.float32)
    xg_ref[...] = z[:, :1024]
    q = z[:, 1024:1280] * (GLA_DK ** -0.5)
    qk_ref[...] = jnp.concatenate([q, z[:, 1280:1536]], axis=1)
    v_ref[...] = z[:, 1536:2048]
    go_ref[...] = z[:, 2048:2560]
    zg = jnp.dot(xn, wzg_ref[...], preferred_element_type=jnp.float32)
    zhi = zg.astype(jnp.bfloat16)
    zlo = (zg - zhi.astype(jnp.float32)).astype(jnp.bfloat16)
    lane = lax.broadcasted_iota(jnp.int32, zg.shape, 1)
    lhs = jnp.where(lane < 4 * GLA_RANK, zhi, zlo)
    pre = jnp.dot(lhs, wg_ref[...], preferred_element_type=jnp.float32) + bg_ref[...]
    logsig = jnp.minimum(pre, 0.0) - jnp.log1p(jnp.exp(-jnp.abs(pre)))
    gfb_ref[...] = logsig * (1.0 / GLA_GATE_NORM)


def _in_proj(h, g, w_main, w_zg, wg3, bg):
    rows = h.shape[0]
    tm = ROW_TILE
    row = lambda w: pl.BlockSpec((tm, w), lambda i: (i, 0))
    return pl.pallas_call(
        _in_proj_kernel,
        grid=(rows // tm,),
        in_specs=[row(D_MODEL), _const_spec(g.shape), _const_spec(w_main.shape),
                  _const_spec(w_zg.shape), _const_spec(wg3.shape), _const_spec(bg.shape)],
        out_specs=[row(1024), row(512), row(512), row(512), row(512)],
        out_shape=[jax.ShapeDtypeStruct((rows, w), jnp.float32) for w in (1024, 512, 512, 512, 512)],
        compiler_params=pltpu.CompilerParams(
            dimension_semantics=("arbitrary",), vmem_limit_bytes=VMEM_LIMIT),
        name="in_proj",
    )(h, g, w_main, w_zg, wg3, bg)


def _lru_kernel(*refs, reverse, n_tiles):
    if reverse:
        (xc_ref, xp_ref, xn_ref, cw_ref, cb_ref, wg_ref, ba_ref, bx_ref, lam_ref,
         hf_ref, gate_ref, out_ref, a_scr, u_scr, carry_ref) = refs
    else:
        (xc_ref, xp_ref, xn_ref, cw_ref, cb_ref, wg_ref, ba_ref, bx_ref, lam_ref,
         out_ref, a_scr, u_scr, carry_ref) = refs
    tt = xc_ref.shape[0]
    step = pl.program_id(1)
    tile = (n_tiles - 1 - step) if reverse else step

    @pl.when(step == 0)
    def _():
        carry_ref[...] = jnp.zeros_like(carry_ref)

    x = xc_ref[...]
    row = lax.broadcasted_iota(jnp.int32, x.shape, 0)
    prev = jnp.where(tile == 0, 0.0, xp_ref[...])
    nxt = jnp.where(tile == n_tiles - 1, 0.0, xn_ref[...])
    xm1 = jnp.where(row == 0, prev[7:8, :], pltpu.roll(x, 1, axis=0))
    xm2 = jnp.where(row == 0, prev[6:7, :],
                    jnp.where(row == 1, prev[7:8, :], pltpu.roll(x, 2, axis=0)))
    xp1 = jnp.where(row == tt - 1, nxt[0:1, :], pltpu.roll(x, tt - 1, axis=0))
    cw = cw_ref[...]
    xc = (cw[0:1, :] * xm2 + cw[1:2, :] * xm1 + cw[2:3, :] * x + cw[3:4, :] * xp1
          + cb_ref[...])

    xcb = xc.astype(jnp.bfloat16)
    half = LRU_WIDTH // 2
    p0 = jnp.dot(xcb[:, :half], wg_ref[0], preferred_element_type=jnp.float32)
    p1 = jnp.dot(xcb[:, half:], wg_ref[1], preferred_element_type=jnp.float32)
    r_pre = jnp.concatenate([p0[:, :half], p1[:, :half]], axis=1)
    i_pre = jnp.concatenate([p0[:, half:], p1[:, half:]], axis=1)
    r = _sigmoid(r_pre + ba_ref[...])
    gi = _sigmoid(i_pre + bx_ref[...])
    log_a = (-LRU_C) * r * _softplus(-lam_ref[...])
    a = jnp.exp(log_a)
    u = jnp.sqrt(-jnp.tanh(log_a) * (1.0 + a * a)) * (gi * xc)
    if not reverse:
        u = jnp.where(row + tile * tt >= PADF, u, 0.0)

    r8 = row & (SUBLANES - 1)
    for d in (1, 2, 4):
        shift = (tt - d) if reverse else d
        a_s = pltpu.roll(a, shift, axis=0)
        u_s = pltpu.roll(u, shift, axis=0)
        m = (r8 <= SUBLANES - 1 - d) if reverse else (r8 >= d)
        u = jnp.where(m, a * u_s + u, u)
        a = jnp.where(m, a * a_s, a)
    a_scr[...] = a
    u_scr[...] = u

    n_groups = tt // SUBLANES

    def body(j, c):
        g = (n_groups - 1 - j) if reverse else j
        i = pl.multiple_of(g * SUBLANES, SUBLANES)
        hg = a_scr[pl.ds(i, SUBLANES), :] * c + u_scr[pl.ds(i, SUBLANES), :]
        u_scr[pl.ds(i, SUBLANES), :] = hg
        return hg[0:1, :] if reverse else hg[SUBLANES - 1:SUBLANES, :]

    carry_ref[...] = lax.fori_loop(0, n_groups, body, carry_ref[...], unroll=8)

    hs = u_scr[...]
    if reverse:
        gt = gate_ref[...]
        gelu = 0.5 * gt * (1.0 + jnp.tanh(0.7978845608028654 * (gt + 0.044715 * gt * gt * gt)))
        out_ref[...] = ((hf_ref[...] + hs) * gelu).astype(out_ref.dtype)
    else:
        out_ref[...] = hs


def _lru_pass(xg, conv_w, conv_b, wgate, ba, bx, lam, *, reverse, hf=None):
    bsz, tp, _ = xg.shape
    tt = TIME_TILE
    nt = tp // tt
    g8 = tt // SUBLANES
    n8 = tp // SUBLANES
    tsel = (lambda t: nt - 1 - t) if reverse else (lambda t: t)
    cur = pl.BlockSpec((None, tt, LRU_WIDTH), lambda b, t: (b, tsel(t), 0))
    prv = pl.BlockSpec((None, SUBLANES, LRU_WIDTH),
                       lambda b, t: (b, jnp.maximum(tsel(t) * g8 - 1, 0), 0))
    nxt = pl.BlockSpec((None, SUBLANES, LRU_WIDTH),
                       lambda b, t: (b, jnp.minimum((tsel(t) + 1) * g8, n8 - 1), 0))
    in_specs = [cur, prv, nxt] + [_const_spec(a.shape) for a in (conv_w, conv_b, wgate, ba, bx, lam)]
    args = [xg, xg, xg, conv_w, conv_b, wgate, ba, bx, lam]
    if reverse:
        in_specs += [cur, pl.BlockSpec((None, tt, LRU_WIDTH), lambda b, t: (b, tsel(t), 1))]
        args += [hf, xg]
        out_dtype = jnp.bfloat16
    else:
        out_dtype = jnp.float32
    return pl.pallas_call(
        functools.partial(_lru_kernel, reverse=reverse, n_tiles=nt),
        grid=(bsz, nt),
        in_specs=in_specs,
        out_specs=cur,
        out_shape=jax.ShapeDtypeStruct((bsz, tp, LRU_WIDTH), out_dtype),
        scratch_shapes=[pltpu.VMEM((tt, LRU_WIDTH), jnp.float32),
                        pltpu.VMEM((tt, LRU_WIDTH), jnp.float32),
                        pltpu.VMEM((1, LRU_WIDTH), jnp.float32)],
        compiler_params=pltpu.CompilerParams(
            dimension_semantics=("arbitrary", "arbitrary"), vmem_limit_bytes=VMEM_LIMIT),
        name="lru_bwd" if reverse else "lru_fwd",
    )(*args)


def _gla_constants(reverse):
    c = CHUNK
    mats = []
    level = np.full((c, c), N_LEVELS + 1, np.int32)
    for lv in range(N_LEVELS):
        m = c >> (lv + 1)
        mat = np.zeros((c, c), np.float32)
        for t in range(c):
            blk = (t // (2 * m)) * (2 * m)
            mid = blk + m
            if t >= mid:
                mat[t, mid:t + 1] = 1.0
                level[t, blk:mid] = lv
            else:
                mat[t, t + 1:mid] = 1.0
        mats.append(mat)
    mats.append(np.tril(np.ones((c, c), np.float32)))
    mats.append(np.triu(np.ones((c, c), np.float32), 1))
    if not reverse:
        level[np.arange(c), np.arange(c)] = N_LEVELS
    lall = np.concatenate(mats, axis=0)
    if reverse:
        lall = lall.reshape(N_LEVELS + 2, c, c)[:, ::-1, ::-1].reshape(-1, c)
        level = level[::-1, ::-1]
    level = np.tile(level, (1, GLA_HEADS))
    return jnp.asarray(lall, jnp.bfloat16), jnp.asarray(np.ascontiguousarray(level))


def _gla_kernel(*refs, reverse, n_tiles):
    if reverse:
        (qk_ref, v_ref, g_ref, lall_ref, lvl_ref, of_ref, go_ref, hn_ref,
         out_ref, s_ref) = refs
    else:
        qk_ref, v_ref, g_ref, lall_ref, lvl_ref, out_ref, s_ref = refs
    tt = qk_ref.shape[0]
    n_chunks = tt // CHUNK
    step = pl.program_id(1)

    @pl.when(step == 0)
    def _():
        s_ref[...] = jnp.zeros_like(s_ref)

    lall = lall_ref[...]
    lvl = lvl_ref[...]
    lane_head = lax.shift_right_logical(
        lax.broadcasted_iota(jnp.int32, (CHUNK, GLA_QK), 1), GLA_DK.bit_length() - 1)
    last = 0 if reverse else CHUNK - 1

    def chunk_body(j, carry):
        cidx = (n_chunks - 1 - j) if reverse else j
        r0 = pl.multiple_of(cidx * CHUNK, CHUNK)
        rows = pl.ds(r0, CHUNK)
        q = qk_ref[rows, 0:GLA_QK]
        k = qk_ref[rows, GLA_QK:2 * GLA_QK]
        v = v_ref[rows, :]
        vb = v.astype(jnp.bfloat16)
        g = g_ref[rows, :]
        ghi, glo = _split_bf16(g)
        e2 = jnp.dot(lall, jnp.concatenate([ghi, glo], axis=1),
                     preferred_element_type=jnp.float32)
        dec = jnp.exp(e2[:, :GLA_QK] + e2[:, GLA_QK:])

        scores = jnp.zeros((CHUNK, GLA_HEADS * CHUNK), jnp.float32)
        n_terms = N_LEVELS if reverse else N_LEVELS + 1
        for lv in range(n_terms):
            if lv < N_LEVELS:
                f = dec[lv * CHUNK:(lv + 1) * CHUNK, :]
                qt = (q * f).astype(jnp.bfloat16)
                kt = k * f
            else:
                qt = q.astype(jnp.bfloat16)
                kt = k
            kstack = jnp.concatenate(
                [jnp.where(lane_head == hd, kt, 0.0) for hd in range(GLA_HEADS)],
                axis=0).astype(jnp.bfloat16)
            p = lax.dot_general(qt, kstack, (((1,), (1,)), ((), ())),
                                preferred_element_type=jnp.float32)
            scores = jnp.where(lvl == lv, p, scores)
        sb = scores.astype(jnp.bfloat16)

        b_dec = dec[N_LEVELS * CHUNK:(N_LEVELS + 1) * CHUNK, :]
        r_dec = dec[(N_LEVELS + 1) * CHUNK:(N_LEVELS + 2) * CHUNK, :]
        qe = (q * b_dec).astype(jnp.bfloat16)
        ke = (k * r_dec).astype(jnp.bfloat16)
        e_last = b_dec[last:last + 1, :]
        outs = []
        for hd in range(GLA_HEADS):
            ks = slice(hd * GLA_DK, (hd + 1) * GLA_DK)
            vs = slice(hd * GLA_DV, (hd + 1) * GLA_DV)
            st = s_ref[hd]
            intra = jnp.dot(sb[:, hd * CHUNK:(hd + 1) * CHUNK], vb[:, vs],
                            preferred_element_type=jnp.float32)
            inter = lax.dot_general(qe[:, ks], st.astype(jnp.bfloat16),
                                    (((1,), (1,)), ((), ())),
                                    preferred_element_type=jnp.float32)
            outs.append(intra + inter)
            kv = lax.dot_general(vb[:, vs], ke[:, ks], (((0,), (0,)), ((), ())),
                                 preferred_element_type=jnp.float32)
            s_ref[hd] = st * e_last[:, ks] + kv
        o = jnp.concatenate(outs, axis=1)
        if reverse:
            o = o + of_ref[rows, :]
            parts = []
            for hd in range(GLA_HEADS):
                oh = o[:, hd * GLA_DV:(hd + 1) * GLA_DV]
                parts.append(oh * _rms_scale(oh))
            gt = go_ref[rows, :]
            y = jnp.concatenate(parts, axis=1) * hn_ref[...] * (gt * _sigmoid(gt))
            out_ref[rows, :] = y.astype(out_ref.dtype)
        else:
            out_ref[rows, :] = o
        return carry

    lax.fori_loop(0, n_chunks, chunk_body, 0)


def _gla_pass(qk, v, gfb, *, reverse, o_f=None, go=None, head_norm=None):
    bsz, tp, _ = qk.shape
    tt = TIME_TILE
    nt = tp // tt
    tsel = (lambda t: nt - 1 - t) if reverse else (lambda t: t)
    blk = lambda w, c=0: pl.BlockSpec((None, tt, w), lambda b, t: (b, tsel(t), c))
    lall, lvl = _gla_constants(reverse)
    in_specs = [blk(512), blk(512), blk(GLA_QK, 1 if reverse else 0),
                _const_spec(lall.shape), _const_spec(lvl.shape)]
    args = [qk, v, gfb, lall, lvl]
    if reverse:
        in_specs += [blk(512), blk(512), _const_spec(head_norm.shape)]
        args += [o_f, go, head_norm]
        out_dtype = jnp.bfloat16
    else:
        out_dtype = jnp.float32
    return pl.pallas_call(
        functools.partial(_gla_kernel, reverse=reverse, n_tiles=nt),
        grid=(bsz, nt),
        in_specs=in_specs,
        out_specs=blk(512),
        out_shape=jax.ShapeDtypeStruct((bsz, tp, GLA_WIDTH), out_dtype),
        scratch_shapes=[pltpu.VMEM((GLA_HEADS, GLA_DV, GLA_DK), jnp.float32)],
        compiler_params=pltpu.CompilerParams(
            dimension_semantics=("arbitrary", "arbitrary"), vmem_limit_bytes=VMEM_LIMIT),
        name="gla_bwd" if reverse else "gla_fwd",
    )(*args)


def _out_mlp_kernel(h_ref, yl_ref, yg_ref, wol_ref, wog_ref, gpost_ref, gpre_ref, gpost2_ref,
                    wup_ref, wdn_ref, out_ref, acc_ref, *, rows_per_batch, n_batch):
    tm = h_ref.shape[0]
    mix = (jnp.dot(yl_ref[...], wol_ref[...], preferred_element_type=jnp.float32)
           + jnp.dot(yg_ref[...], wog_ref[...], preferred_element_type=jnp.float32))
    row = lax.broadcasted_iota(jnp.int32, (tm, 1), 0) + pl.program_id(0) * tm
    real = row >= PADF
    for b in range(1, n_batch):
        real = real & ((row < b * rows_per_batch) | (row >= b * rows_per_batch + PADF))
    h1 = h_ref[...] + jnp.where(real, mix * _rms_scale(mix) * gpost_ref[...], 0.0)
    xn = (h1 * _rms_scale(h1) * gpre_ref[...]).astype(jnp.bfloat16)
    for c in range(D_FF // FF_TILE):
        cs = slice(c * FF_TILE, (c + 1) * FF_TILE)
        up = jnp.dot(xn, wup_ref[:, cs], preferred_element_type=jnp.float32)
        act = jnp.square(jnp.maximum(up, 0.0)).astype(jnp.bfloat16)
        part = jnp.dot(act, wdn_ref[cs, :], preferred_element_type=jnp.float32)
        if c == 0:
            acc_ref[...] = part
        else:
            acc_ref[...] += part
    ff = acc_ref[...]
    out_ref[...] = h1 + ff * _rms_scale(ff) * gpost2_ref[...]


def _out_mlp(h, y_lru, y_gla, wo_l, wo_g, g_post, g_pre, g_post2, w_up, w_dn, rows_per_batch):
    rows = h.shape[0]
    tm = ROW_TILE
    row = lambda w: pl.BlockSpec((tm, w), lambda i: (i, 0))
    consts = (wo_l, wo_g, g_post, g_pre, g_post2, w_up, w_dn)
    return pl.pallas_call(
        functools.partial(_out_mlp_kernel, rows_per_batch=rows_per_batch,
                          n_batch=rows // rows_per_batch),
        grid=(rows // tm,),
        in_specs=[row(D_MODEL), row(512), row(512)] + [_const_spec(a.shape) for a in consts],
        out_specs=row(D_MODEL),
        out_shape=jax.ShapeDtypeStruct((rows, D_MODEL), jnp.float32),
        scratch_shapes=[pltpu.VMEM((tm, D_MODEL), jnp.float32)],
        compiler_params=pltpu.CompilerParams(
            dimension_semantics=("arbitrary",), vmem_limit_bytes=VMEM_LIMIT),
        name="out_mlp",
    )(h, y_lru, y_gla, *consts)


def _block_diag_gates(wa, wx):
    per_half = LRU_HEADS // 2
    eye = jnp.eye(per_half, dtype=wa.dtype)

    def bd(w):
        w4 = w.reshape(2, per_half, LRU_HEAD_DIM, LRU_HEAD_DIM)
        return jnp.einsum('hjil,jk->hjikl', w4, eye).reshape(2, LRU_WIDTH // 2, LRU_WIDTH // 2)
    return jnp.concatenate([bd(wa), bd(wx)], axis=2).astype(jnp.bfloat16)


def kernel(x, meta_tokens, norm_mix_pre, norm_mix_post, norm_mlp_pre, norm_mlp_post,
           w_in, conv_w, conv_b,
           lru_wa_f, lru_ba_f, lru_wx_f, lru_bx_f, lru_lambda_f,
           lru_wa_b, lru_ba_b, lru_wx_b, lru_bx_b, lru_lambda_b,
           gla_wg_f, gla_bg_f, gla_wg_b, gla_bg_b, gla_head_norm,
           w_out, w_mlp_up, w_mlp_down):
    bsz, seq, d = x.shape
    depth = w_in.shape[0]
    tp = seq + T0
    assert d == D_MODEL and tp % TIME_TILE == 0 and (bsz * tp) % ROW_TILE == 0
    meta = jnp.broadcast_to(meta_tokens.astype(x.dtype)[None], (bsz, N_META, d))
    h = jnp.concatenate([jnp.zeros((bsz, PADF, d), x.dtype), meta, x], axis=1)
    h = h.reshape(bsz * tp, d)
    vec = lambda a: a.reshape(1, -1)
    bf = lambda a: a.astype(jnp.bfloat16)

    for l in range(depth):
        w_main = bf(w_in[l][:, :2560])
        w_zg = bf(jnp.tile(w_in[l][:, 2560:2592], (1, 3)))
        wg = jnp.zeros((2 * GLA_RANK, 2 * GLA_QK), jnp.float32)
        wg = wg.at[:GLA_RANK, :GLA_QK].set(gla_wg_f[l]).at[GLA_RANK:, GLA_QK:].set(gla_wg_b[l])
        wg_hi, wg_lo = _split_bf16(wg)
        wg3 = jnp.concatenate([wg_hi, wg_lo, wg_hi], axis=0)
        bg = jnp.concatenate([gla_bg_f[l], gla_bg_b[l]]).reshape(1, -1)

        xg, qk, v, go, gfb = _in_proj(h, vec(norm_mix_pre[l]), w_main, w_zg, wg3, bg)
        r3 = lambda a: a.reshape(bsz, tp, a.shape[-1])
        xg3, qk3, v3, go3, gfb3 = r3(xg), r3(qk), r3(v), r3(go), r3(gfb)

        h_f = _lru_pass(xg3, conv_w[l], vec(conv_b[l]),
                        _block_diag_gates(lru_wa_f[l], lru_wx_f[l]),
                        vec(lru_ba_f[l]), vec(lru_bx_f[l]), vec(lru_lambda_f[l]), reverse=False)
        y_lru = _lru_pass(xg3, conv_w[l], vec(conv_b[l]),
                          _block_diag_gates(lru_wa_b[l], lru_wx_b[l]),
                          vec(lru_ba_b[l]), vec(lru_bx_b[l]), vec(lru_lambda_b[l]),
                          reverse=True, hf=h_f)
        o_f = _gla_pass(qk3, v3, gfb3, reverse=False)
        y_gla = _gla_pass(qk3, v3, gfb3, reverse=True, o_f=o_f, go=go3,
                          head_norm=vec(gla_head_norm[l]))

        h = _out_mlp(h, y_lru.reshape(bsz * tp, -1), y_gla.reshape(bsz * tp, -1),
                     bf(w_out[l][:LRU_WIDTH]), bf(w_out[l][LRU_WIDTH:]),
                     vec(norm_mix_post[l]), vec(norm_mlp_pre[l]), vec(norm_mlp_post[l]),
                     bf(w_mlp_up[l]), bf(w_mlp_down[l]), tp)
    return h.reshape(bsz, tp, d)[:, T0:]
```

```python
import functools

import numpy as np
import jax
import jax.numpy as jnp
from jax import lax
from jax.experimental import pallas as pl
from jax.experimental.pallas import tpu as pltpu

D_MODEL = 1024
N_META = 16
CHUNK = 64
LRU_WIDTH = 512
LRU_HEADS = 8
LRU_HEAD_DIM = 64
LRU_C = 8.0
GLA_WIDTH = 512
GLA_HEADS = 4
GLA_DV = 128
GLA_DK = 64
GLA_QK = GLA_HEADS * GLA_DK
GLA_RANK = 16
GLA_GATE_NORM = 16.0
D_FF = 4096
EPS = 1e-6
LOG2_E = 1.4426950408889634

LEAD_CHUNKS = 4
T0 = LEAD_CHUNKS * CHUNK
PADF = T0 - N_META
ROW_TILE = 512
TIME_TILE = 768
GLA_TIME_TILE = 768
FF_TILE = 1024
SUBLANES = 8
VMEM_LIMIT = 56 * 1024 * 1024


def _const_spec(shape):
    nd = len(shape)
    return pl.BlockSpec(shape, lambda *_: (0,) * nd, pipeline_mode=pl.Buffered(1))


def _rms_scale(x):
    return lax.rsqrt(jnp.mean(x * x, axis=-1, keepdims=True) + EPS)


def _sigmoid(x):
    return 1.0 / (1.0 + jnp.exp(-x))


def _softplus(x):
    return jnp.maximum(x, 0.0) + jnp.log1p(jnp.exp(-jnp.abs(x)))


def _split_bf16(x):
    hi = x.astype(jnp.bfloat16)
    lo = (x - hi.astype(jnp.float32)).astype(jnp.bfloat16)
    return hi, lo


def _in_proj_kernel(h_ref, g_ref, w_ref, wzg_ref, wg_ref, bg_ref,
                    xg_ref, qk_ref, v_ref, go_ref, gfb_ref):
    x = h_ref[...]
    xn = (x * _rms_scale(x) * g_ref[...]).astype(jnp.bfloat16)
    z = jnp.dot(xn, w_ref[...], preferred_element_type=jnp.float32)
    xg_ref[...] = z[:, :1024]
    q = z[:, 1024:1280] * (GLA_DK ** -0.5)
    qk_ref[...] = jnp.concatenate([q, z[:, 1280:1536]], axis=1)
    v_ref[...] = z[:, 1536:2048]
    go_ref[...] = z[:, 2048:2560]
    zg = jnp.dot(xn, wzg_ref[...], preferred_element_type=jnp.float32)
    zhi = zg.astype(jnp.bfloat16)
    zlo = (zg - zhi.astype(jnp.float32)).astype(jnp.bfloat16)
    lane = lax.broadcasted_iota(jnp.int32, zg.shape, 1)
    lhs = jnp.where(lane < 4 * GLA_RANK, zhi, zlo)
    pre = jnp.dot(lhs, wg_ref[...], preferred_element_type=jnp.float32) + bg_ref[...]
    logsig = jnp.minimum(pre, 0.0) - jnp.log1p(jnp.exp(-jnp.abs(pre)))
    gfb_ref[...] = logsig * (LOG2_E / GLA_GATE_NORM)


def _in_proj(h, g, w_main, w_zg, wg3, bg):
    rows = h.shape[0]
    tm = ROW_TILE
    row = lambda w: pl.BlockSpec((tm, w), lambda i: (i, 0))
    return pl.pallas_call(
        _in_proj_kernel,
        grid=(rows // tm,),
        in_specs=[row(D_MODEL), _const_spec(g.shape), _const_spec(w_main.shape),
                  _const_spec(w_zg.shape), _const_spec(wg3.shape), _const_spec(bg.shape)],
        out_specs=[row(1024), row(512), row(512), row(512), row(512)],
        out_shape=[jax.ShapeDtypeStruct((rows, w), jnp.float32) for w in (1024, 512, 512, 512, 512)],
        compiler_params=pltpu.CompilerParams(
            dimension_semantics=("arbitrary",), vmem_limit_bytes=VMEM_LIMIT),
        name="in_proj",
    )(h, g, w_main, w_zg, wg3, bg)


def _lru_kernel(*refs, reverse, n_tiles):
    if reverse:
        (xc_ref, xp_ref, xn_ref, cw_ref, cb_ref, wg_ref, ba_ref, bx_ref, lam_ref,
         hf_ref, gate_ref, out_ref, xs_ref, a_scr, u_scr, carry_ref) = refs
    else:
        (xc_ref, xp_ref, xn_ref, cw_ref, cb_ref, wg_ref, ba_ref, bx_ref, lam_ref,
         out_ref, xs_ref, a_scr, u_scr, carry_ref) = refs
    tt = xc_ref.shape[0]
    step = pl.program_id(1)
    tile = (n_tiles - 1 - step) if reverse else step

    @pl.when(step == 0)
    def _():
        carry_ref[...] = jnp.zeros_like(carry_ref)

    x = xc_ref[...]
    row = lax.broadcasted_iota(jnp.int32, x.shape, 0)
    xs_ref[0:SUBLANES, :] = jnp.where(tile == 0, 0.0, xp_ref[...])
    xs_ref[SUBLANES:SUBLANES + tt, :] = x
    xs_ref[SUBLANES + tt:2 * SUBLANES + tt, :] = jnp.where(tile == n_tiles - 1, 0.0, xn_ref[...])
    cw = cw_ref[...]
    xc = (cw[0:1, :] * xs_ref[SUBLANES - 2:SUBLANES - 2 + tt, :]
          + cw[1:2, :] * xs_ref[SUBLANES - 1:SUBLANES - 1 + tt, :]
          + cw[2:3, :] * x
          + cw[3:4, :] * xs_ref[SUBLANES + 1:SUBLANES + 1 + tt, :]
          + cb_ref[...])

    xcb = xc.astype(jnp.bfloat16)
    half = LRU_WIDTH // 2
    p0 = jnp.dot(xcb[:, :half], wg_ref[0], preferred_element_type=jnp.float32)
    p1 = jnp.dot(xcb[:, half:], wg_ref[1], preferred_element_type=jnp.float32)
    r_pre = jnp.concatenate([p0[:, :half], p1[:, :half]], axis=1)
    i_pre = jnp.concatenate([p0[:, half:], p1[:, half:]], axis=1)
    r = _sigmoid(r_pre + ba_ref[...])
    gi = _sigmoid(i_pre + bx_ref[...])
    log_a = (-LRU_C) * r * _softplus(-lam_ref[...])
    a = jnp.exp(log_a)
    u = jnp.sqrt(-jnp.tanh(log_a) * (1.0 + a * a)) * (gi * xc)
    if not reverse:
        u = jnp.where(row + tile * tt >= PADF, u, 0.0)

    n_groups = tt // SUBLANES
    a = a.reshape(n_groups, SUBLANES, LRU_WIDTH)
    u = u.reshape(n_groups, SUBLANES, LRU_WIDTH)
    r8 = lax.broadcasted_iota(jnp.int32, a.shape, 1)
    for d in (1, 2, 4):
        shift = (SUBLANES - d) if reverse else d
        m = (r8 <= SUBLANES - 1 - d) if reverse else (r8 >= d)
        a_s = jnp.where(m, pltpu.roll(a, shift, axis=1), 1.0)
        u_s = jnp.where(m, pltpu.roll(u, shift, axis=1), 0.0)
        u = a * u_s + u
        a = a * a_s
    a_scr[...] = a.reshape(tt, LRU_WIDTH)
    u_scr[...] = u.reshape(tt, LRU_WIDTH)


    def body(j, c):
        g = (n_groups - 1 - j) if reverse else j
        i = pl.multiple_of(g * SUBLANES, SUBLANES)
        hg = a_scr[pl.ds(i, SUBLANES), :] * c + u_scr[pl.ds(i, SUBLANES), :]
        u_scr[pl.ds(i, SUBLANES), :] = hg
        return hg[0:1, :] if reverse else hg[SUBLANES - 1:SUBLANES, :]

    carry_ref[...] = lax.fori_loop(0, n_groups, body, carry_ref[...], unroll=8)

    hs = u_scr[...]
    if reverse:
        gt = gate_ref[...]
        gelu = 0.5 * gt * (1.0 + jnp.tanh(0.7978845608028654 * (gt + 0.044715 * gt * gt * gt)))
        out_ref[...] = ((hf_ref[...] + hs) * gelu).astype(out_ref.dtype)
    else:
        out_ref[...] = hs


def _lru_pass(xg, conv_w, conv_b, wgate, ba, bx, lam, *, reverse, hf=None):
    bsz, tp, _ = xg.shape
    tt = TIME_TILE
    nt = tp // tt
    g8 = tt // SUBLANES
    n8 = tp // SUBLANES
    tsel = (lambda t: nt - 1 - t) if reverse else (lambda t: t)
    cur = pl.BlockSpec((None, tt, LRU_WIDTH), lambda b, t: (b, tsel(t), 0))
    prv = pl.BlockSpec((None, SUBLANES, LRU_WIDTH),
                       lambda b, t: (b, jnp.maximum(tsel(t) * g8 - 1, 0), 0))
    nxt = pl.BlockSpec((None, SUBLANES, LRU_WIDTH),
                       lambda b, t: (b, jnp.minimum((tsel(t) + 1) * g8, n8 - 1), 0))
    in_specs = [cur, prv, nxt] + [_const_spec(a.shape) for a in (conv_w, conv_b, wgate, ba, bx, lam)]
    args = [xg, xg, xg, conv_w, conv_b, wgate, ba, bx, lam]
    if reverse:
        in_specs += [cur, pl.BlockSpec((None, tt, LRU_WIDTH), lambda b, t: (b, tsel(t), 1))]
        args += [hf, xg]
        out_dtype = jnp.bfloat16
    else:
        out_dtype = jnp.float32
    return pl.pallas_call(
        functools.partial(_lru_kernel, reverse=reverse, n_tiles=nt),
        grid=(bsz, nt),
        in_specs=in_specs,
        out_specs=cur,
        out_shape=jax.ShapeDtypeStruct((bsz, tp, LRU_WIDTH), out_dtype),
        scratch_shapes=[pltpu.VMEM((tt + 2 * SUBLANES, LRU_WIDTH), jnp.float32),
                        pltpu.VMEM((tt, LRU_WIDTH), jnp.float32),
                        pltpu.VMEM((tt, LRU_WIDTH), jnp.float32),
                        pltpu.VMEM((1, LRU_WIDTH), jnp.float32)],
        compiler_params=pltpu.CompilerParams(
            dimension_semantics=("arbitrary", "arbitrary"), vmem_limit_bytes=VMEM_LIMIT),
        name="lru_bwd" if reverse else "lru_fwd",
    )(*args)


def _gla_constants(reverse):
    c = CHUNK
    t = np.arange(c)[:, None]
    s = np.arange(c)[None, :]
    gt, gs = t // SUBLANES, s // SUBLANES
    if reverse:
        tri = np.triu(np.ones((c, c), np.float32))
        code = np.where((gt == gs) & (s > t), s - t, np.where(gs > gt, SUBLANES + gs, -1))
    else:
        tri = np.tril(np.ones((c, c), np.float32))
        code = np.where((gt == gs) & (s <= t), t - s, np.where(gs < gt, SUBLANES + gs, -1))
    code = np.tile(code.astype(np.int32), (1, GLA_HEADS))
    hmask = np.kron(np.eye(GLA_HEADS, dtype=np.float32), np.ones((c, GLA_DK), np.float32))
    return jnp.asarray(tri, jnp.bfloat16), jnp.asarray(code), jnp.asarray(hmask, jnp.bfloat16)


def _gla_chunk(q, k, v, g, s_ref, tri, code, hmask, reverse):
    n_groups = CHUNK // SUBLANES
    ghi, glo = _split_bf16(g)
    b2 = jnp.dot(tri, jnp.concatenate([ghi, glo], axis=1), preferred_element_type=jnp.float32)
    b = b2[:, :GLA_QK] + b2[:, GLA_QK:]
    vb = v.astype(jnp.bfloat16)

    def head_stack(x):
        return jnp.concatenate([x.astype(jnp.bfloat16)] * GLA_HEADS, axis=0) * hmask

    nt_dims = (((1,), (1,)), ((), ()))

    anchor_row = (lambda gi: gi * SUBLANES) if reverse else (lambda gi: gi * SUBLANES + SUBLANES - 1)
    anch = [b[anchor_row(gi):anchor_row(gi) + 1, :] for gi in range(n_groups)]
    b_anchor = jnp.concatenate([jnp.broadcast_to(a, (SUBLANES, GLA_QK)) for a in anch], axis=0)

    offsets = list(range(1 if reverse else 0, SUBLANES))
    lhs = []
    for d in offsets:
        if d == 0:
            lhs.append(q)
        else:
            other = pltpu.roll(b, (CHUNK - d) if reverse else d, axis=0)
            lhs.append(q * jnp.exp2(jnp.minimum(b - other, 0.0)))
    p_within = lax.dot_general(jnp.concatenate(lhs, axis=0).astype(jnp.bfloat16), head_stack(k),
                               nt_dims, preferred_element_type=jnp.float32)

    k_anch = k * jnp.exp2(b_anchor - b)
    groups = list(range(1, n_groups)) if reverse else list(range(n_groups - 1))
    row_sel = (lambda gi: slice(0, gi * SUBLANES)) if reverse else \
              (lambda gi: slice((gi + 1) * SUBLANES, CHUNK))
    lhs = [q[row_sel(gi), :] * jnp.exp2(b[row_sel(gi), :] - anch[gi]) for gi in groups]
    p_cross = lax.dot_general(jnp.concatenate(lhs, axis=0).astype(jnp.bfloat16), head_stack(k_anch),
                              nt_dims, preferred_element_type=jnp.float32)

    scores = jnp.zeros((CHUNK, GLA_HEADS * CHUNK), jnp.float32)
    for i, d in enumerate(offsets):
        scores = jnp.where(code == d, p_within[i * CHUNK:(i + 1) * CHUNK, :], scores)
    off = 0
    for gi in groups:
        n = gi * SUBLANES if reverse else CHUNK - (gi + 1) * SUBLANES
        blk = p_cross[off:off + n, :]
        off += n
        pad = jnp.zeros((CHUNK - n, GLA_HEADS * CHUNK), jnp.float32)
        full = jnp.concatenate([blk, pad] if reverse else [pad, blk], axis=0)
        scores = jnp.where(code == SUBLANES + gi, full, scores)
    sb = scores.astype(jnp.bfloat16)

    last = 0 if reverse else CHUNK - 1
    b_last = b[last:last + 1, :]
    qe = (q * jnp.exp2(b)).astype(jnp.bfloat16)
    ke = (k * jnp.exp2(b_last - b)).astype(jnp.bfloat16)
    e_last = jnp.exp2(b_last)
    eye = (lax.broadcasted_iota(jnp.int32, (GLA_DK, GLA_DK), 0)
           == lax.broadcasted_iota(jnp.int32, (GLA_DK, GLA_DK), 1))
    outs = []
    for hd in range(GLA_HEADS):
        ks = slice(hd * GLA_DK, (hd + 1) * GLA_DK)
        vs = slice(hd * GLA_DV, (hd + 1) * GLA_DV)
        st = s_ref[hd]
        intra = jnp.dot(sb[:, hd * CHUNK:(hd + 1) * CHUNK], vb[:, vs],
                        preferred_element_type=jnp.float32)
        inter = jnp.dot(qe[:, ks], st.astype(jnp.bfloat16), preferred_element_type=jnp.float32)
        outs.append(intra + inter)
        kv = lax.dot_general(ke[:, ks], vb[:, vs], (((0,), (0,)), ((), ())),
                             preferred_element_type=jnp.float32)
        e_col = jnp.sum(jnp.where(eye, e_last[:, ks], 0.0), axis=1, keepdims=True)
        s_ref[hd] = st * e_col + kv
    return jnp.concatenate(outs, axis=1)


def _gla_kernel(*refs, reverse):
    if reverse:
        (qk_ref, v_ref, g_ref, tri_ref, code_ref, hmask_ref, of_ref, go_ref, hn_ref,
         out_ref, s_ref) = refs
    else:
        qk_ref, v_ref, g_ref, tri_ref, code_ref, hmask_ref, out_ref, s_ref = refs
    bsz, tt, _ = qk_ref.shape
    n_chunks = tt // CHUNK

    @pl.when(pl.program_id(0) == 0)
    def _():
        s_ref[...] = jnp.zeros_like(s_ref)

    tri = tri_ref[...]
    code = code_ref[...]
    hmask = hmask_ref[...]

    def chunk_body(j, carry):
        cidx = (n_chunks - 1 - j) if reverse else j
        rows = pl.ds(pl.multiple_of(cidx * CHUNK, CHUNK), CHUNK)
        for bi in range(bsz):
            o = _gla_chunk(qk_ref[bi, rows, 0:GLA_QK], qk_ref[bi, rows, GLA_QK:2 * GLA_QK],
                           v_ref[bi, rows, :], g_ref[bi, rows, :], s_ref.at[bi], tri, code, hmask,
                           reverse)
            if reverse:
                o = o + of_ref[bi, rows, :]
                parts = []
                for hd in range(GLA_HEADS):
                    oh = o[:, hd * GLA_DV:(hd + 1) * GLA_DV]
                    parts.append(oh * _rms_scale(oh))
                gt = go_ref[bi, rows, :]
                y = jnp.concatenate(parts, axis=1) * hn_ref[...] * (gt * _sigmoid(gt))
                out_ref[bi, rows, :] = y.astype(out_ref.dtype)
            else:
                out_ref[bi, rows, :] = o
        return carry

    lax.fori_loop(0, n_chunks, chunk_body, 0, unroll=2)


def _gla_pass(qk, v, gfb, *, reverse, o_f=None, go=None, head_norm=None):
    bsz, tp, _ = qk.shape
    tt = GLA_TIME_TILE
    nt = tp // tt
    tsel = (lambda t: nt - 1 - t) if reverse else (lambda t: t)
    blk = lambda w, c=0: pl.BlockSpec((bsz, tt, w), lambda t: (0, tsel(t), c))
    consts = _gla_constants(reverse)
    in_specs = [blk(512), blk(512), blk(GLA_QK, 1 if reverse else 0)]
    in_specs += [_const_spec(a.shape) for a in consts]
    args = [qk, v, gfb, *consts]
    if reverse:
        in_specs += [blk(512), blk(512), _const_spec(head_norm.shape)]
        args += [o_f, go, head_norm]
        out_dtype = jnp.bfloat16
    else:
        out_dtype = jnp.float32
    return pl.pallas_call(
        functools.partial(_gla_kernel, reverse=reverse),
        grid=(nt,),
        in_specs=in_specs,
        out_specs=blk(512),
        out_shape=jax.ShapeDtypeStruct((bsz, tp, GLA_WIDTH), out_dtype),
        scratch_shapes=[pltpu.VMEM((bsz, GLA_HEADS, GLA_DK, GLA_DV), jnp.float32)],
        compiler_params=pltpu.CompilerParams(
            dimension_semantics=("arbitrary",), vmem_limit_bytes=VMEM_LIMIT),
        name="gla_bwd" if reverse else "gla_fwd",
    )(*args)


def _out_mlp_kernel(h_ref, yl_ref, yg_ref, wol_ref, wog_ref, gpost_ref, gpre_ref, gpost2_ref,
                    wup_ref, wdn_ref, out_ref, acc_ref, *, rows_per_batch, n_batch):
    tm = h_ref.shape[0]
    mix = (jnp.dot(yl_ref[...], wol_ref[...], preferred_element_type=jnp.float32)
           + jnp.dot(yg_ref[...], wog_ref[...], preferred_element_type=jnp.float32))
    row = lax.broadcasted_iota(jnp.int32, (tm, 1), 0) + pl.program_id(0) * tm
    real = row >= PADF
    for b in range(1, n_batch):
        real = real & ((row < b * rows_per_batch) | (row >= b * rows_per_batch + PADF))
    h1 = h_ref[...] + jnp.where(real, mix * _rms_scale(mix) * gpost_ref[...], 0.0)
    xn = (h1 * _rms_scale(h1) * gpre_ref[...]).astype(jnp.bfloat16)
    for c in range(D_FF // FF_TILE):
        cs = slice(c * FF_TILE, (c + 1) * FF_TILE)
        up = jnp.dot(xn, wup_ref[:, cs], preferred_element_type=jnp.float32)
        act = jnp.square(jnp.maximum(up, 0.0)).astype(jnp.bfloat16)
        part = jnp.dot(act, wdn_ref[cs, :], preferred_element_type=jnp.float32)
        if c == 0:
            acc_ref[...] = part
        else:
            acc_ref[...] += part
    ff = acc_ref[...]
    out_ref[...] = h1 + ff * _rms_scale(ff) * gpost2_ref[...]


def _out_mlp(h, y_lru, y_gla, wo_l, wo_g, g_post, g_pre, g_post2, w_up, w_dn, rows_per_batch):
    rows = h.shape[0]
    tm = ROW_TILE
    row = lambda w: pl.BlockSpec((tm, w), lambda i: (i, 0))
    consts = (wo_l, wo_g, g_post, g_pre, g_post2, w_up, w_dn)
    return pl.pallas_call(
        functools.partial(_out_mlp_kernel, rows_per_batch=rows_per_batch,
                          n_batch=rows // rows_per_batch),
        grid=(rows // tm,),
        in_specs=[row(D_MODEL), row(512), row(512)] + [_const_spec(a.shape) for a in consts],
        out_specs=row(D_MODEL),
        out_shape=jax.ShapeDtypeStruct((rows, D_MODEL), jnp.float32),
        scratch_shapes=[pltpu.VMEM((tm, D_MODEL), jnp.float32)],
        compiler_params=pltpu.CompilerParams(
            dimension_semantics=("arbitrary",), vmem_limit_bytes=VMEM_LIMIT),
        name="out_mlp",
    )(h, y_lru, y_gla, *consts)


def _block_diag_gates(wa, wx):
    per_half = LRU_HEADS // 2
    eye = jnp.eye(per_half, dtype=wa.dtype)

    def bd(w):
        w4 = w.reshape(2, per_half, LRU_HEAD_DIM, LRU_HEAD_DIM)
        return jnp.einsum('hjil,jk->hjikl', w4, eye).reshape(2, LRU_WIDTH // 2, LRU_WIDTH // 2)
    return jnp.concatenate([bd(wa), bd(wx)], axis=2).astype(jnp.bfloat16)


def kernel(x, meta_tokens, norm_mix_pre, norm_mix_post, norm_mlp_pre, norm_mlp_post,
           w_in, conv_w, conv_b,
           lru_wa_f, lru_ba_f, lru_wx_f, lru_bx_f, lru_lambda_f,
           lru_wa_b, lru_ba_b, lru_wx_b, lru_bx_b, lru_lambda_b,
           gla_wg_f, gla_bg_f, gla_wg_b, gla_bg_b, gla_head_norm,
           w_out, w_mlp_up, w_mlp_down):
    bsz, seq, d = x.shape
    depth = w_in.shape[0]
    tp = seq + T0
    assert d == D_MODEL and tp % TIME_TILE == 0 and tp % GLA_TIME_TILE == 0
    assert (bsz * tp) % ROW_TILE == 0
    meta = jnp.broadcast_to(meta_tokens.astype(x.dtype)[None], (bsz, N_META, d))
    h = jnp.concatenate([jnp.zeros((bsz, PADF, d), x.dtype), meta, x], axis=1)
    h = h.reshape(bsz * tp, d)
    vec = lambda a: a.reshape(1, -1)
    bf = lambda a: a.astype(jnp.bfloat16)

    for l in range(depth):
        w_main = bf(w_in[l][:, :2560])
        w_zg = bf(jnp.tile(w_in[l][:, 2560:2592], (1, 3)))
        wg = jnp.zeros((2 * GLA_RANK, 2 * GLA_QK), jnp.float32)
        wg = wg.at[:GLA_RANK, :GLA_QK].set(gla_wg_f[l]).at[GLA_RANK:, GLA_QK:].set(gla_wg_b[l])
        wg_hi, wg_lo = _split_bf16(wg)
        wg3 = jnp.concatenate([wg_hi, wg_lo, wg_hi], axis=0)
        bg = jnp.concatenate([gla_bg_f[l], gla_bg_b[l]]).reshape(1, -1)

        xg, qk, v, go, gfb = _in_proj(h, vec(norm_mix_pre[l]), w_main, w_zg, wg3, bg)
        r3 = lambda a: a.reshape(bsz, tp, a.shape[-1])
        xg3, qk3, v3, go3, gfb3 = r3(xg), r3(qk), r3(v), r3(go), r3(gfb)

        h_f = _lru_pass(xg3, conv_w[l], vec(conv_b[l]),
                        _block_diag_gates(lru_wa_f[l], lru_wx_f[l]),
                        vec(lru_ba_f[l]), vec(lru_bx_f[l]), vec(lru_lambda_f[l]), reverse=False)
        y_lru = _lru_pass(xg3, conv_w[l], vec(conv_b[l]),
                          _block_diag_gates(lru_wa_b[l], lru_wx_b[l]),
                          vec(lru_ba_b[l]), vec(lru_bx_b[l]), vec(lru_lambda_b[l]),
                          reverse=True, hf=h_f)
        o_f = _gla_pass(qk3, v3, gfb3, reverse=False)
        y_gla = _gla_pass(qk3, v3, gfb3, reverse=True, o_f=o_f, go=go3,
                          head_norm=vec(gla_head_norm[l]))

        h = _out_mlp(h, y_lru.reshape(bsz * tp, -1), y_gla.reshape(bsz * tp, -1),
                     bf(w_out[l][:LRU_WIDTH]), bf(w_out[l][LRU_WIDTH:]),
                     vec(norm_mix_post[l]), vec(norm_mlp_pre[l]), vec(norm_mlp_post[l]),
                     bf(w_mlp_up[l]), bf(w_mlp_down[l]), tp)
    return h.reshape(bsz, tp, d)[:, T0:]
```

```python
import functools

import numpy as np
import jax
import jax.numpy as jnp
from jax import lax
from jax.experimental import pallas as pl
from jax.experimental.pallas import tpu as pltpu

D_MODEL = 1024
N_META = 16
CHUNK = 64
LRU_WIDTH = 512
LRU_HEADS = 8
LRU_HEAD_DIM = 64
LRU_C = 8.0
GLA_WIDTH = 512
GLA_HEADS = 4
GLA_DV = 128
GLA_DK = 64
GLA_QK = GLA_HEADS * GLA_DK
GLA_RANK = 16
GLA_GATE_NORM = 16.0
D_FF = 4096
EPS = 1e-6
LOG2_E = 1.4426950408889634

LEAD_CHUNKS = 4
T0 = LEAD_CHUNKS * CHUNK
PADF = T0 - N_META
ROW_TILE = 512
TIME_TILE = 768
GLA_TIME_TILE = 768
GLA_UNROLL = 3
FF_TILE = 1024
SUBLANES = 8
VMEM_LIMIT = 56 * 1024 * 1024


def _const_spec(shape):
    nd = len(shape)
    return pl.BlockSpec(shape, lambda *_: (0,) * nd, pipeline_mode=pl.Buffered(1))


def _rms_scale(x):
    return lax.rsqrt(jnp.mean(x * x, axis=-1, keepdims=True) + EPS)


def _sigmoid(x):
    return 1.0 / (1.0 + jnp.exp(-x))


def _softplus(x):
    return jnp.maximum(x, 0.0) + jnp.log1p(jnp.exp(-jnp.abs(x)))


def _split_bf16(x):
    hi = x.astype(jnp.bfloat16)
    lo = (x - hi.astype(jnp.float32)).astype(jnp.bfloat16)
    return hi, lo


def _in_proj_kernel(h_ref, hp_ref, hn_ref, g_ref, w_ref, wzg_ref, wg_ref, bg_ref, cw_ref, cb_ref,
                    xc_ref, gate_ref, qk_ref, v_ref, go_ref, gfb_ref, xs_ref, *, n_tiles):
    tm = h_ref.shape[0]
    normed = lambda x: (x * _rms_scale(x) * g_ref[...]).astype(jnp.bfloat16)
    halo = jnp.concatenate([hp_ref[...], hn_ref[...]], axis=0)
    xn = normed(h_ref[...])
    xn_ext = jnp.concatenate([xn, normed(halo)], axis=0)

    zg = jnp.dot(xn, wzg_ref[...], preferred_element_type=jnp.float32)
    zx = jnp.dot(xn_ext, w_ref[:, 0:LRU_WIDTH], preferred_element_type=jnp.float32)
    zhi = zg.astype(jnp.bfloat16)
    zlo = (zg - zhi.astype(jnp.float32)).astype(jnp.bfloat16)
    lane = lax.broadcasted_iota(jnp.int32, zg.shape, 1)
    lhs = jnp.where(lane < 4 * GLA_RANK, zhi, zlo)
    pre = jnp.dot(lhs, wg_ref[...], preferred_element_type=jnp.float32) + bg_ref[...]
    z = jnp.dot(xn, w_ref[:, LRU_WIDTH:], preferred_element_type=jnp.float32)

    xs_ref[0:SUBLANES, :] = zx[tm:tm + SUBLANES, :]
    xs_ref[SUBLANES:SUBLANES + tm, :] = zx[0:tm, :]
    xs_ref[SUBLANES + tm:2 * SUBLANES + tm, :] = jnp.where(
        pl.program_id(0) == n_tiles - 1, 0.0, zx[tm + SUBLANES:tm + 2 * SUBLANES, :])
    cw = cw_ref[...]
    xc_ref[...] = (cw[0:1, :] * xs_ref[SUBLANES - 2:SUBLANES - 2 + tm, :]
                   + cw[1:2, :] * xs_ref[SUBLANES - 1:SUBLANES - 1 + tm, :]
                   + cw[2:3, :] * xs_ref[SUBLANES:SUBLANES + tm, :]
                   + cw[3:4, :] * xs_ref[SUBLANES + 1:SUBLANES + 1 + tm, :]
                   + cb_ref[...])
    logsig = jnp.minimum(pre, 0.0) - jnp.log1p(jnp.exp(-jnp.abs(pre)))
    gfb_ref[...] = logsig * (LOG2_E / GLA_GATE_NORM)

    gate_ref[...] = z[:, 0:512].astype(jnp.bfloat16)
    q = z[:, 512:768] * (GLA_DK ** -0.5)
    qk_ref[...] = jnp.concatenate([q, z[:, 768:1024]], axis=1).astype(jnp.bfloat16)
    v_ref[...] = z[:, 1024:1536].astype(jnp.bfloat16)
    go_ref[...] = z[:, 1536:2048].astype(jnp.bfloat16)


def _in_proj(h, g, w_main, w_zg, wg3, bg, conv_w, conv_b):
    rows = h.shape[0]
    tm = ROW_TILE
    nt = rows // tm
    g8 = tm // SUBLANES
    row = lambda w: pl.BlockSpec((tm, w), lambda i: (i, 0))
    before = pl.BlockSpec((SUBLANES, D_MODEL), lambda i: (jnp.maximum(i * g8 - 1, 0), 0))
    after = pl.BlockSpec((SUBLANES, D_MODEL),
                         lambda i: (jnp.minimum((i + 1) * g8, rows // SUBLANES - 1), 0))
    consts = (g, w_main, w_zg, wg3, bg, conv_w, conv_b)
    dtypes = (jnp.float32, jnp.bfloat16, jnp.bfloat16, jnp.bfloat16, jnp.bfloat16, jnp.float32)
    return pl.pallas_call(
        functools.partial(_in_proj_kernel, n_tiles=nt),
        grid=(nt,),
        in_specs=[row(D_MODEL), before, after] + [_const_spec(a.shape) for a in consts],
        out_specs=[row(512)] * len(dtypes),
        out_shape=[jax.ShapeDtypeStruct((rows, 512), dt) for dt in dtypes],
        scratch_shapes=[pltpu.VMEM((tm + 2 * SUBLANES, LRU_WIDTH), jnp.float32)],
        compiler_params=pltpu.CompilerParams(
            dimension_semantics=("arbitrary",), vmem_limit_bytes=VMEM_LIMIT),
        name="in_proj",
    )(h, h, h, *consts)


def _lru_kernel(*refs, reverse, n_tiles):
    if reverse:
        (xc_ref, wg_ref, ba_ref, bx_ref, lam_ref, hf_ref, gate_ref,
         out_ref, a_scr, u_scr, carry_ref) = refs
    else:
        xc_ref, wg_ref, ba_ref, bx_ref, lam_ref, out_ref, a_scr, u_scr, carry_ref = refs
    tt = xc_ref.shape[0]
    step = pl.program_id(1)
    tile = (n_tiles - 1 - step) if reverse else step

    @pl.when(step == 0)
    def _():
        carry_ref[...] = jnp.zeros_like(carry_ref)

    xc = xc_ref[...]
    row = lax.broadcasted_iota(jnp.int32, xc.shape, 0)
    xcb = xc.astype(jnp.bfloat16)
    half = LRU_WIDTH // 2
    p0 = jnp.dot(xcb[:, :half], wg_ref[0], preferred_element_type=jnp.float32)
    p1 = jnp.dot(xcb[:, half:], wg_ref[1], preferred_element_type=jnp.float32)
    r_pre = jnp.concatenate([p0[:, :half], p1[:, :half]], axis=1)
    i_pre = jnp.concatenate([p0[:, half:], p1[:, half:]], axis=1)
    r = _sigmoid(r_pre + ba_ref[...])
    gi = _sigmoid(i_pre + bx_ref[...])
    log_a = (-LRU_C) * r * _softplus(-lam_ref[...])
    a = jnp.exp(log_a)
    u = jnp.sqrt(-jnp.tanh(log_a) * (1.0 + a * a)) * (gi * xc)
    if not reverse:
        u = jnp.where(row + tile * tt >= PADF, u, 0.0)

    n_groups = tt // SUBLANES
    a = a.reshape(n_groups, SUBLANES, LRU_WIDTH)
    u = u.reshape(n_groups, SUBLANES, LRU_WIDTH)
    r8 = lax.broadcasted_iota(jnp.int32, a.shape, 1)
    for d in (1, 2, 4):
        shift = (SUBLANES - d) if reverse else d
        m = (r8 <= SUBLANES - 1 - d) if reverse else (r8 >= d)
        a_s = jnp.where(m, pltpu.roll(a, shift, axis=1), 1.0)
        u_s = jnp.where(m, pltpu.roll(u, shift, axis=1), 0.0)
        u = a * u_s + u
        a = a * a_s
    a_scr[...] = a.reshape(tt, LRU_WIDTH)
    u_scr[...] = u.reshape(tt, LRU_WIDTH)


    def body(j, c):
        g = (n_groups - 1 - j) if reverse else j
        i = pl.multiple_of(g * SUBLANES, SUBLANES)
        hg = a_scr[pl.ds(i, SUBLANES), :] * c + u_scr[pl.ds(i, SUBLANES), :]
        u_scr[pl.ds(i, SUBLANES), :] = hg
        return hg[0:1, :] if reverse else hg[SUBLANES - 1:SUBLANES, :]

    carry_ref[...] = lax.fori_loop(0, n_groups, body, carry_ref[...], unroll=8)

    hs = u_scr[...]
    if reverse:
        gt = gate_ref[...].astype(jnp.float32)
        gelu = 0.5 * gt * (1.0 + jnp.tanh(0.7978845608028654 * (gt + 0.044715 * gt * gt * gt)))
        out_ref[...] = ((hf_ref[...] + hs) * gelu).astype(out_ref.dtype)
    else:
        out_ref[...] = hs


def _lru_pass(xc, wgate, ba, bx, lam, *, reverse, hf=None, gate=None):
    bsz, tp, _ = xc.shape
    tt = TIME_TILE
    nt = tp // tt
    tsel = (lambda t: nt - 1 - t) if reverse else (lambda t: t)
    cur = pl.BlockSpec((None, tt, LRU_WIDTH), lambda b, t: (b, tsel(t), 0))
    in_specs = [cur] + [_const_spec(a.shape) for a in (wgate, ba, bx, lam)]
    args = [xc, wgate, ba, bx, lam]
    if reverse:
        in_specs += [cur, cur]
        args += [hf, gate]
        out_dtype = jnp.bfloat16
    else:
        out_dtype = jnp.float32
    return pl.pallas_call(
        functools.partial(_lru_kernel, reverse=reverse, n_tiles=nt),
        grid=(bsz, nt),
        in_specs=in_specs,
        out_specs=cur,
        out_shape=jax.ShapeDtypeStruct((bsz, tp, LRU_WIDTH), out_dtype),
        scratch_shapes=[pltpu.VMEM((tt, LRU_WIDTH), jnp.float32),
                        pltpu.VMEM((tt, LRU_WIDTH), jnp.float32),
                        pltpu.VMEM((1, LRU_WIDTH), jnp.float32)],
        compiler_params=pltpu.CompilerParams(
            dimension_semantics=("arbitrary", "arbitrary"), vmem_limit_bytes=VMEM_LIMIT),
        name="lru_bwd" if reverse else "lru_fwd",
    )(*args)


def _gla_constants(reverse):
    c, w = CHUNK, SUBLANES
    t = np.arange(c)[:, None]
    s = np.arange(c)[None, :]
    gt, gs = t // w, s // w
    if reverse:
        code = np.where((gt == gs) & (s > t), s - t, np.where(gs > gt, w + gs, -1))
        blocks = [s >= t] + [(s >= t) & (s < t + d) for d in range(1, w)]
    else:
        code = np.where((gt == gs) & (s <= t), t - s, np.where(gs < gt, w + gs, -1))
        blocks = [s <= t] + [(s > t - d) & (s <= t) for d in range(1, w)]
    lmat = np.concatenate([blk.astype(np.float32) for blk in blocks], axis=0)
    lmat = np.concatenate([lmat, lmat], axis=1)
    code = np.tile(code.astype(np.int32), (1, GLA_HEADS))
    hmask = np.kron(np.eye(GLA_HEADS, dtype=np.float32), np.ones((c, GLA_DK), np.float32))
    return jnp.asarray(lmat, jnp.bfloat16), jnp.asarray(code), jnp.asarray(hmask, jnp.bfloat16)


def _gla_decays(g, lmat):
    ghi, glo = _split_bf16(g)
    return jnp.dot(lmat, jnp.concatenate([ghi, glo], axis=0), preferred_element_type=jnp.float32)


def _gla_scores(q, k, logdec, hmask, reverse):
    n_groups = CHUNK // SUBLANES

    def head_stack(x):
        return jnp.concatenate([x.astype(jnp.bfloat16)] * GLA_HEADS, axis=0) * hmask

    nt_dims = (((1,), (1,)), ((), ()))
    b = logdec[0:CHUNK, :]

    lhs = [q] if not reverse else []
    for d in range(1, SUBLANES):
        lhs.append(q * jnp.exp2(logdec[d * CHUNK:(d + 1) * CHUNK, :]))
    p_within = lax.dot_general(jnp.concatenate(lhs, axis=0).astype(jnp.bfloat16), head_stack(k),
                               nt_dims, preferred_element_type=jnp.float32)

    anchor_row = (lambda gi: gi * SUBLANES) if reverse else (lambda gi: gi * SUBLANES + SUBLANES - 1)
    anch = [b[anchor_row(gi):anchor_row(gi) + 1, :] for gi in range(n_groups)]
    b_anchor = jnp.concatenate([jnp.broadcast_to(a, (SUBLANES, GLA_QK)) for a in anch], axis=0)
    groups = list(range(1, n_groups)) if reverse else list(range(n_groups - 1))
    row_sel = (lambda gi: slice(0, gi * SUBLANES)) if reverse else \
              (lambda gi: slice((gi + 1) * SUBLANES, CHUNK))
    lhs = [q[row_sel(gi), :] * jnp.exp2(b[row_sel(gi), :] - anch[gi]) for gi in groups]
    k_anch = k * jnp.exp2(b_anchor - b)
    p_cross = lax.dot_general(jnp.concatenate(lhs, axis=0).astype(jnp.bfloat16), head_stack(k_anch),
                              nt_dims, preferred_element_type=jnp.float32)

    last = 0 if reverse else CHUNK - 1
    b_last = b[last:last + 1, :]
    qe = (q * jnp.exp2(b)).astype(jnp.bfloat16)
    ke = (k * jnp.exp2(b_last - b)).astype(jnp.bfloat16)
    return p_within, p_cross, qe, ke, jnp.exp2(b_last)


def _gla_apply(p_within, p_cross, qe, ke, e_last, vb, s_ref, code, reverse):
    n_groups = CHUNK // SUBLANES
    offsets = list(range(1 if reverse else 0, SUBLANES))
    groups = list(range(1, n_groups)) if reverse else list(range(n_groups - 1))
    scores = jnp.zeros((CHUNK, GLA_HEADS * CHUNK), jnp.float32)
    for i, d in enumerate(offsets):
        scores = jnp.where(code == d, p_within[i * CHUNK:(i + 1) * CHUNK, :], scores)
    off = 0
    for gi in groups:
        n = gi * SUBLANES if reverse else CHUNK - (gi + 1) * SUBLANES
        blk = p_cross[off:off + n, :]
        off += n
        pad = jnp.zeros((CHUNK - n, GLA_HEADS * CHUNK), jnp.float32)
        full = jnp.concatenate([blk, pad] if reverse else [pad, blk], axis=0)
        scores = jnp.where(code == SUBLANES + gi, full, scores)
    sb = scores.astype(jnp.bfloat16)

    eye = (lax.broadcasted_iota(jnp.int32, (GLA_DK, GLA_DK), 0)
           == lax.broadcasted_iota(jnp.int32, (GLA_DK, GLA_DK), 1))
    zero_v = jnp.zeros((CHUNK, GLA_DV), jnp.bfloat16)
    zero_s = jnp.zeros((GLA_DK, GLA_DV), jnp.bfloat16)
    outs = []
    for pair in range(GLA_HEADS // 2):
        heads = (2 * pair, 2 * pair + 1)
        vh = [vb[:, hd * GLA_DV:(hd + 1) * GLA_DV] for hd in heads]
        st = [s_ref[hd] for hd in heads]
        sh = [s.astype(jnp.bfloat16) for s in st]
        lanes = slice(pair * 2 * CHUNK, (pair + 1) * 2 * CHUNK)
        lhs = jnp.concatenate([sb[:, lanes], qe[:, lanes]], axis=1)
        rhs = jnp.concatenate([jnp.concatenate([vh[0], zero_v], axis=1),
                               jnp.concatenate([zero_v, vh[1]], axis=1),
                               jnp.concatenate([sh[0], zero_s], axis=1),
                               jnp.concatenate([zero_s, sh[1]], axis=1)], axis=0)
        outs.append(jnp.dot(lhs, rhs, preferred_element_type=jnp.float32))
        for i, hd in enumerate(heads):
            ks = slice(hd * GLA_DK, (hd + 1) * GLA_DK)
            kv = lax.dot_general(ke[:, ks], vh[i], (((0,), (0,)), ((), ())),
                                 preferred_element_type=jnp.float32)
            e_col = jnp.sum(jnp.where(eye, e_last[:, ks], 0.0), axis=1, keepdims=True)
            s_ref[hd] = st[i] * e_col + kv
    return jnp.concatenate(outs, axis=1)


def _gla_kernel(*refs, reverse):
    if reverse:
        (qk_ref, v_ref, g_ref, lmat_ref, code_ref, hmask_ref, of_ref, go_ref, hn_ref,
         out_ref, s_ref) = refs
    else:
        qk_ref, v_ref, g_ref, lmat_ref, code_ref, hmask_ref, out_ref, s_ref = refs
    bsz, tt, _ = qk_ref.shape
    n_chunks = tt // CHUNK

    @pl.when(pl.program_id(0) == 0)
    def _():
        s_ref[...] = jnp.zeros_like(s_ref)

    lmat = lmat_ref[...]
    code = code_ref[...]
    hmask = hmask_ref[...]

    def chunk_body(j, carry):
        work = []
        for u in range(GLA_UNROLL):
            jj = j * GLA_UNROLL + u
            cidx = (n_chunks - 1 - jj) if reverse else jj
            rows = pl.ds(pl.multiple_of(cidx * CHUNK, CHUNK), CHUNK)
            work += [(bi, rows) for bi in range(bsz)]
        decs = [_gla_decays(g_ref[bi, rows, :], lmat) for bi, rows in work]
        terms = [_gla_scores(qk_ref[bi, rows, 0:GLA_QK], qk_ref[bi, rows, GLA_QK:2 * GLA_QK],
                             dec, hmask, reverse) for (bi, rows), dec in zip(work, decs)]
        for (bi, rows), term in zip(work, terms):
            o = _gla_apply(*term, v_ref[bi, rows, :].astype(jnp.bfloat16), s_ref.at[bi], code,
                           reverse)
            if reverse:
                o = o + of_ref[bi, rows, :]
                parts = []
                for hd in range(GLA_HEADS):
                    oh = o[:, hd * GLA_DV:(hd + 1) * GLA_DV]
                    parts.append(oh * _rms_scale(oh))
                gt = go_ref[bi, rows, :].astype(jnp.float32)
                y = jnp.concatenate(parts, axis=1) * hn_ref[...] * (gt * _sigmoid(gt))
                out_ref[bi, rows, :] = y.astype(out_ref.dtype)
            else:
                out_ref[bi, rows, :] = o
        return carry

    lax.fori_loop(0, n_chunks // GLA_UNROLL, chunk_body, 0)


def _gla_pass(qk, v, gfb, *, reverse, o_f=None, go=None, head_norm=None):
    bsz, tp, _ = qk.shape
    tt = GLA_TIME_TILE
    nt = tp // tt
    tsel = (lambda t: nt - 1 - t) if reverse else (lambda t: t)
    blk = lambda w, c=0: pl.BlockSpec((bsz, tt, w), lambda t: (0, tsel(t), c))
    consts = _gla_constants(reverse)
    in_specs = [blk(512), blk(512), blk(GLA_QK, 1 if reverse else 0)]
    in_specs += [_const_spec(a.shape) for a in consts]
    args = [qk, v, gfb, *consts]
    if reverse:
        in_specs += [blk(512), blk(512), _const_spec(head_norm.shape)]
        args += [o_f, go, head_norm]
        out_dtype = jnp.bfloat16
    else:
        out_dtype = jnp.float32
    return pl.pallas_call(
        functools.partial(_gla_kernel, reverse=reverse),
        grid=(nt,),
        in_specs=in_specs,
        out_specs=blk(512),
        out_shape=jax.ShapeDtypeStruct((bsz, tp, GLA_WIDTH), out_dtype),
        scratch_shapes=[pltpu.VMEM((bsz, GLA_HEADS, GLA_DK, GLA_DV), jnp.float32)],
        compiler_params=pltpu.CompilerParams(
            dimension_semantics=("arbitrary",), vmem_limit_bytes=VMEM_LIMIT),
        name="gla_bwd" if reverse else "gla_fwd",
    )(*args)


def _out_mlp_kernel(h_ref, yl_ref, yg_ref, wol_ref, wog_ref, gpost_ref, gpre_ref, gpost2_ref,
                    wup_ref, wdn_ref, out_ref, acc_ref, *, rows_per_batch, n_batch):
    tm = h_ref.shape[0]
    mix = (jnp.dot(yl_ref[...], wol_ref[...], preferred_element_type=jnp.float32)
           + jnp.dot(yg_ref[...], wog_ref[...], preferred_element_type=jnp.float32))
    row = lax.broadcasted_iota(jnp.int32, (tm, 1), 0) + pl.program_id(0) * tm
    real = row >= PADF
    for b in range(1, n_batch):
        real = real & ((row < b * rows_per_batch) | (row >= b * rows_per_batch + PADF))
    h1 = h_ref[...] + jnp.where(real, mix * _rms_scale(mix) * gpost_ref[...], 0.0)
    xn = (h1 * _rms_scale(h1) * gpre_ref[...]).astype(jnp.bfloat16)
    for c in range(D_FF // FF_TILE):
        cs = slice(c * FF_TILE, (c + 1) * FF_TILE)
        up = jnp.dot(xn, wup_ref[:, cs], preferred_element_type=jnp.float32)
        act = jnp.square(jnp.maximum(up, 0.0)).astype(jnp.bfloat16)
        part = jnp.dot(act, wdn_ref[cs, :], preferred_element_type=jnp.float32)
        if c == 0:
            acc_ref[...] = part
        else:
            acc_ref[...] += part
    ff = acc_ref[...]
    out_ref[...] = h1 + ff * _rms_scale(ff) * gpost2_ref[...]


def _out_mlp(h, y_lru, y_gla, wo_l, wo_g, g_post, g_pre, g_post2, w_up, w_dn, rows_per_batch):
    rows = h.shape[0]
    tm = ROW_TILE
    row = lambda w: pl.BlockSpec((tm, w), lambda i: (i, 0))
    consts = (wo_l, wo_g, g_post, g_pre, g_post2, w_up, w_dn)
    return pl.pallas_call(
        functools.partial(_out_mlp_kernel, rows_per_batch=rows_per_batch,
                          n_batch=rows // rows_per_batch),
        grid=(rows // tm,),
        in_specs=[row(D_MODEL), row(512), row(512)] + [_const_spec(a.shape) for a in consts],
        out_specs=row(D_MODEL),
        out_shape=jax.ShapeDtypeStruct((rows, D_MODEL), jnp.float32),
        scratch_shapes=[pltpu.VMEM((tm, D_MODEL), jnp.float32)],
        compiler_params=pltpu.CompilerParams(
            dimension_semantics=("arbitrary",), vmem_limit_bytes=VMEM_LIMIT),
        name="out_mlp",
    )(h, y_lru, y_gla, *consts)


def _block_diag_gates(wa, wx):
    per_half = LRU_HEADS // 2
    eye = jnp.eye(per_half, dtype=wa.dtype)

    def bd(w):
        w4 = w.reshape(2, per_half, LRU_HEAD_DIM, LRU_HEAD_DIM)
        return jnp.einsum('hjil,jk->hjikl', w4, eye).reshape(2, LRU_WIDTH // 2, LRU_WIDTH // 2)
    return jnp.concatenate([bd(wa), bd(wx)], axis=2).astype(jnp.bfloat16)


def kernel(x, meta_tokens, norm_mix_pre, norm_mix_post, norm_mlp_pre, norm_mlp_post,
           w_in, conv_w, conv_b,
           lru_wa_f, lru_ba_f, lru_wx_f, lru_bx_f, lru_lambda_f,
           lru_wa_b, lru_ba_b, lru_wx_b, lru_bx_b, lru_lambda_b,
           gla_wg_f, gla_bg_f, gla_wg_b, gla_bg_b, gla_head_norm,
           w_out, w_mlp_up, w_mlp_down):
    bsz, seq, d = x.shape
    depth = w_in.shape[0]
    tp = seq + T0
    assert d == D_MODEL and tp % TIME_TILE == 0 and tp % GLA_TIME_TILE == 0
    assert (bsz * tp) % ROW_TILE == 0
    meta = jnp.broadcast_to(meta_tokens.astype(x.dtype)[None], (bsz, N_META, d))
    h = jnp.concatenate([jnp.zeros((bsz, PADF, d), x.dtype), meta, x], axis=1)
    h = h.reshape(bsz * tp, d)
    vec = lambda a: a.reshape(1, -1)
    bf = lambda a: a.astype(jnp.bfloat16)

    for l in range(depth):
        w_main = bf(w_in[l][:, :2560])
        w_zg = bf(jnp.tile(w_in[l][:, 2560:2592], (1, 3)))
        wg = jnp.zeros((2 * GLA_RANK, 2 * GLA_QK), jnp.float32)
        wg = wg.at[:GLA_RANK, :GLA_QK].set(gla_wg_f[l]).at[GLA_RANK:, GLA_QK:].set(gla_wg_b[l])
        wg_hi, wg_lo = _split_bf16(wg)
        wg3 = jnp.concatenate([wg_hi, wg_lo, wg_hi], axis=0)
        bg = jnp.concatenate([gla_bg_f[l], gla_bg_b[l]]).reshape(1, -1)

        outs = _in_proj(h, vec(norm_mix_pre[l]), w_main, w_zg, wg3, bg, conv_w[l], vec(conv_b[l]))
        xc3, gate3, qk3, v3, go3, gfb3 = (a.reshape(bsz, tp, a.shape[-1]) for a in outs)

        h_f = _lru_pass(xc3, _block_diag_gates(lru_wa_f[l], lru_wx_f[l]),
                        vec(lru_ba_f[l]), vec(lru_bx_f[l]), vec(lru_lambda_f[l]), reverse=False)
        y_lru = _lru_pass(xc3, _block_diag_gates(lru_wa_b[l], lru_wx_b[l]),
                          vec(lru_ba_b[l]), vec(lru_bx_b[l]), vec(lru_lambda_b[l]),
                          reverse=True, hf=h_f, gate=gate3)
        o_f = _gla_pass(qk3, v3, gfb3, reverse=False)
        y_gla = _gla_pass(qk3, v3, gfb3, reverse=True, o_f=o_f, go=go3,
                          head_norm=vec(gla_head_norm[l]))

        h = _out_mlp(h, y_lru.reshape(bsz * tp, -1), y_gla.reshape(bsz * tp, -1),
                     bf(w_out[l][:LRU_WIDTH]), bf(w_out[l][LRU_WIDTH:]),
                     vec(norm_mix_post[l]), vec(norm_mlp_pre[l]), vec(norm_mlp_post[l]),
                     bf(w_mlp_up[l]), bf(w_mlp_down[l]), tp)
    return h.reshape(bsz, tp, d)[:, T0:]
```

```python
import functools

import numpy as np
import jax
import jax.numpy as jnp
from jax import lax
from jax.experimental import pallas as pl
from jax.experimental.pallas import tpu as pltpu

D_MODEL = 1024
N_META = 16
CHUNK = 64
LRU_WIDTH = 512
LRU_HEADS = 8
LRU_HEAD_DIM = 64
LRU_C = 8.0
GLA_WIDTH = 512
GLA_HEADS = 4
GLA_DV = 128
GLA_DK = 64
GLA_QK = GLA_HEADS * GLA_DK
GLA_RANK = 16
GLA_GATE_NORM = 16.0
D_FF = 4096
EPS = 1e-6
LOG2_E = 1.4426950408889634

LEAD_CHUNKS = 4
T0 = LEAD_CHUNKS * CHUNK
PADF = T0 - N_META
ROW_TILE = 512
TIME_TILE = 1056
GLA_TIME_TILE = 768
GLA_UNROLL = 3
FF_TILE = 1024
SUBLANES = 8
LANES = 128
VMEM_LIMIT = 56 * 1024 * 1024


def _const_spec(shape):
    nd = len(shape)
    return pl.BlockSpec(shape, lambda *_: (0,) * nd, pipeline_mode=pl.Buffered(1))


def _rms_scale(x):
    return lax.rsqrt(jnp.mean(x * x, axis=-1, keepdims=True) + EPS)


def _sigmoid(x):
    return 1.0 / (1.0 + jnp.exp(-x))


def _softplus(x):
    return jnp.maximum(x, 0.0) + jnp.log1p(jnp.exp(-jnp.abs(x)))


def _split_bf16(x):
    hi = x.astype(jnp.bfloat16)
    lo = (x - hi.astype(jnp.float32)).astype(jnp.bfloat16)
    return hi, lo


def _in_proj_kernel(h_ref, hp_ref, hn_ref, g_ref, w_ref, wzg_ref, wg_ref, bg_ref, cw_ref, cb_ref,
                    xc_ref, gate_ref, qk_ref, v_ref, go_ref, gfb_ref, xs_ref, *, n_tiles):
    tm = h_ref.shape[0]
    normed = lambda x: (x * _rms_scale(x) * g_ref[...]).astype(jnp.bfloat16)
    halo = jnp.concatenate([hp_ref[...], hn_ref[...]], axis=0)
    xn = normed(h_ref[...])
    xn_ext = jnp.concatenate([xn, normed(halo)], axis=0)

    zg = jnp.dot(xn, wzg_ref[...], preferred_element_type=jnp.float32)
    zx = jnp.dot(xn_ext, w_ref[:, 0:LRU_WIDTH], preferred_element_type=jnp.float32)
    zhi = zg.astype(jnp.bfloat16)
    zlo = (zg - zhi.astype(jnp.float32)).astype(jnp.bfloat16)
    lane = lax.broadcasted_iota(jnp.int32, zg.shape, 1)
    lhs = jnp.where(lane < 4 * GLA_RANK, zhi, zlo)
    pre = jnp.dot(lhs, wg_ref[...], preferred_element_type=jnp.float32) + bg_ref[...]
    z = jnp.dot(xn, w_ref[:, LRU_WIDTH:], preferred_element_type=jnp.float32)

    xs_ref[0:SUBLANES, :] = zx[tm:tm + SUBLANES, :]
    xs_ref[SUBLANES:SUBLANES + tm, :] = zx[0:tm, :]
    xs_ref[SUBLANES + tm:2 * SUBLANES + tm, :] = jnp.where(
        pl.program_id(0) == n_tiles - 1, 0.0, zx[tm + SUBLANES:tm + 2 * SUBLANES, :])
    cw = cw_ref[...]
    xc_ref[...] = (cw[0:1, :] * xs_ref[SUBLANES - 2:SUBLANES - 2 + tm, :]
                   + cw[1:2, :] * xs_ref[SUBLANES - 1:SUBLANES - 1 + tm, :]
                   + cw[2:3, :] * xs_ref[SUBLANES:SUBLANES + tm, :]
                   + cw[3:4, :] * xs_ref[SUBLANES + 1:SUBLANES + 1 + tm, :]
                   + cb_ref[...])
    logsig = jnp.minimum(pre, 0.0) - jnp.log1p(jnp.exp(-jnp.abs(pre)))
    gfb_ref[...] = logsig * (LOG2_E / GLA_GATE_NORM)

    gate_ref[...] = z[:, 0:512].astype(jnp.bfloat16)
    q = z[:, 512:768] * (GLA_DK ** -0.5)
    qk_ref[...] = jnp.concatenate([q, z[:, 768:1024]], axis=1).astype(jnp.bfloat16)
    v_ref[...] = z[:, 1024:1536].astype(jnp.bfloat16)
    go_ref[...] = z[:, 1536:2048].astype(jnp.bfloat16)


def _in_proj(h, g, w_main, w_zg, wg3, bg, conv_w, conv_b):
    rows = h.shape[0]
    tm = ROW_TILE
    nt = rows // tm
    g8 = tm // SUBLANES
    row = lambda w: pl.BlockSpec((tm, w), lambda i: (i, 0))
    before = pl.BlockSpec((SUBLANES, D_MODEL), lambda i: (jnp.maximum(i * g8 - 1, 0), 0))
    after = pl.BlockSpec((SUBLANES, D_MODEL),
                         lambda i: (jnp.minimum((i + 1) * g8, rows // SUBLANES - 1), 0))
    consts = (g, w_main, w_zg, wg3, bg, conv_w, conv_b)
    dtypes = (jnp.float32, jnp.bfloat16, jnp.bfloat16, jnp.bfloat16, jnp.bfloat16, jnp.float32)
    return pl.pallas_call(
        functools.partial(_in_proj_kernel, n_tiles=nt),
        grid=(nt,),
        in_specs=[row(D_MODEL), before, after] + [_const_spec(a.shape) for a in consts],
        out_specs=[row(512)] * len(dtypes),
        out_shape=[jax.ShapeDtypeStruct((rows, 512), dt) for dt in dtypes],
        scratch_shapes=[pltpu.VMEM((tm + 2 * SUBLANES, LRU_WIDTH), jnp.float32)],
        compiler_params=pltpu.CompilerParams(
            dimension_semantics=("arbitrary",), vmem_limit_bytes=VMEM_LIMIT),
        name="in_proj",
    )(h, h, h, *consts)


def _lru_kernel(*refs, reverse, n_tiles):
    if reverse:
        (xc_ref, wg_ref, ba_ref, bx_ref, lam_ref, hf_ref, gate_ref,
         out_ref, a_scr, u_scr, carry_ref) = refs
    else:
        xc_ref, wg_ref, ba_ref, bx_ref, lam_ref, out_ref, a_scr, u_scr, carry_ref = refs
    tt = xc_ref.shape[0]
    step = pl.program_id(1)
    tile = (n_tiles - 1 - step) if reverse else step

    @pl.when(step == 0)
    def _():
        carry_ref[...] = jnp.zeros_like(carry_ref)

    xc = xc_ref[...]
    row = lax.broadcasted_iota(jnp.int32, xc.shape, 0)
    xcb = xc.astype(jnp.bfloat16)
    half = LRU_WIDTH // 2
    p0 = jnp.dot(xcb[:, :half], wg_ref[0], preferred_element_type=jnp.float32)
    p1 = jnp.dot(xcb[:, half:], wg_ref[1], preferred_element_type=jnp.float32)
    r_pre = jnp.concatenate([p0[:, :half], p1[:, :half]], axis=1)
    i_pre = jnp.concatenate([p0[:, half:], p1[:, half:]], axis=1)
    r = _sigmoid(r_pre + ba_ref[...])
    gi = _sigmoid(i_pre + bx_ref[...])
    log_a = (-LRU_C) * r * _softplus(-lam_ref[...])
    a = jnp.exp(log_a)
    u = jnp.sqrt(-jnp.tanh(log_a) * (1.0 + a * a)) * (gi * xc)
    if not reverse:
        u = jnp.where(row + tile * tt >= PADF, u, 0.0)

    run = tt // SUBLANES
    n_slabs = LRU_WIDTH // LANES
    for k in range(n_slabs):
        a_scr[k] = a[:, k * LANES:(k + 1) * LANES]
        u_scr[k] = u[:, k * LANES:(k + 1) * LANES]
    steps = lambda j: pl.ds(j, SUBLANES, stride=run)

    def local_scan(jj, carry):
        j = (run - 1 - jj) if reverse else jj
        state, prod = carry
        new_state, new_prod = [], []
        for k in range(n_slabs):
            aj = a_scr[k, steps(j), :]
            sk = aj * state[k] + u_scr[k, steps(j), :]
            pk = prod[k] * aj
            u_scr[k, steps(j), :] = sk
            a_scr[k, steps(j), :] = pk
            new_state.append(sk)
            new_prod.append(pk)
        return jnp.stack(new_state), jnp.stack(new_prod)

    zeros = jnp.zeros((n_slabs, SUBLANES, LANES), jnp.float32)
    end_state, end_prod = lax.fori_loop(0, run, local_scan, (zeros, zeros + 1.0), unroll=4)

    c = carry_ref[...]
    order = range(SUBLANES - 1, -1, -1) if reverse else range(SUBLANES)
    entering = [None] * SUBLANES
    for r in order:
        entering[r] = c
        c = end_prod[:, r:r + 1, :] * c + end_state[:, r:r + 1, :]
    carry_ref[...] = c
    entering = jnp.concatenate(entering, axis=1)

    def add_entering(j, carry):
        for k in range(n_slabs):
            u_scr[k, steps(j), :] = u_scr[k, steps(j), :] + a_scr[k, steps(j), :] * entering[k]
        return carry

    lax.fori_loop(0, run, add_entering, 0, unroll=4)

    hs = jnp.concatenate([u_scr[k] for k in range(n_slabs)], axis=1)
    if reverse:
        gt = gate_ref[...].astype(jnp.float32)
        gelu = 0.5 * gt * (1.0 + jnp.tanh(0.7978845608028654 * (gt + 0.044715 * gt * gt * gt)))
        out_ref[...] = ((hf_ref[...] + hs) * gelu).astype(out_ref.dtype)
    else:
        out_ref[...] = hs


def _lru_pass(xc, wgate, ba, bx, lam, *, reverse, hf=None, gate=None):
    bsz, tp, _ = xc.shape
    tt = TIME_TILE
    nt = tp // tt
    tsel = (lambda t: nt - 1 - t) if reverse else (lambda t: t)
    cur = pl.BlockSpec((None, tt, LRU_WIDTH), lambda b, t: (b, tsel(t), 0))
    in_specs = [cur] + [_const_spec(a.shape) for a in (wgate, ba, bx, lam)]
    args = [xc, wgate, ba, bx, lam]
    if reverse:
        in_specs += [cur, cur]
        args += [hf, gate]
        out_dtype = jnp.bfloat16
    else:
        out_dtype = jnp.float32
    return pl.pallas_call(
        functools.partial(_lru_kernel, reverse=reverse, n_tiles=nt),
        grid=(bsz, nt),
        in_specs=in_specs,
        out_specs=cur,
        out_shape=jax.ShapeDtypeStruct((bsz, tp, LRU_WIDTH), out_dtype),
        scratch_shapes=[pltpu.VMEM((LRU_WIDTH // LANES, tt, LANES), jnp.float32),
                        pltpu.VMEM((LRU_WIDTH // LANES, tt, LANES), jnp.float32),
                        pltpu.VMEM((LRU_WIDTH // LANES, 1, LANES), jnp.float32)],
        compiler_params=pltpu.CompilerParams(
            dimension_semantics=("arbitrary", "arbitrary"), vmem_limit_bytes=VMEM_LIMIT),
        name="lru_bwd" if reverse else "lru_fwd",
    )(*args)


def _gla_constants(reverse):
    c, w = CHUNK, SUBLANES
    t = np.arange(c)[:, None]
    s = np.arange(c)[None, :]
    gt, gs = t // w, s // w
    if reverse:
        code = np.where((gt == gs) & (s > t), s - t, np.where(gs > gt, w + gs, -1))
        blocks = [s >= t] + [(s >= t) & (s < t + d) for d in range(1, w)]
    else:
        code = np.where((gt == gs) & (s <= t), t - s, np.where(gs < gt, w + gs, -1))
        blocks = [s <= t] + [(s > t - d) & (s <= t) for d in range(1, w)]
    lmat = np.concatenate([blk.astype(np.float32) for blk in blocks], axis=0)
    lmat = np.concatenate([lmat, lmat], axis=1)
    code = np.tile(code.astype(np.int32), (1, GLA_HEADS))
    hmask = np.kron(np.eye(GLA_HEADS, dtype=np.float32), np.ones((c, GLA_DK), np.float32))
    return jnp.asarray(lmat, jnp.bfloat16), jnp.asarray(code), jnp.asarray(hmask, jnp.bfloat16)


def _gla_decays(g, lmat):
    ghi, glo = _split_bf16(g)
    return jnp.dot(lmat, jnp.concatenate([ghi, glo], axis=0), preferred_element_type=jnp.float32)


def _gla_scores(q, k, logdec, hmask, reverse):
    n_groups = CHUNK // SUBLANES

    def head_stack(x):
        return jnp.concatenate([x.astype(jnp.bfloat16)] * GLA_HEADS, axis=0) * hmask

    nt_dims = (((1,), (1,)), ((), ()))
    b = logdec[0:CHUNK, :]

    lhs = [q] if not reverse else []
    for d in range(1, SUBLANES):
        lhs.append(q * jnp.exp2(logdec[d * CHUNK:(d + 1) * CHUNK, :]))
    p_within = lax.dot_general(jnp.concatenate(lhs, axis=0).astype(jnp.bfloat16), head_stack(k),
                               nt_dims, preferred_element_type=jnp.float32)

    anchor_row = (lambda gi: gi * SUBLANES) if reverse else (lambda gi: gi * SUBLANES + SUBLANES - 1)
    anch = [b[anchor_row(gi):anchor_row(gi) + 1, :] for gi in range(n_groups)]
    b_anchor = jnp.concatenate([jnp.broadcast_to(a, (SUBLANES, GLA_QK)) for a in anch], axis=0)
    groups = list(range(1, n_groups)) if reverse else list(range(n_groups - 1))
    row_sel = (lambda gi: slice(0, gi * SUBLANES)) if reverse else \
              (lambda gi: slice((gi + 1) * SUBLANES, CHUNK))
    lhs = [q[row_sel(gi), :] * jnp.exp2(b[row_sel(gi), :] - anch[gi]) for gi in groups]
    k_anch = k * jnp.exp2(b_anchor - b)
    p_cross = lax.dot_general(jnp.concatenate(lhs, axis=0).astype(jnp.bfloat16), head_stack(k_anch),
                              nt_dims, preferred_element_type=jnp.float32)

    last = 0 if reverse else CHUNK - 1
    b_last = b[last:last + 1, :]
    qe = (q * jnp.exp2(b)).astype(jnp.bfloat16)
    ke = (k * jnp.exp2(b_last - b)).astype(jnp.bfloat16)
    return p_within, p_cross, qe, ke, jnp.exp2(b_last)


def _gla_apply(p_within, p_cross, qe, ke, e_last, vb, s_ref, code, reverse):
    n_groups = CHUNK // SUBLANES
    offsets = list(range(1 if reverse else 0, SUBLANES))
    groups = list(range(1, n_groups)) if reverse else list(range(n_groups - 1))
    scores = jnp.zeros((CHUNK, GLA_HEADS * CHUNK), jnp.float32)
    for i, d in enumerate(offsets):
        scores = jnp.where(code == d, p_within[i * CHUNK:(i + 1) * CHUNK, :], scores)
    off = 0
    for gi in groups:
        n = gi * SUBLANES if reverse else CHUNK - (gi + 1) * SUBLANES
        blk = p_cross[off:off + n, :]
        off += n
        pad = jnp.zeros((CHUNK - n, GLA_HEADS * CHUNK), jnp.float32)
        full = jnp.concatenate([blk, pad] if reverse else [pad, blk], axis=0)
        scores = jnp.where(code == SUBLANES + gi, full, scores)
    sb = scores.astype(jnp.bfloat16)

    eye = (lax.broadcasted_iota(jnp.int32, (GLA_DK, GLA_DK), 0)
           == lax.broadcasted_iota(jnp.int32, (GLA_DK, GLA_DK), 1))
    zero_v = jnp.zeros((CHUNK, GLA_DV), jnp.bfloat16)
    zero_s = jnp.zeros((GLA_DK, GLA_DV), jnp.bfloat16)
    outs = []
    for pair in range(GLA_HEADS // 2):
        heads = (2 * pair, 2 * pair + 1)
        vh = [vb[:, hd * GLA_DV:(hd + 1) * GLA_DV] for hd in heads]
        st = [s_ref[hd] for hd in heads]
        sh = [s.astype(jnp.bfloat16) for s in st]
        lanes = slice(pair * 2 * CHUNK, (pair + 1) * 2 * CHUNK)
        lhs = jnp.concatenate([sb[:, lanes], qe[:, lanes]], axis=1)
        rhs = jnp.concatenate([jnp.concatenate([vh[0], zero_v], axis=1),
                               jnp.concatenate([zero_v, vh[1]], axis=1),
                               jnp.concatenate([sh[0], zero_s], axis=1),
                               jnp.concatenate([zero_s, sh[1]], axis=1)], axis=0)
        outs.append(jnp.dot(lhs, rhs, preferred_element_type=jnp.float32))
        for i, hd in enumerate(heads):
            ks = slice(hd * GLA_DK, (hd + 1) * GLA_DK)
            kv = lax.dot_general(ke[:, ks], vh[i], (((0,), (0,)), ((), ())),
                                 preferred_element_type=jnp.float32)
            e_col = jnp.sum(jnp.where(eye, e_last[:, ks], 0.0), axis=1, keepdims=True)
            s_ref[hd] = st[i] * e_col + kv
    return jnp.concatenate(outs, axis=1)


def _gla_kernel(*refs, reverse):
    if reverse:
        (qk_ref, v_ref, g_ref, lmat_ref, code_ref, hmask_ref, of_ref, go_ref, hn_ref,
         out_ref, s_ref) = refs
    else:
        qk_ref, v_ref, g_ref, lmat_ref, code_ref, hmask_ref, out_ref, s_ref = refs
    bsz, tt, _ = qk_ref.shape
    n_chunks = tt // CHUNK

    @pl.when(pl.program_id(0) == 0)
    def _():
        s_ref[...] = jnp.zeros_like(s_ref)

    lmat = lmat_ref[...]
    code = code_ref[...]
    hmask = hmask_ref[...]

    def chunk_body(j, carry):
        work = []
        for u in range(GLA_UNROLL):
            jj = j * GLA_UNROLL + u
            cidx = (n_chunks - 1 - jj) if reverse else jj
            rows = pl.ds(pl.multiple_of(cidx * CHUNK, CHUNK), CHUNK)
            work += [(bi, rows) for bi in range(bsz)]
        decs = [_gla_decays(g_ref[bi, rows, :], lmat) for bi, rows in work]
        terms = [_gla_scores(qk_ref[bi, rows, 0:GLA_QK], qk_ref[bi, rows, GLA_QK:2 * GLA_QK],
                             dec, hmask, reverse) for (bi, rows), dec in zip(work, decs)]
        for (bi, rows), term in zip(work, terms):
            o = _gla_apply(*term, v_ref[bi, rows, :].astype(jnp.bfloat16), s_ref.at[bi], code,
                           reverse)
            if reverse:
                o = o + of_ref[bi, rows, :]
                parts = []
                for hd in range(GLA_HEADS):
                    oh = o[:, hd * GLA_DV:(hd + 1) * GLA_DV]
                    parts.append(oh * _rms_scale(oh))
                gt = go_ref[bi, rows, :].astype(jnp.float32)
                y = jnp.concatenate(parts, axis=1) * hn_ref[...] * (gt * _sigmoid(gt))
                out_ref[bi, rows, :] = y.astype(out_ref.dtype)
            else:
                out_ref[bi, rows, :] = o
        return carry

    lax.fori_loop(0, n_chunks // GLA_UNROLL, chunk_body, 0)


def _gla_pass(qk, v, gfb, *, reverse, o_f=None, go=None, head_norm=None):
    bsz, tp, _ = qk.shape
    tt = GLA_TIME_TILE
    nt = tp // tt
    tsel = (lambda t: nt - 1 - t) if reverse else (lambda t: t)
    blk = lambda w, c=0: pl.BlockSpec((bsz, tt, w), lambda t: (0, tsel(t), c))
    consts = _gla_constants(reverse)
    in_specs = [blk(512), blk(512), blk(GLA_QK, 1 if reverse else 0)]
    in_specs += [_const_spec(a.shape) for a in consts]
    args = [qk, v, gfb, *consts]
    if reverse:
        in_specs += [blk(512), blk(512), _const_spec(head_norm.shape)]
        args += [o_f, go, head_norm]
        out_dtype = jnp.bfloat16
    else:
        out_dtype = jnp.float32
    return pl.pallas_call(
        functools.partial(_gla_kernel, reverse=reverse),
        grid=(nt,),
        in_specs=in_specs,
        out_specs=blk(512),
        out_shape=jax.ShapeDtypeStruct((bsz, tp, GLA_WIDTH), out_dtype),
        scratch_shapes=[pltpu.VMEM((bsz, GLA_HEADS, GLA_DK, GLA_DV), jnp.float32)],
        compiler_params=pltpu.CompilerParams(
            dimension_semantics=("arbitrary",), vmem_limit_bytes=VMEM_LIMIT),
        name="gla_bwd" if reverse else "gla_fwd",
    )(*args)


def _out_mlp_kernel(h_ref, yl_ref, yg_ref, wol_ref, wog_ref, gpost_ref, gpre_ref, gpost2_ref,
                    wup_ref, wdn_ref, out_ref, acc_ref, *, rows_per_batch, n_batch, per_batch_grid):
    tm = h_ref.shape[0]
    mix = (jnp.dot(yl_ref[...], wol_ref[...], preferred_element_type=jnp.float32)
           + jnp.dot(yg_ref[...], wog_ref[...], preferred_element_type=jnp.float32))
    tile = pl.program_id(0)
    if per_batch_grid:
        tile = tile * pl.num_programs(1) + pl.program_id(1)
    row = lax.broadcasted_iota(jnp.int32, (tm, 1), 0) + tile * tm
    real = row >= PADF
    for b in range(1, n_batch):
        real = real & ((row < b * rows_per_batch) | (row >= b * rows_per_batch + PADF))
    h1 = h_ref[...] + jnp.where(real, mix * _rms_scale(mix) * gpost_ref[...], 0.0)
    xn = (h1 * _rms_scale(h1) * gpre_ref[...]).astype(jnp.bfloat16)
    for c in range(D_FF // FF_TILE):
        cs = slice(c * FF_TILE, (c + 1) * FF_TILE)
        up = jnp.dot(xn, wup_ref[:, cs], preferred_element_type=jnp.float32)
        act = jnp.square(jnp.maximum(up, 0.0)).astype(jnp.bfloat16)
        part = jnp.dot(act, wdn_ref[cs, :], preferred_element_type=jnp.float32)
        if c == 0:
            acc_ref[...] = part
        else:
            acc_ref[...] += part
    ff = acc_ref[...]
    out_ref[...] = h1 + ff * _rms_scale(ff) * gpost2_ref[...]


def _out_mlp(h, y_lru, y_gla, wo_l, wo_g, g_post, g_pre, g_post2, w_up, w_dn, rows_per_batch,
             drop_lead):
    rows = h.shape[0]
    n_batch = rows // rows_per_batch
    consts = (wo_l, wo_g, g_post, g_pre, g_post2, w_up, w_dn)
    if drop_lead:
        tm = T0
        per_batch = rows_per_batch // tm
        grid = (n_batch, per_batch)
        row = lambda w: pl.BlockSpec((tm, w), lambda b, j: (b * per_batch + j, 0))
        out_spec = pl.BlockSpec(
            (tm, D_MODEL), lambda b, j: (b * (per_batch - 1) + jnp.maximum(j - 1, 0), 0))
        out_rows = rows - n_batch * tm
    else:
        tm = ROW_TILE
        grid = (rows // tm,)
        row = lambda w: pl.BlockSpec((tm, w), lambda i: (i, 0))
        out_spec = row(D_MODEL)
        out_rows = rows
    return pl.pallas_call(
        functools.partial(_out_mlp_kernel, rows_per_batch=rows_per_batch, n_batch=n_batch,
                          per_batch_grid=drop_lead),
        grid=grid,
        in_specs=[row(D_MODEL), row(512), row(512)] + [_const_spec(a.shape) for a in consts],
        out_specs=out_spec,
        out_shape=jax.ShapeDtypeStruct((out_rows, D_MODEL), jnp.float32),
        scratch_shapes=[pltpu.VMEM((tm, D_MODEL), jnp.float32)],
        compiler_params=pltpu.CompilerParams(
            dimension_semantics=("arbitrary",) * len(grid), vmem_limit_bytes=VMEM_LIMIT),
        name="out_mlp",
    )(h, y_lru, y_gla, *consts)


def _block_diag_gates(wa, wx):
    per_half = LRU_HEADS // 2
    eye = jnp.eye(per_half, dtype=wa.dtype)

    def bd(w):
        w4 = w.reshape(2, per_half, LRU_HEAD_DIM, LRU_HEAD_DIM)
        return jnp.einsum('hjil,jk->hjikl', w4, eye).reshape(2, LRU_WIDTH // 2, LRU_WIDTH // 2)
    return jnp.concatenate([bd(wa), bd(wx)], axis=2).astype(jnp.bfloat16)


def kernel(x, meta_tokens, norm_mix_pre, norm_mix_post, norm_mlp_pre, norm_mlp_post,
           w_in, conv_w, conv_b,
           lru_wa_f, lru_ba_f, lru_wx_f, lru_bx_f, lru_lambda_f,
           lru_wa_b, lru_ba_b, lru_wx_b, lru_bx_b, lru_lambda_b,
           gla_wg_f, gla_bg_f, gla_wg_b, gla_bg_b, gla_head_norm,
           w_out, w_mlp_up, w_mlp_down):
    bsz, seq, d = x.shape
    depth = w_in.shape[0]
    tp = seq + T0
    assert d == D_MODEL and tp % TIME_TILE == 0 and tp % GLA_TIME_TILE == 0
    assert (bsz * tp) % ROW_TILE == 0
    meta = jnp.broadcast_to(meta_tokens.astype(x.dtype)[None], (bsz, N_META, d))
    h = jnp.concatenate([jnp.zeros((bsz, PADF, d), x.dtype), meta, x], axis=1)
    h = h.reshape(bsz * tp, d)
    vec = lambda a: a.reshape(1, -1)
    bf = lambda a: a.astype(jnp.bfloat16)

    for l in range(depth):
        w_main = bf(w_in[l][:, :2560])
        w_zg = bf(jnp.tile(w_in[l][:, 2560:2592], (1, 3)))
        wg = jnp.zeros((2 * GLA_RANK, 2 * GLA_QK), jnp.float32)
        wg = wg.at[:GLA_RANK, :GLA_QK].set(gla_wg_f[l]).at[GLA_RANK:, GLA_QK:].set(gla_wg_b[l])
        wg_hi, wg_lo = _split_bf16(wg)
        wg3 = jnp.concatenate([wg_hi, wg_lo, wg_hi], axis=0)
        bg = jnp.concatenate([gla_bg_f[l], gla_bg_b[l]]).reshape(1, -1)

        outs = _in_proj(h, vec(norm_mix_pre[l]), w_main, w_zg, wg3, bg, conv_w[l], vec(conv_b[l]))
        xc3, gate3, qk3, v3, go3, gfb3 = (a.reshape(bsz, tp, a.shape[-1]) for a in outs)

        h_f = _lru_pass(xc3, _block_diag_gates(lru_wa_f[l], lru_wx_f[l]),
                        vec(lru_ba_f[l]), vec(lru_bx_f[l]), vec(lru_lambda_f[l]), reverse=False)
        y_lru = _lru_pass(xc3, _block_diag_gates(lru_wa_b[l], lru_wx_b[l]),
                          vec(lru_ba_b[l]), vec(lru_bx_b[l]), vec(lru_lambda_b[l]),
                          reverse=True, hf=h_f, gate=gate3)
        o_f = _gla_pass(qk3, v3, gfb3, reverse=False)
        y_gla = _gla_pass(qk3, v3, gfb3, reverse=True, o_f=o_f, go=go3,
                          head_norm=vec(gla_head_norm[l]))

        h = _out_mlp(h, y_lru.reshape(bsz * tp, -1), y_gla.reshape(bsz * tp, -1),
                     bf(w_out[l][:LRU_WIDTH]), bf(w_out[l][LRU_WIDTH:]),
                     vec(norm_mix_post[l]), vec(norm_mlp_pre[l]), vec(norm_mlp_post[l]),
                     bf(w_mlp_up[l]), bf(w_mlp_down[l]), tp, drop_lead=(l == depth - 1))
    return h.reshape(bsz, seq, d)
```

```python
import functools

import numpy as np
import jax
import jax.numpy as jnp
from jax import lax
from jax.experimental import pallas as pl
from jax.experimental.pallas import tpu as pltpu

D_MODEL = 1024
N_META = 16
CHUNK = 64
LRU_WIDTH = 512
LRU_HEADS = 8
LRU_HEAD_DIM = 64
LRU_C = 8.0
GLA_WIDTH = 512
GLA_HEADS = 4
GLA_DV = 128
GLA_DK = 64
GLA_QK = GLA_HEADS * GLA_DK
GLA_RANK = 16
GLA_GATE_NORM = 16.0
D_FF = 4096
EPS = 1e-6
LOG2_E = 1.4426950408889634

LEAD_CHUNKS = 4
T0 = LEAD_CHUNKS * CHUNK
PADF = T0 - N_META
ROW_TILE = 768
TIME_TILE = 1056
GLA_TIME_TILE = 768
GLA_UNROLL = 3
FF_TILE = 1024
SUBLANES = 8
LANES = 128
VMEM_LIMIT = 56 * 1024 * 1024


class _Layer:
    def __init__(self, stacked, layer):
        self.stacked, self.layer, self.shape = stacked, layer, tuple(stacked.shape[1:])


def _operand(a):
    return a.stacked if isinstance(a, _Layer) else a


def _const_spec(a):
    nd = len(a.shape)
    if isinstance(a, _Layer):
        layer = a.layer
        return pl.BlockSpec((None,) + a.shape, lambda *_: (layer,) + (0,) * nd,
                            pipeline_mode=pl.Buffered(1))
    return pl.BlockSpec(a.shape, lambda *_: (0,) * nd, pipeline_mode=pl.Buffered(1))


def _rms_scale(x):
    return lax.rsqrt(jnp.mean(x * x, axis=-1, keepdims=True) + EPS)


def _sigmoid(x):
    return 1.0 / (1.0 + jnp.exp(-x))


def _softplus(x):
    return jnp.maximum(x, 0.0) + jnp.log1p(jnp.exp(-jnp.abs(x)))


def _split_bf16(x):
    hi = x.astype(jnp.bfloat16)
    lo = (x - hi.astype(jnp.float32)).astype(jnp.bfloat16)
    return hi, lo


def _in_proj_kernel(h_ref, hp_ref, hn_ref, g_ref, w_ref, wzg_ref, wg_ref, bg_ref, cw_ref, cb_ref,
                    xc_ref, gate_ref, qk_ref, v_ref, go_ref, gfb_ref, xs_ref, *, n_tiles):
    tm = h_ref.shape[0]
    normed = lambda x: (x * _rms_scale(x) * g_ref[...]).astype(jnp.bfloat16)
    halo = jnp.concatenate([hp_ref[...], hn_ref[...]], axis=0)
    xn = normed(h_ref[...])
    xn_ext = jnp.concatenate([xn, normed(halo)], axis=0)

    zg = jnp.dot(xn, wzg_ref[...], preferred_element_type=jnp.float32)
    zx = jnp.dot(xn_ext, w_ref[:, 0:LRU_WIDTH], preferred_element_type=jnp.float32)
    zhi = zg.astype(jnp.bfloat16)
    zlo = (zg - zhi.astype(jnp.float32)).astype(jnp.bfloat16)
    lane = lax.broadcasted_iota(jnp.int32, zg.shape, 1)
    lhs = jnp.where(lane < 4 * GLA_RANK, zhi, zlo)
    pre = jnp.dot(lhs, wg_ref[...], preferred_element_type=jnp.float32) + bg_ref[...]
    z = jnp.dot(xn, w_ref[:, LRU_WIDTH:], preferred_element_type=jnp.float32)

    xs_ref[0:SUBLANES, :] = zx[tm:tm + SUBLANES, :]
    xs_ref[SUBLANES:SUBLANES + tm, :] = zx[0:tm, :]
    xs_ref[SUBLANES + tm:2 * SUBLANES + tm, :] = jnp.where(
        pl.program_id(0) == n_tiles - 1, 0.0, zx[tm + SUBLANES:tm + 2 * SUBLANES, :])
    cw = cw_ref[...]
    xc_ref[...] = (cw[0:1, :] * xs_ref[SUBLANES - 2:SUBLANES - 2 + tm, :]
                   + cw[1:2, :] * xs_ref[SUBLANES - 1:SUBLANES - 1 + tm, :]
                   + cw[2:3, :] * xs_ref[SUBLANES:SUBLANES + tm, :]
                   + cw[3:4, :] * xs_ref[SUBLANES + 1:SUBLANES + 1 + tm, :]
                   + cb_ref[...])
    logsig = jnp.minimum(pre, 0.0) - jnp.log1p(jnp.exp(-jnp.abs(pre)))
    gfb_ref[...] = logsig * (LOG2_E / GLA_GATE_NORM)

    gt = z[:, 0:512]
    gelu = 0.5 * gt * (1.0 + jnp.tanh(0.7978845608028654 * (gt + 0.044715 * gt * gt * gt)))
    gate_ref[...] = gelu.astype(jnp.bfloat16)
    q = z[:, 512:768] * (GLA_DK ** -0.5)
    qk_ref[...] = jnp.concatenate([q, z[:, 768:1024]], axis=1).astype(jnp.bfloat16)
    v_ref[...] = z[:, 1024:1536].astype(jnp.bfloat16)
    go = z[:, 1536:2048]
    go_ref[...] = (go * _sigmoid(go)).astype(jnp.bfloat16)


def _in_proj(h, g, w_main, w_zg, wg3, bg, conv_w, conv_b):
    rows = h.shape[0]
    tm = ROW_TILE
    nt = rows // tm
    g8 = tm // SUBLANES
    row = lambda w: pl.BlockSpec((tm, w), lambda i: (i, 0))
    before = pl.BlockSpec((SUBLANES, D_MODEL), lambda i: (jnp.maximum(i * g8 - 1, 0), 0))
    after = pl.BlockSpec((SUBLANES, D_MODEL),
                         lambda i: (jnp.minimum((i + 1) * g8, rows // SUBLANES - 1), 0))
    consts = (g, w_main, w_zg, wg3, bg, conv_w, conv_b)
    dtypes = (jnp.float32, jnp.bfloat16, jnp.bfloat16, jnp.bfloat16, jnp.bfloat16, jnp.float32)
    return pl.pallas_call(
        functools.partial(_in_proj_kernel, n_tiles=nt),
        grid=(nt,),
        in_specs=[row(D_MODEL), before, after] + [_const_spec(a) for a in consts],
        out_specs=[row(512)] * len(dtypes),
        out_shape=[jax.ShapeDtypeStruct((rows, 512), dt) for dt in dtypes],
        scratch_shapes=[pltpu.VMEM((tm + 2 * SUBLANES, LRU_WIDTH), jnp.float32)],
        compiler_params=pltpu.CompilerParams(
            dimension_semantics=("arbitrary",), vmem_limit_bytes=VMEM_LIMIT),
        name="in_proj",
    )(h, h, h, *map(_operand, consts))


def _lru_kernel(*refs, reverse, n_tiles):
    if reverse:
        (xc_ref, wg_ref, ba_ref, bx_ref, lam_ref, hf_ref, gate_ref,
         out_ref, a_scr, u_scr, carry_ref) = refs
    else:
        xc_ref, wg_ref, ba_ref, bx_ref, lam_ref, out_ref, a_scr, u_scr, carry_ref = refs
    tt = xc_ref.shape[0]
    step = pl.program_id(1)
    tile = (n_tiles - 1 - step) if reverse else step

    @pl.when(step == 0)
    def _():
        carry_ref[...] = jnp.zeros_like(carry_ref)

    xc = xc_ref[...]
    row = lax.broadcasted_iota(jnp.int32, xc.shape, 0)
    xcb = xc.astype(jnp.bfloat16)
    half = LRU_WIDTH // 2
    p0 = jnp.dot(xcb[:, :half], wg_ref[0], preferred_element_type=jnp.float32)
    p1 = jnp.dot(xcb[:, half:], wg_ref[1], preferred_element_type=jnp.float32)
    r_pre = jnp.concatenate([p0[:, :half], p1[:, :half]], axis=1)
    i_pre = jnp.concatenate([p0[:, half:], p1[:, half:]], axis=1)
    r = _sigmoid(r_pre + ba_ref[...])
    gi = _sigmoid(i_pre + bx_ref[...])
    log_a = (-LRU_C) * r * _softplus(-lam_ref[...])
    a = jnp.exp(log_a)
    u = jnp.sqrt(-jnp.tanh(log_a) * (1.0 + a * a)) * (gi * xc)
    if not reverse:
        u = jnp.where(row + tile * tt >= PADF, u, 0.0)

    run = tt // SUBLANES
    n_slabs = LRU_WIDTH // LANES
    for k in range(n_slabs):
        a_scr[k] = a[:, k * LANES:(k + 1) * LANES]
        u_scr[k] = u[:, k * LANES:(k + 1) * LANES]
    steps = lambda j: pl.ds(j, SUBLANES, stride=run)

    def local_scan(jj, carry):
        j = (run - 1 - jj) if reverse else jj
        state, prod = carry
        new_state, new_prod = [], []
        for k in range(n_slabs):
            aj = a_scr[k, steps(j), :]
            sk = aj * state[k] + u_scr[k, steps(j), :]
            pk = prod[k] * aj
            u_scr[k, steps(j), :] = sk
            a_scr[k, steps(j), :] = pk
            new_state.append(sk)
            new_prod.append(pk)
        return jnp.stack(new_state), jnp.stack(new_prod)

    zeros = jnp.zeros((n_slabs, SUBLANES, LANES), jnp.float32)
    end_state, end_prod = lax.fori_loop(0, run, local_scan, (zeros, zeros + 1.0), unroll=4)

    c = carry_ref[...]
    order = range(SUBLANES - 1, -1, -1) if reverse else range(SUBLANES)
    entering = [None] * SUBLANES
    for r in order:
        entering[r] = c
        c = end_prod[:, r:r + 1, :] * c + end_state[:, r:r + 1, :]
    carry_ref[...] = c
    entering = jnp.concatenate(entering, axis=1)

    def add_entering(j, carry):
        for k in range(n_slabs):
            u_scr[k, steps(j), :] = u_scr[k, steps(j), :] + a_scr[k, steps(j), :] * entering[k]
        return carry

    lax.fori_loop(0, run, add_entering, 0, unroll=4)

    hs = jnp.concatenate([u_scr[k] for k in range(n_slabs)], axis=1)
    if reverse:
        out_ref[...] = ((hf_ref[...] + hs) * gate_ref[...].astype(jnp.float32)).astype(out_ref.dtype)
    else:
        out_ref[...] = hs


def _lru_pass(xc, wgate, ba, bx, lam, *, reverse, hf=None, gate=None):
    bsz, tp, _ = xc.shape
    tt = TIME_TILE
    nt = tp // tt
    tsel = (lambda t: nt - 1 - t) if reverse else (lambda t: t)
    cur = pl.BlockSpec((None, tt, LRU_WIDTH), lambda b, t: (b, tsel(t), 0))
    in_specs = [cur] + [_const_spec(a) for a in (wgate, ba, bx, lam)]
    args = [xc, wgate, ba, bx, lam]
    if reverse:
        in_specs += [cur, cur]
        args += [hf, gate]
        out_dtype = jnp.bfloat16
    else:
        out_dtype = jnp.float32
    return pl.pallas_call(
        functools.partial(_lru_kernel, reverse=reverse, n_tiles=nt),
        grid=(bsz, nt),
        in_specs=in_specs,
        out_specs=cur,
        out_shape=jax.ShapeDtypeStruct((bsz, tp, LRU_WIDTH), out_dtype),
        scratch_shapes=[pltpu.VMEM((LRU_WIDTH // LANES, tt, LANES), jnp.float32),
                        pltpu.VMEM((LRU_WIDTH // LANES, tt, LANES), jnp.float32),
                        pltpu.VMEM((LRU_WIDTH // LANES, 1, LANES), jnp.float32)],
        compiler_params=pltpu.CompilerParams(
            dimension_semantics=("arbitrary", "arbitrary"), vmem_limit_bytes=VMEM_LIMIT),
        name="lru_bwd" if reverse else "lru_fwd",
    )(*map(_operand, args))


def _gla_constants(reverse):
    c, w = CHUNK, SUBLANES
    t = np.arange(c)[:, None]
    s = np.arange(c)[None, :]
    gt, gs = t // w, s // w
    if reverse:
        code = np.where((gt == gs) & (s > t), s - t, np.where(gs > gt, w + gs, -1))
        blocks = [s >= t] + [(s >= t) & (s < t + d) for d in range(1, w)]
    else:
        code = np.where((gt == gs) & (s <= t), t - s, np.where(gs < gt, w + gs, -1))
        blocks = [s <= t] + [(s > t - d) & (s <= t) for d in range(1, w)]
    lmat = np.concatenate([blk.astype(np.float32) for blk in blocks], axis=0)
    lmat = np.concatenate([lmat, lmat], axis=1)
    code = np.tile(code.astype(np.int32), (1, GLA_HEADS))
    hmask = np.kron(np.eye(GLA_HEADS, dtype=np.float32), np.ones((c, GLA_DK), np.float32))
    return jnp.asarray(lmat, jnp.bfloat16), jnp.asarray(code), jnp.asarray(hmask, jnp.bfloat16)


def _gla_decays(g, lmat):
    ghi, glo = _split_bf16(g)
    return jnp.dot(lmat, jnp.concatenate([ghi, glo], axis=0), preferred_element_type=jnp.float32)


def _gla_scores(q, k, logdec, hmask, reverse):
    n_groups = CHUNK // SUBLANES

    def head_stack(x):
        return jnp.concatenate([x.astype(jnp.bfloat16)] * GLA_HEADS, axis=0) * hmask

    nt_dims = (((1,), (1,)), ((), ()))
    b = logdec[0:CHUNK, :]

    lhs = [q] if not reverse else []
    for d in range(1, SUBLANES):
        lhs.append(q * jnp.exp2(logdec[d * CHUNK:(d + 1) * CHUNK, :]))
    p_within = lax.dot_general(jnp.concatenate(lhs, axis=0).astype(jnp.bfloat16), head_stack(k),
                               nt_dims, preferred_element_type=jnp.float32)

    anchor_row = (lambda gi: gi * SUBLANES) if reverse else (lambda gi: gi * SUBLANES + SUBLANES - 1)
    anch = [b[anchor_row(gi):anchor_row(gi) + 1, :] for gi in range(n_groups)]
    b_anchor = jnp.concatenate([jnp.broadcast_to(a, (SUBLANES, GLA_QK)) for a in anch], axis=0)
    groups = list(range(1, n_groups)) if reverse else list(range(n_groups - 1))
    row_sel = (lambda gi: slice(0, gi * SUBLANES)) if reverse else \
              (lambda gi: slice((gi + 1) * SUBLANES, CHUNK))
    lhs = [q[row_sel(gi), :] * jnp.exp2(b[row_sel(gi), :] - anch[gi]) for gi in groups]
    k_anch = k * jnp.exp2(b_anchor - b)
    p_cross = lax.dot_general(jnp.concatenate(lhs, axis=0).astype(jnp.bfloat16), head_stack(k_anch),
                              nt_dims, preferred_element_type=jnp.float32)

    last = 0 if reverse else CHUNK - 1
    b_last = b[last:last + 1, :]
    qe = (q * jnp.exp2(b)).astype(jnp.bfloat16)
    ke = (k * jnp.exp2(b_last - b)).astype(jnp.bfloat16)
    return p_within, p_cross, qe, ke, jnp.exp2(b_last)


def _gla_apply(p_within, p_cross, qe, ke, e_last, vb, s_ref, code, reverse):
    n_groups = CHUNK // SUBLANES
    offsets = list(range(1 if reverse else 0, SUBLANES))
    groups = list(range(1, n_groups)) if reverse else list(range(n_groups - 1))
    scores = jnp.zeros((CHUNK, GLA_HEADS * CHUNK), jnp.float32)
    for i, d in enumerate(offsets):
        scores = jnp.where(code == d, p_within[i * CHUNK:(i + 1) * CHUNK, :], scores)
    off = 0
    for gi in groups:
        rows = slice(0, gi * SUBLANES) if reverse else slice((gi + 1) * SUBLANES, CHUNK)
        n = rows.stop - rows.start
        part = jnp.where(code[rows, :] == SUBLANES + gi, p_cross[off:off + n, :], scores[rows, :])
        scores = jnp.concatenate([part, scores[n:, :]] if reverse else [scores[:CHUNK - n, :], part],
                                 axis=0)
        off += n
    sb = scores.astype(jnp.bfloat16)

    eye = (lax.broadcasted_iota(jnp.int32, (GLA_DK, GLA_DK), 0)
           == lax.broadcasted_iota(jnp.int32, (GLA_DK, GLA_DK), 1))
    zero_v = jnp.zeros((CHUNK, GLA_DV), jnp.bfloat16)
    zero_s = jnp.zeros((GLA_DK, GLA_DV), jnp.bfloat16)
    outs = []
    for pair in range(GLA_HEADS // 2):
        heads = (2 * pair, 2 * pair + 1)
        vh = [vb[:, hd * GLA_DV:(hd + 1) * GLA_DV] for hd in heads]
        st = [s_ref[hd] for hd in heads]
        sh = [s.astype(jnp.bfloat16) for s in st]
        lanes = slice(pair * 2 * CHUNK, (pair + 1) * 2 * CHUNK)
        lhs = jnp.concatenate([sb[:, lanes], qe[:, lanes]], axis=1)
        rhs = jnp.concatenate([jnp.concatenate([vh[0], zero_v], axis=1),
                               jnp.concatenate([zero_v, vh[1]], axis=1),
                               jnp.concatenate([sh[0], zero_s], axis=1),
                               jnp.concatenate([zero_s, sh[1]], axis=1)], axis=0)
        outs.append(jnp.dot(lhs, rhs, preferred_element_type=jnp.float32))
        for i, hd in enumerate(heads):
            ks = slice(hd * GLA_DK, (hd + 1) * GLA_DK)
            kv = lax.dot_general(ke[:, ks], vh[i], (((0,), (0,)), ((), ())),
                                 preferred_element_type=jnp.float32)
            e_col = jnp.sum(jnp.where(eye, e_last[:, ks], 0.0), axis=1, keepdims=True)
            s_ref[hd] = st[i] * e_col + kv
    return jnp.concatenate(outs, axis=1)


def _gla_kernel(*refs, reverse):
    if reverse:
        (qk_ref, v_ref, g_ref, lmat_ref, code_ref, hmask_ref, of_ref, go_ref, hn_ref,
         out_ref, s_ref) = refs
    else:
        qk_ref, v_ref, g_ref, lmat_ref, code_ref, hmask_ref, out_ref, s_ref = refs
    bsz, tt, _ = qk_ref.shape
    n_chunks = tt // CHUNK

    @pl.when(pl.program_id(0) == 0)
    def _():
        s_ref[...] = jnp.zeros_like(s_ref)

    lmat = lmat_ref[...]
    code = code_ref[...]
    hmask = hmask_ref[...]

    def chunk_body(j, carry):
        work = []
        for u in range(GLA_UNROLL):
            jj = j * GLA_UNROLL + u
            cidx = (n_chunks - 1 - jj) if reverse else jj
            rows = pl.ds(pl.multiple_of(cidx * CHUNK, CHUNK), CHUNK)
            work += [(bi, rows) for bi in range(bsz)]
        decs = [_gla_decays(g_ref[bi, rows, :], lmat) for bi, rows in work]
        terms = [_gla_scores(qk_ref[bi, rows, 0:GLA_QK], qk_ref[bi, rows, GLA_QK:2 * GLA_QK],
                             dec, hmask, reverse) for (bi, rows), dec in zip(work, decs)]
        for (bi, rows), term in zip(work, terms):
            o = _gla_apply(*term, v_ref[bi, rows, :].astype(jnp.bfloat16), s_ref.at[bi], code,
                           reverse)
            if reverse:
                o = o + of_ref[bi, rows, :]
                parts = []
                for hd in range(GLA_HEADS):
                    oh = o[:, hd * GLA_DV:(hd + 1) * GLA_DV]
                    parts.append(oh * _rms_scale(oh))
                y = jnp.concatenate(parts, axis=1) * hn_ref[...] * go_ref[bi, rows, :].astype(jnp.float32)
                out_ref[bi, rows, :] = y.astype(out_ref.dtype)
            else:
                out_ref[bi, rows, :] = o
        return carry

    lax.fori_loop(0, n_chunks // GLA_UNROLL, chunk_body, 0)


def _gla_pass(qk, v, gfb, *, reverse, o_f=None, go=None, head_norm=None):
    bsz, tp, _ = qk.shape
    tt = GLA_TIME_TILE
    nt = tp // tt
    tsel = (lambda t: nt - 1 - t) if reverse else (lambda t: t)
    blk = lambda w, c=0: pl.BlockSpec((bsz, tt, w), lambda t: (0, tsel(t), c))
    consts = _gla_constants(reverse)
    in_specs = [blk(512), blk(512), blk(GLA_QK, 1 if reverse else 0)]
    in_specs += [_const_spec(a) for a in consts]
    args = [qk, v, gfb, *consts]
    if reverse:
        in_specs += [blk(512), blk(512), _const_spec(head_norm)]
        args += [o_f, go, head_norm]
        out_dtype = jnp.bfloat16
    else:
        out_dtype = jnp.float32
    return pl.pallas_call(
        functools.partial(_gla_kernel, reverse=reverse),
        grid=(nt,),
        in_specs=in_specs,
        out_specs=blk(512),
        out_shape=jax.ShapeDtypeStruct((bsz, tp, GLA_WIDTH), out_dtype),
        scratch_shapes=[pltpu.VMEM((bsz, GLA_HEADS, GLA_DK, GLA_DV), jnp.float32)],
        compiler_params=pltpu.CompilerParams(
            dimension_semantics=("arbitrary",), vmem_limit_bytes=VMEM_LIMIT),
        name="gla_bwd" if reverse else "gla_fwd",
    )(*map(_operand, args))


def _out_mlp_kernel(h_ref, yl_ref, yg_ref, wol_ref, wog_ref, gpost_ref, gpre_ref, gpost2_ref,
                    wup_ref, wdn_ref, out_ref, acc_ref, *, rows_per_batch, n_batch, per_batch_grid):
    tm = h_ref.shape[0]
    mix = (jnp.dot(yl_ref[...], wol_ref[...], preferred_element_type=jnp.float32)
           + jnp.dot(yg_ref[...], wog_ref[...], preferred_element_type=jnp.float32))
    tile = pl.program_id(0)
    if per_batch_grid:
        tile = tile * pl.num_programs(1) + pl.program_id(1)
    row = lax.broadcasted_iota(jnp.int32, (tm, 1), 0) + tile * tm
    real = row >= PADF
    for b in range(1, n_batch):
        real = real & ((row < b * rows_per_batch) | (row >= b * rows_per_batch + PADF))
    h1 = h_ref[...] + jnp.where(real, mix * _rms_scale(mix) * gpost_ref[...], 0.0)
    xn = (h1 * _rms_scale(h1) * gpre_ref[...]).astype(jnp.bfloat16)
    for c in range(D_FF // FF_TILE):
        cs = slice(c * FF_TILE, (c + 1) * FF_TILE)
        up = jnp.dot(xn, wup_ref[:, cs], preferred_element_type=jnp.float32)
        act = jnp.square(jnp.maximum(up, 0.0)).astype(jnp.bfloat16)
        part = jnp.dot(act, wdn_ref[cs, :], preferred_element_type=jnp.float32)
        if c == 0:
            acc_ref[...] = part
        else:
            acc_ref[...] += part
    ff = acc_ref[...]
    out_ref[...] = h1 + ff * _rms_scale(ff) * gpost2_ref[...]


def _out_mlp(h, y_lru, y_gla, wo_l, wo_g, g_post, g_pre, g_post2, w_up, w_dn, rows_per_batch,
             drop_lead):
    rows = h.shape[0]
    n_batch = rows // rows_per_batch
    consts = (wo_l, wo_g, g_post, g_pre, g_post2, w_up, w_dn)
    if drop_lead:
        tm = T0
        per_batch = rows_per_batch // tm
        grid = (n_batch, per_batch)
        row = lambda w: pl.BlockSpec((tm, w), lambda b, j: (b * per_batch + j, 0))
        out_spec = pl.BlockSpec(
            (tm, D_MODEL), lambda b, j: (b * (per_batch - 1) + jnp.maximum(j - 1, 0), 0))
        out_rows = rows - n_batch * tm
    else:
        tm = ROW_TILE
        grid = (rows // tm,)
        row = lambda w: pl.BlockSpec((tm, w), lambda i: (i, 0))
        out_spec = row(D_MODEL)
        out_rows = rows
    return pl.pallas_call(
        functools.partial(_out_mlp_kernel, rows_per_batch=rows_per_batch, n_batch=n_batch,
                          per_batch_grid=drop_lead),
        grid=grid,
        in_specs=[row(D_MODEL), row(512), row(512)] + [_const_spec(a) for a in consts],
        out_specs=out_spec,
        out_shape=jax.ShapeDtypeStruct((out_rows, D_MODEL), jnp.float32),
        scratch_shapes=[pltpu.VMEM((tm, D_MODEL), jnp.float32)],
        compiler_params=pltpu.CompilerParams(
            dimension_semantics=("arbitrary",) * len(grid), vmem_limit_bytes=VMEM_LIMIT),
        name="out_mlp",
    )(h, y_lru, y_gla, *map(_operand, consts))


def _block_diag_gates(wa, wx):
    per_half = LRU_HEADS // 2
    eye = jnp.eye(per_half, dtype=wa.dtype)

    def bd(w):
        w5 = w.reshape(-1, 2, per_half, LRU_HEAD_DIM, LRU_HEAD_DIM)
        return jnp.einsum('nhjil,jk->nhjikl', w5, eye).reshape(-1, 2, LRU_WIDTH // 2, LRU_WIDTH // 2)
    return jnp.concatenate([bd(wa), bd(wx)], axis=3).astype(jnp.bfloat16)


def kernel(x, meta_tokens, norm_mix_pre, norm_mix_post, norm_mlp_pre, norm_mlp_post,
           w_in, conv_w, conv_b,
           lru_wa_f, lru_ba_f, lru_wx_f, lru_bx_f, lru_lambda_f,
           lru_wa_b, lru_ba_b, lru_wx_b, lru_bx_b, lru_lambda_b,
           gla_wg_f, gla_bg_f, gla_wg_b, gla_bg_b, gla_head_norm,
           w_out, w_mlp_up, w_mlp_down):
    bsz, seq, d = x.shape
    depth = w_in.shape[0]
    tp = seq + T0
    assert d == D_MODEL and tp % TIME_TILE == 0 and tp % GLA_TIME_TILE == 0
    assert (bsz * tp) % ROW_TILE == 0
    meta = jnp.broadcast_to(meta_tokens.astype(x.dtype)[None], (bsz, N_META, d))
    h = jnp.concatenate([jnp.zeros((bsz, PADF, d), x.dtype), meta, x], axis=1)
    h = h.reshape(bsz * tp, d)
    vec = lambda a: a.reshape(depth, 1, -1)
    bf = lambda a: a.astype(jnp.bfloat16)

    w_main = bf(w_in[:, :, :2560])
    w_zg = bf(jnp.tile(w_in[:, :, 2560:2592], (1, 1, 3)))
    wg = jnp.zeros((depth, 2 * GLA_RANK, 2 * GLA_QK), jnp.float32)
    wg = wg.at[:, :GLA_RANK, :GLA_QK].set(gla_wg_f).at[:, GLA_RANK:, GLA_QK:].set(gla_wg_b)
    wg_hi, wg_lo = _split_bf16(wg)
    wg3 = jnp.concatenate([wg_hi, wg_lo, wg_hi], axis=1)
    bg = vec(jnp.concatenate([gla_bg_f, gla_bg_b], axis=1))
    gates_f = _block_diag_gates(lru_wa_f, lru_wx_f)
    gates_b = _block_diag_gates(lru_wa_b, lru_wx_b)
    params = dict(
        g_pre=vec(norm_mix_pre), w_main=w_main, w_zg=w_zg, wg3=wg3, bg=bg,
        conv_w=conv_w, conv_b=vec(conv_b),
        lru_f=(gates_f, vec(lru_ba_f), vec(lru_bx_f), vec(lru_lambda_f)),
        lru_b=(gates_b, vec(lru_ba_b), vec(lru_bx_b), vec(lru_lambda_b)),
        head_norm=vec(gla_head_norm),
        out=(bf(w_out[:, :LRU_WIDTH]), bf(w_out[:, LRU_WIDTH:]), vec(norm_mix_post),
             vec(norm_mlp_pre), vec(norm_mlp_post), bf(w_mlp_up), bf(w_mlp_down)))

    for l in range(depth):
        at = lambda a: _Layer(a, l)
        p = jax.tree.map(at, params)
        outs = _in_proj(h, p['g_pre'], p['w_main'], p['w_zg'], p['wg3'], p['bg'],
                        p['conv_w'], p['conv_b'])
        xc3, gate3, qk3, v3, go3, gfb3 = (a.reshape(bsz, tp, a.shape[-1]) for a in outs)

        h_f = _lru_pass(xc3, *p['lru_f'], reverse=False)
        y_lru = _lru_pass(xc3, *p['lru_b'], reverse=True, hf=h_f, gate=gate3)
        o_f = _gla_pass(qk3, v3, gfb3, reverse=False)
        y_gla = _gla_pass(qk3, v3, gfb3, reverse=True, o_f=o_f, go=go3, head_norm=p['head_norm'])

        h = _out_mlp(h, y_lru.reshape(bsz * tp, -1), y_gla.reshape(bsz * tp, -1), *p['out'],
                     tp, drop_lead=(l == depth - 1))
    return h.reshape(bsz, seq, d)
```

```python
import functools

import numpy as np
import jax
import jax.numpy as jnp
from jax import lax
from jax.experimental import pallas as pl
from jax.experimental.pallas import tpu as pltpu

D_MODEL = 1024
N_META = 16
CHUNK = 64
LRU_WIDTH = 512
LRU_HEADS = 8
LRU_HEAD_DIM = 64
LRU_C = 8.0
GLA_WIDTH = 512
GLA_HEADS = 4
GLA_DV = 128
GLA_DK = 64
GLA_QK = GLA_HEADS * GLA_DK
GLA_RANK = 16
GLA_GATE_NORM = 16.0
D_FF = 4096
EPS = 1e-6
LOG2_E = 1.4426950408889634

LEAD_CHUNKS = 4
T0 = LEAD_CHUNKS * CHUNK
PADF = T0 - N_META
ROW_TILE = 768
TIME_TILE = 1056
GLA_TIME_TILE = 768
GLA_UNROLL = 3
FF_TILE = 1024
SUBLANES = 8
LANES = 128
VMEM_LIMIT = 56 * 1024 * 1024


class _Layer:
    def __init__(self, stacked, layer):
        self.stacked, self.layer, self.shape = stacked, layer, tuple(stacked.shape[1:])


def _operand(a):
    return a.stacked if isinstance(a, _Layer) else a


def _const_spec(a):
    nd = len(a.shape)
    if isinstance(a, _Layer):
        layer = a.layer
        return pl.BlockSpec((None,) + a.shape, lambda *_: (layer,) + (0,) * nd,
                            pipeline_mode=pl.Buffered(1))
    return pl.BlockSpec(a.shape, lambda *_: (0,) * nd, pipeline_mode=pl.Buffered(1))


def _rms_scale(x):
    return lax.rsqrt(jnp.mean(x * x, axis=-1, keepdims=True) + EPS)


def _sigmoid(x):
    return 1.0 / (1.0 + jnp.exp(-x))


def _softplus(x):
    return jnp.maximum(x, 0.0) + jnp.log1p(jnp.exp(-jnp.abs(x)))


def _split_bf16(x):
    hi = x.astype(jnp.bfloat16)
    lo = (x - hi.astype(jnp.float32)).astype(jnp.bfloat16)
    return hi, lo


def _in_proj_kernel(h_ref, hp_ref, hn_ref, g_ref, w_ref, wzg_ref, wg_ref, bg_ref, cw_ref, cb_ref,
                    xc_ref, gate_ref, qk_ref, v_ref, go_ref, gfb_ref, xs_ref, *, n_tiles):
    tm = h_ref.shape[0]
    normed = lambda x: (x * _rms_scale(x) * g_ref[...]).astype(jnp.bfloat16)
    halo = jnp.concatenate([hp_ref[...], hn_ref[...]], axis=0)
    xn = normed(h_ref[...])
    xn_ext = jnp.concatenate([xn, normed(halo)], axis=0)

    zg = jnp.dot(xn, wzg_ref[...], preferred_element_type=jnp.float32)
    zx = jnp.dot(xn_ext, w_ref[:, 0:LRU_WIDTH], preferred_element_type=jnp.float32)
    zhi = zg.astype(jnp.bfloat16)
    zlo = (zg - zhi.astype(jnp.float32)).astype(jnp.bfloat16)
    lane = lax.broadcasted_iota(jnp.int32, zg.shape, 1)
    lhs = jnp.where(lane < 4 * GLA_RANK, zhi, zlo)
    pre = jnp.dot(lhs, wg_ref[...], preferred_element_type=jnp.float32) + bg_ref[...]
    z = jnp.dot(xn, w_ref[:, LRU_WIDTH:], preferred_element_type=jnp.float32)

    xs_ref[0:SUBLANES, :] = zx[tm:tm + SUBLANES, :]
    xs_ref[SUBLANES:SUBLANES + tm, :] = zx[0:tm, :]
    xs_ref[SUBLANES + tm:2 * SUBLANES + tm, :] = jnp.where(
        pl.program_id(0) == n_tiles - 1, 0.0, zx[tm + SUBLANES:tm + 2 * SUBLANES, :])
    cw = cw_ref[...]
    xc_ref[...] = (cw[0:1, :] * xs_ref[SUBLANES - 2:SUBLANES - 2 + tm, :]
                   + cw[1:2, :] * xs_ref[SUBLANES - 1:SUBLANES - 1 + tm, :]
                   + cw[2:3, :] * xs_ref[SUBLANES:SUBLANES + tm, :]
                   + cw[3:4, :] * xs_ref[SUBLANES + 1:SUBLANES + 1 + tm, :]
                   + cb_ref[...])
    logsig = jnp.minimum(pre, 0.0) - jnp.log1p(jnp.exp(-jnp.abs(pre)))
    gfb_ref[...] = logsig * (LOG2_E / GLA_GATE_NORM)

    gt = z[:, 0:512]
    gelu = 0.5 * gt * (1.0 + jnp.tanh(0.7978845608028654 * (gt + 0.044715 * gt * gt * gt)))
    gate_ref[...] = gelu.astype(jnp.bfloat16)
    q = z[:, 512:768] * (GLA_DK ** -0.5)
    qk_ref[...] = jnp.concatenate([q, z[:, 768:1024]], axis=1).astype(jnp.bfloat16)
    v_ref[...] = z[:, 1024:1536].astype(jnp.bfloat16)
    go = z[:, 1536:2048]
    go_ref[...] = (go * _sigmoid(go)).astype(jnp.bfloat16)


def _in_proj(h, g, w_main, w_zg, wg3, bg, conv_w, conv_b):
    rows = h.shape[0]
    tm = ROW_TILE
    nt = rows // tm
    g8 = tm // SUBLANES
    row = lambda w: pl.BlockSpec((tm, w), lambda i: (i, 0))
    before = pl.BlockSpec((SUBLANES, D_MODEL), lambda i: (jnp.maximum(i * g8 - 1, 0), 0))
    after = pl.BlockSpec((SUBLANES, D_MODEL),
                         lambda i: (jnp.minimum((i + 1) * g8, rows // SUBLANES - 1), 0))
    consts = (g, w_main, w_zg, wg3, bg, conv_w, conv_b)
    outs = [(512, jnp.float32)] + [(512, jnp.bfloat16)] * 4 + [(512, jnp.float32)]
    return pl.pallas_call(
        functools.partial(_in_proj_kernel, n_tiles=nt),
        grid=(nt,),
        in_specs=[row(D_MODEL), before, after] + [_const_spec(a) for a in consts],
        out_specs=[row(w) for w, _ in outs],
        out_shape=[jax.ShapeDtypeStruct((rows, w), dt) for w, dt in outs],
        scratch_shapes=[pltpu.VMEM((tm + 2 * SUBLANES, LRU_WIDTH), jnp.float32)],
        compiler_params=pltpu.CompilerParams(
            dimension_semantics=("arbitrary",), vmem_limit_bytes=VMEM_LIMIT),
        name="in_proj",
    )(h, h, h, *map(_operand, consts))


def _lru_kernel(*refs, reverse, n_tiles):
    if reverse:
        (xc_ref, wg_ref, ba_ref, bx_ref, lam_ref, hf_ref, gate_ref,
         out_ref, a_scr, u_scr, carry_ref) = refs
    else:
        xc_ref, wg_ref, ba_ref, bx_ref, lam_ref, out_ref, a_scr, u_scr, carry_ref = refs
    tt = xc_ref.shape[0]
    step = pl.program_id(1)
    tile = (n_tiles - 1 - step) if reverse else step

    @pl.when(step == 0)
    def _():
        carry_ref[...] = jnp.zeros_like(carry_ref)

    xc = xc_ref[...]
    row = lax.broadcasted_iota(jnp.int32, xc.shape, 0)
    xcb = xc.astype(jnp.bfloat16)
    half = LRU_WIDTH // 2
    p0 = jnp.dot(xcb[:, :half], wg_ref[0], preferred_element_type=jnp.float32)
    p1 = jnp.dot(xcb[:, half:], wg_ref[1], preferred_element_type=jnp.float32)
    r_pre = jnp.concatenate([p0[:, :half], p1[:, :half]], axis=1)
    i_pre = jnp.concatenate([p0[:, half:], p1[:, half:]], axis=1)
    r = _sigmoid(r_pre + ba_ref[...])
    gi = _sigmoid(i_pre + bx_ref[...])
    log_a = (-LRU_C) * r * _softplus(-lam_ref[...])
    a = jnp.exp(log_a)
    u = jnp.sqrt(-jnp.tanh(log_a) * (1.0 + a * a)) * (gi * xc)
    if not reverse:
        u = jnp.where(row + tile * tt >= PADF, u, 0.0)

    run = tt // SUBLANES
    n_slabs = LRU_WIDTH // LANES
    for k in range(n_slabs):
        a_scr[k] = a[:, k * LANES:(k + 1) * LANES]
        u_scr[k] = u[:, k * LANES:(k + 1) * LANES]
    steps = lambda j: pl.ds(j, SUBLANES, stride=run)

    def local_scan(jj, carry):
        j = (run - 1 - jj) if reverse else jj
        state, prod = carry
        new_state, new_prod = [], []
        for k in range(n_slabs):
            aj = a_scr[k, steps(j), :]
            sk = aj * state[k] + u_scr[k, steps(j), :]
            pk = prod[k] * aj
            u_scr[k, steps(j), :] = sk
            a_scr[k, steps(j), :] = pk
            new_state.append(sk)
            new_prod.append(pk)
        return jnp.stack(new_state), jnp.stack(new_prod)

    zeros = jnp.zeros((n_slabs, SUBLANES, LANES), jnp.float32)
    end_state, end_prod = lax.fori_loop(0, run, local_scan, (zeros, zeros + 1.0), unroll=4)

    c = carry_ref[...]
    order = range(SUBLANES - 1, -1, -1) if reverse else range(SUBLANES)
    entering = [None] * SUBLANES
    for r in order:
        entering[r] = c
        c = end_prod[:, r:r + 1, :] * c + end_state[:, r:r + 1, :]
    carry_ref[...] = c
    entering = jnp.concatenate(entering, axis=1)

    def add_entering(j, carry):
        for k in range(n_slabs):
            u_scr[k, steps(j), :] = u_scr[k, steps(j), :] + a_scr[k, steps(j), :] * entering[k]
        return carry

    lax.fori_loop(0, run, add_entering, 0, unroll=4)

    hs = jnp.concatenate([u_scr[k] for k in range(n_slabs)], axis=1)
    if reverse:
        out_ref[...] = ((hf_ref[...] + hs) * gate_ref[...].astype(jnp.float32)).astype(out_ref.dtype)
    else:
        out_ref[...] = hs


def _lru_pass(xc, wgate, ba, bx, lam, *, reverse, hf=None, gate=None):
    bsz, tp, _ = xc.shape
    tt = TIME_TILE
    nt = tp // tt
    tsel = (lambda t: nt - 1 - t) if reverse else (lambda t: t)
    cur = pl.BlockSpec((None, tt, LRU_WIDTH), lambda b, t: (b, tsel(t), 0))
    in_specs = [cur] + [_const_spec(a) for a in (wgate, ba, bx, lam)]
    args = [xc, wgate, ba, bx, lam]
    if reverse:
        in_specs += [cur, cur]
        args += [hf, gate]
        out_dtype = jnp.bfloat16
    else:
        out_dtype = jnp.float32
    return pl.pallas_call(
        functools.partial(_lru_kernel, reverse=reverse, n_tiles=nt),
        grid=(bsz, nt),
        in_specs=in_specs,
        out_specs=cur,
        out_shape=jax.ShapeDtypeStruct((bsz, tp, LRU_WIDTH), out_dtype),
        scratch_shapes=[pltpu.VMEM((LRU_WIDTH // LANES, tt, LANES), jnp.float32),
                        pltpu.VMEM((LRU_WIDTH // LANES, tt, LANES), jnp.float32),
                        pltpu.VMEM((LRU_WIDTH // LANES, 1, LANES), jnp.float32)],
        compiler_params=pltpu.CompilerParams(
            dimension_semantics=("arbitrary", "arbitrary"), vmem_limit_bytes=VMEM_LIMIT),
        name="lru_bwd" if reverse else "lru_fwd",
    )(*map(_operand, args))


def _gla_constants(reverse):
    c, w = CHUNK, SUBLANES
    t = np.arange(c)[:, None]
    s = np.arange(c)[None, :]
    gt, gs, pt, ps = t // w, s // w, t // 2, s // 2
    pair_code = 2 + (s % w) // 2
    if reverse:
        code = np.where((pt == ps) & (s > t), 1,
                        np.where((gt == gs) & (ps > pt), pair_code, np.where(gs > gt, w + gs, -1)))
        blocks = [s >= t, (t % 2 == 1) & (s == t - 1)]
    else:
        code = np.where((pt == ps) & (s <= t), t - s,
                        np.where((gt == gs) & (ps < pt), pair_code, np.where(gs < gt, w + gs, -1)))
        blocks = [s <= t, (t % 2 == 0) & (s == t + 1)]
    lmat = np.concatenate([blk.astype(np.float32) for blk in blocks], axis=0)
    lmat = np.concatenate([lmat, lmat], axis=1)
    code = np.tile(code.astype(np.int32), (1, GLA_HEADS))
    hmask = np.kron(np.eye(GLA_HEADS, dtype=np.float32), np.ones((c, GLA_DK), np.float32))
    return jnp.asarray(lmat, jnp.bfloat16), jnp.asarray(code), jnp.asarray(hmask, jnp.bfloat16)


def _gla_decays(g, lmat):
    ghi, glo = _split_bf16(g)
    return jnp.dot(lmat, jnp.concatenate([ghi, glo], axis=0), preferred_element_type=jnp.float32)


def _gla_scores(q, k, g, logdec, hmask, reverse):
    n_groups = CHUNK // SUBLANES
    pairs = SUBLANES // 2

    def head_stack(x):
        return jnp.concatenate([x.astype(jnp.bfloat16)] * GLA_HEADS, axis=0) * hmask

    def terms(lhs, rhs):
        return lax.dot_general(jnp.concatenate(lhs, axis=0).astype(jnp.bfloat16), head_stack(rhs),
                               (((1,), (1,)), ((), ())), preferred_element_type=jnp.float32)

    def facing_rows(first):
        return jnp.concatenate([jnp.broadcast_to(b[first + gi * SUBLANES:first + gi * SUBLANES + 1, :],
                                                 (SUBLANES, GLA_QK)) for gi in range(n_groups)], axis=0)

    b = logdec[0:CHUNK, :]

    step = q * jnp.exp2(g)
    p_pair = terms([step] if reverse else [q, step], k)

    k_pair = k * jnp.exp2(logdec[CHUNK:2 * CHUNK, :])
    pair_ids = range(1, pairs) if reverse else range(pairs - 1)
    lhs = [q * jnp.exp2(jnp.minimum(b - facing_rows(2 * p if reverse else 2 * p + 1), 0.0))
           for p in pair_ids]
    p_group = terms(lhs, k_pair)

    b_anchor = facing_rows(0 if reverse else SUBLANES - 1)
    groups = list(range(1, n_groups)) if reverse else list(range(n_groups - 1))
    row_sel = (lambda gi: slice(0, gi * SUBLANES)) if reverse else \
              (lambda gi: slice((gi + 1) * SUBLANES, CHUNK))
    anchor_row = (lambda gi: gi * SUBLANES) if reverse else (lambda gi: gi * SUBLANES + SUBLANES - 1)
    lhs = [q[row_sel(gi), :] * jnp.exp2(b[row_sel(gi), :] - b[anchor_row(gi):anchor_row(gi) + 1, :])
           for gi in groups]
    p_cross = terms(lhs, k * jnp.exp2(b_anchor - b))

    last = 0 if reverse else CHUNK - 1
    b_last = b[last:last + 1, :]
    qe = (q * jnp.exp2(b)).astype(jnp.bfloat16)
    ke = (k * jnp.exp2(b_last - b)).astype(jnp.bfloat16)
    return p_pair, p_group, p_cross, qe, ke, jnp.exp2(b_last)


def _gla_apply(p_pair, p_group, p_cross, qe, ke, e_last, vb, s_ref, code, reverse):
    n_groups = CHUNK // SUBLANES
    pairs = SUBLANES // 2
    groups = list(range(1, n_groups)) if reverse else list(range(n_groups - 1))
    scores = jnp.zeros((CHUNK, GLA_HEADS * CHUNK), jnp.float32)
    for i, d in enumerate([1] if reverse else [0, 1]):
        scores = jnp.where(code == d, p_pair[i * CHUNK:(i + 1) * CHUNK, :], scores)
    for i, p in enumerate(range(1, pairs) if reverse else range(pairs - 1)):
        scores = jnp.where(code == 2 + p, p_group[i * CHUNK:(i + 1) * CHUNK, :], scores)
    off = 0
    for gi in groups:
        rows = slice(0, gi * SUBLANES) if reverse else slice((gi + 1) * SUBLANES, CHUNK)
        n = rows.stop - rows.start
        part = jnp.where(code[rows, :] == SUBLANES + gi, p_cross[off:off + n, :], scores[rows, :])
        scores = jnp.concatenate([part, scores[n:, :]] if reverse else [scores[:CHUNK - n, :], part],
                                 axis=0)
        off += n
    sb = scores.astype(jnp.bfloat16)

    eye = (lax.broadcasted_iota(jnp.int32, (GLA_DK, GLA_DK), 0)
           == lax.broadcasted_iota(jnp.int32, (GLA_DK, GLA_DK), 1))
    zero_v = jnp.zeros((CHUNK, GLA_DV), jnp.bfloat16)
    zero_s = jnp.zeros((GLA_DK, GLA_DV), jnp.bfloat16)
    outs = []
    for pair in range(GLA_HEADS // 2):
        heads = (2 * pair, 2 * pair + 1)
        vh = [vb[:, hd * GLA_DV:(hd + 1) * GLA_DV] for hd in heads]
        st = [s_ref[hd] for hd in heads]
        sh = [s.astype(jnp.bfloat16) for s in st]
        lanes = slice(pair * 2 * CHUNK, (pair + 1) * 2 * CHUNK)
        lhs = jnp.concatenate([sb[:, lanes], qe[:, lanes]], axis=1)
        rhs = jnp.concatenate([jnp.concatenate([vh[0], zero_v], axis=1),
                               jnp.concatenate([zero_v, vh[1]], axis=1),
                               jnp.concatenate([sh[0], zero_s], axis=1),
                               jnp.concatenate([zero_s, sh[1]], axis=1)], axis=0)
        outs.append(jnp.dot(lhs, rhs, preferred_element_type=jnp.float32))
        for i, hd in enumerate(heads):
            ks = slice(hd * GLA_DK, (hd + 1) * GLA_DK)
            kv = lax.dot_general(ke[:, ks], vh[i], (((0,), (0,)), ((), ())),
                                 preferred_element_type=jnp.float32)
            e_col = jnp.sum(jnp.where(eye, e_last[:, ks], 0.0), axis=1, keepdims=True)
            s_ref[hd] = st[i] * e_col + kv
    return jnp.concatenate(outs, axis=1)


def _gla_kernel(*refs, reverse):
    if reverse:
        (qk_ref, v_ref, g_ref, lmat_ref, code_ref, hmask_ref, of_ref, go_ref, hn_ref,
         out_ref, s_ref) = refs
    else:
        qk_ref, v_ref, g_ref, lmat_ref, code_ref, hmask_ref, out_ref, s_ref = refs
    bsz, tt, _ = qk_ref.shape
    n_chunks = tt // CHUNK

    @pl.when(pl.program_id(0) == 0)
    def _():
        s_ref[...] = jnp.zeros_like(s_ref)

    lmat = lmat_ref[...]
    code = code_ref[...]
    hmask = hmask_ref[...]

    def chunk_body(j, carry):
        work = []
        for u in range(GLA_UNROLL):
            jj = j * GLA_UNROLL + u
            cidx = (n_chunks - 1 - jj) if reverse else jj
            rows = pl.ds(pl.multiple_of(cidx * CHUNK, CHUNK), CHUNK)
            work += [(bi, rows) for bi in range(bsz)]
        decs = [_gla_decays(g_ref[bi, rows, :], lmat) for bi, rows in work]
        terms = [_gla_scores(qk_ref[bi, rows, 0:GLA_QK], qk_ref[bi, rows, GLA_QK:2 * GLA_QK],
                             g_ref[bi, rows, :], dec, hmask, reverse)
                 for (bi, rows), dec in zip(work, decs)]
        for (bi, rows), term in zip(work, terms):
            o = _gla_apply(*term, v_ref[bi, rows, :].astype(jnp.bfloat16), s_ref.at[bi], code,
                           reverse)
            if reverse:
                o = o + of_ref[bi, rows, :]
                parts = []
                for hd in range(GLA_HEADS):
                    oh = o[:, hd * GLA_DV:(hd + 1) * GLA_DV]
                    parts.append(oh * _rms_scale(oh))
                y = jnp.concatenate(parts, axis=1) * hn_ref[...] * go_ref[bi, rows, :].astype(jnp.float32)
                out_ref[bi, rows, :] = y.astype(out_ref.dtype)
            else:
                out_ref[bi, rows, :] = o
        return carry

    lax.fori_loop(0, n_chunks // GLA_UNROLL, chunk_body, 0)


def _gla_pass(qk, v, gfb, *, reverse, o_f=None, go=None, head_norm=None):
    bsz, tp, _ = qk.shape
    tt = GLA_TIME_TILE
    nt = tp // tt
    tsel = (lambda t: nt - 1 - t) if reverse else (lambda t: t)
    blk = lambda w, c=0: pl.BlockSpec((bsz, tt, w), lambda t: (0, tsel(t), c))
    consts = _gla_constants(reverse)
    in_specs = [blk(512), blk(512), blk(GLA_QK, 1 if reverse else 0)]
    in_specs += [_const_spec(a) for a in consts]
    args = [qk, v, gfb, *consts]
    if reverse:
        in_specs += [blk(512), blk(512), _const_spec(head_norm)]
        args += [o_f, go, head_norm]
        out_dtype = jnp.bfloat16
    else:
        out_dtype = jnp.float32
    return pl.pallas_call(
        functools.partial(_gla_kernel, reverse=reverse),
        grid=(nt,),
        in_specs=in_specs,
        out_specs=blk(512),
        out_shape=jax.ShapeDtypeStruct((bsz, tp, GLA_WIDTH), out_dtype),
        scratch_shapes=[pltpu.VMEM((bsz, GLA_HEADS, GLA_DK, GLA_DV), jnp.float32)],
        compiler_params=pltpu.CompilerParams(
            dimension_semantics=("arbitrary",), vmem_limit_bytes=VMEM_LIMIT),
        name="gla_bwd" if reverse else "gla_fwd",
    )(*map(_operand, args))


def _out_mlp_kernel(h_ref, yl_ref, yg_ref, wol_ref, wog_ref, gpost_ref, gpre_ref, gpost2_ref,
                    wup_ref, wdn_ref, out_ref, acc_ref, *, rows_per_batch, n_batch, per_batch_grid):
    tm = h_ref.shape[0]
    mix = (jnp.dot(yl_ref[...], wol_ref[...], preferred_element_type=jnp.float32)
           + jnp.dot(yg_ref[...], wog_ref[...], preferred_element_type=jnp.float32))
    tile = pl.program_id(0)
    if per_batch_grid:
        tile = tile * pl.num_programs(1) + pl.program_id(1)
    row = lax.broadcasted_iota(jnp.int32, (tm, 1), 0) + tile * tm
    real = row >= PADF
    for b in range(1, n_batch):
        real = real & ((row < b * rows_per_batch) | (row >= b * rows_per_batch + PADF))
    h1 = h_ref[...] + jnp.where(real, mix * _rms_scale(mix) * gpost_ref[...], 0.0)
    xn = (h1 * _rms_scale(h1) * gpre_ref[...]).astype(jnp.bfloat16)
    for c in range(D_FF // FF_TILE):
        cs = slice(c * FF_TILE, (c + 1) * FF_TILE)
        up = jnp.dot(xn, wup_ref[:, cs], preferred_element_type=jnp.float32)
        act = jnp.square(jnp.maximum(up, 0.0)).astype(jnp.bfloat16)
        part = jnp.dot(act, wdn_ref[cs, :], preferred_element_type=jnp.float32)
        if c == 0:
            acc_ref[...] = part
        else:
            acc_ref[...] += part
    ff = acc_ref[...]
    out_ref[...] = h1 + ff * _rms_scale(ff) * gpost2_ref[...]


def _out_mlp(h, y_lru, y_gla, wo_l, wo_g, g_post, g_pre, g_post2, w_up, w_dn, rows_per_batch,
             drop_lead):
    rows = h.shape[0]
    n_batch = rows // rows_per_batch
    consts = (wo_l, wo_g, g_post, g_pre, g_post2, w_up, w_dn)
    if drop_lead:
        tm = T0
        per_batch = rows_per_batch // tm
        grid = (n_batch, per_batch)
        row = lambda w: pl.BlockSpec((tm, w), lambda b, j: (b * per_batch + j, 0))
        out_spec = pl.BlockSpec(
            (tm, D_MODEL), lambda b, j: (b * (per_batch - 1) + jnp.maximum(j - 1, 0), 0))
        out_rows = rows - n_batch * tm
    else:
        tm = ROW_TILE
        grid = (rows // tm,)
        row = lambda w: pl.BlockSpec((tm, w), lambda i: (i, 0))
        out_spec = row(D_MODEL)
        out_rows = rows
    return pl.pallas_call(
        functools.partial(_out_mlp_kernel, rows_per_batch=rows_per_batch, n_batch=n_batch,
                          per_batch_grid=drop_lead),
        grid=grid,
        in_specs=[row(D_MODEL), row(512), row(512)] + [_const_spec(a) for a in consts],
        out_specs=out_spec,
        out_shape=jax.ShapeDtypeStruct((out_rows, D_MODEL), jnp.float32),
        scratch_shapes=[pltpu.VMEM((tm, D_MODEL), jnp.float32)],
        compiler_params=pltpu.CompilerParams(
            dimension_semantics=("arbitrary",) * len(grid), vmem_limit_bytes=VMEM_LIMIT),
        name="out_mlp",
    )(h, y_lru, y_gla, *map(_operand, consts))


def _block_diag_gates(wa, wx):
    per_half = LRU_HEADS // 2
    eye = jnp.eye(per_half, dtype=wa.dtype)

    def bd(w):
        w5 = w.reshape(-1, 2, per_half, LRU_HEAD_DIM, LRU_HEAD_DIM)
        return jnp.einsum('nhjil,jk->nhjikl', w5, eye).reshape(-1, 2, LRU_WIDTH // 2, LRU_WIDTH // 2)
    return jnp.concatenate([bd(wa), bd(wx)], axis=3).astype(jnp.bfloat16)


def kernel(x, meta_tokens, norm_mix_pre, norm_mix_post, norm_mlp_pre, norm_mlp_post,
           w_in, conv_w, conv_b,
           lru_wa_f, lru_ba_f, lru_wx_f, lru_bx_f, lru_lambda_f,
           lru_wa_b, lru_ba_b, lru_wx_b, lru_bx_b, lru_lambda_b,
           gla_wg_f, gla_bg_f, gla_wg_b, gla_bg_b, gla_head_norm,
           w_out, w_mlp_up, w_mlp_down):
    bsz, seq, d = x.shape
    depth = w_in.shape[0]
    tp = seq + T0
    assert d == D_MODEL and tp % TIME_TILE == 0 and tp % GLA_TIME_TILE == 0
    assert (bsz * tp) % ROW_TILE == 0
    meta = jnp.broadcast_to(meta_tokens.astype(x.dtype)[None], (bsz, N_META, d))
    h = jnp.concatenate([jnp.zeros((bsz, PADF, d), x.dtype), meta, x], axis=1)
    h = h.reshape(bsz * tp, d)
    vec = lambda a: a.reshape(depth, 1, -1)
    bf = lambda a: a.astype(jnp.bfloat16)

    w_main = bf(w_in[:, :, :2560])
    w_zg = bf(jnp.tile(w_in[:, :, 2560:2592], (1, 1, 3)))
    wg = jnp.zeros((depth, 2 * GLA_RANK, 2 * GLA_QK), jnp.float32)
    wg = wg.at[:, :GLA_RANK, :GLA_QK].set(gla_wg_f).at[:, GLA_RANK:, GLA_QK:].set(gla_wg_b)
    wg_hi, wg_lo = _split_bf16(wg)
    wg3 = jnp.concatenate([wg_hi, wg_lo, wg_hi], axis=1)
    bg = vec(jnp.concatenate([gla_bg_f, gla_bg_b], axis=1))
    gates_f = _block_diag_gates(lru_wa_f, lru_wx_f)
    gates_b = _block_diag_gates(lru_wa_b, lru_wx_b)
    params = dict(
        g_pre=vec(norm_mix_pre), w_main=w_main, w_zg=w_zg, wg3=wg3, bg=bg,
        conv_w=conv_w, conv_b=vec(conv_b),
        lru_f=(gates_f, vec(lru_ba_f), vec(lru_bx_f), vec(lru_lambda_f)),
        lru_b=(gates_b, vec(lru_ba_b), vec(lru_bx_b), vec(lru_lambda_b)),
        head_norm=vec(gla_head_norm),
        out=(bf(w_out[:, :LRU_WIDTH]), bf(w_out[:, LRU_WIDTH:]), vec(norm_mix_post),
             vec(norm_mlp_pre), vec(norm_mlp_post), bf(w_mlp_up), bf(w_mlp_down)))

    for l in range(depth):
        at = lambda a: _Layer(a, l)
        p = jax.tree.map(at, params)
        outs = _in_proj(h, p['g_pre'], p['w_main'], p['w_zg'], p['wg3'], p['bg'],
                        p['conv_w'], p['conv_b'])
        xc3, gate3, qk3, v3, go3, gfb3 = (a.reshape(bsz, tp, a.shape[-1]) for a in outs)

        h_f = _lru_pass(xc3, *p['lru_f'], reverse=False)
        y_lru = _lru_pass(xc3, *p['lru_b'], reverse=True, hf=h_f, gate=gate3)
        o_f = _gla_pass(qk3, v3, gfb3, reverse=False)
        y_gla = _gla_pass(qk3, v3, gfb3, reverse=True, o_f=o_f, go=go3, head_norm=p['head_norm'])

        h = _out_mlp(h, y_lru.reshape(bsz * tp, -1), y_gla.reshape(bsz * tp, -1), *p['out'],
                     tp, drop_lead=(l == depth - 1))
    return h.reshape(bsz, seq, d)
```

```python
import functools

import numpy as np
import jax
import jax.numpy as jnp
from jax import lax
from jax.experimental import pallas as pl
from jax.experimental.pallas import tpu as pltpu

D_MODEL = 1024
N_META = 16
CHUNK = 64
LRU_WIDTH = 512
LRU_HEADS = 8
LRU_HEAD_DIM = 64
LRU_C = 8.0
GLA_WIDTH = 512
GLA_HEADS = 4
GLA_DV = 128
GLA_DK = 64
GLA_QK = GLA_HEADS * GLA_DK
GLA_RANK = 16
GLA_GATE_NORM = 16.0
D_FF = 4096
EPS = 1e-6
LOG2_E = 1.4426950408889634
F32_TINY = float(np.finfo(np.float32).tiny)

LEAD_CHUNKS = 4
T0 = LEAD_CHUNKS * CHUNK
PADF = T0 - N_META
ROW_TILE = 768
TIME_TILE = 1056
SCAN_UNROLL = 12
GLA_TIME_TILE = 768
GLA_UNROLL = 3
FF_TILE = 1024
SUBLANES = 8
LANES = 128
VMEM_LIMIT = 56 * 1024 * 1024


class _Layer:
    def __init__(self, stacked, layer):
        self.stacked, self.layer, self.shape = stacked, layer, tuple(stacked.shape[1:])


def _operand(a):
    return a.stacked if isinstance(a, _Layer) else a


def _const_spec(a):
    nd = len(a.shape)
    if isinstance(a, _Layer):
        layer = a.layer
        return pl.BlockSpec((None,) + a.shape, lambda *_: (layer,) + (0,) * nd,
                            pipeline_mode=pl.Buffered(1))
    return pl.BlockSpec(a.shape, lambda *_: (0,) * nd, pipeline_mode=pl.Buffered(1))


def _rms_scale(x):
    return lax.rsqrt(jnp.mean(x * x, axis=-1, keepdims=True) + EPS)


def _sigmoid(x):
    return 0.5 * jnp.tanh(0.5 * x) + 0.5


def _softplus(x):
    return jnp.maximum(x, 0.0) + jnp.log1p(jnp.exp(-jnp.abs(x)))


def _split_bf16(x):
    hi = x.astype(jnp.bfloat16)
    lo = (x - hi.astype(jnp.float32)).astype(jnp.bfloat16)
    return hi, lo


def _in_proj_kernel(h_ref, hp_ref, hn_ref, g_ref, w_ref, wzg_ref, wg_ref, bg_ref, cw_ref, cb_ref,
                    xc_ref, gate_ref, qk_ref, v_ref, go_ref, gfb_ref, xs_ref, *, n_tiles):
    tm = h_ref.shape[0]
    normed = lambda x: (x * _rms_scale(x) * g_ref[...]).astype(jnp.bfloat16)
    halo = jnp.concatenate([hp_ref[...], hn_ref[...]], axis=0)
    xn = normed(h_ref[...])
    xn_ext = jnp.concatenate([xn, normed(halo)], axis=0)

    zg = jnp.dot(xn, wzg_ref[...], preferred_element_type=jnp.float32)
    zx = jnp.dot(xn_ext, w_ref[:, 0:LRU_WIDTH], preferred_element_type=jnp.float32)
    zhi = zg.astype(jnp.bfloat16)
    zlo = (zg - zhi.astype(jnp.float32)).astype(jnp.bfloat16)
    lane = lax.broadcasted_iota(jnp.int32, zg.shape, 1)
    lhs = jnp.where(lane < 4 * GLA_RANK, zhi, zlo)
    pre = jnp.dot(lhs, wg_ref[...], preferred_element_type=jnp.float32) + bg_ref[...]
    z = jnp.dot(xn, w_ref[:, LRU_WIDTH:], preferred_element_type=jnp.float32)

    xs_ref[0:SUBLANES, :] = zx[tm:tm + SUBLANES, :]
    xs_ref[SUBLANES:SUBLANES + tm, :] = zx[0:tm, :]
    xs_ref[SUBLANES + tm:2 * SUBLANES + tm, :] = jnp.where(
        pl.program_id(0) == n_tiles - 1, 0.0, zx[tm + SUBLANES:tm + 2 * SUBLANES, :])
    cw = cw_ref[...]
    xc_ref[...] = (cw[0:1, :] * xs_ref[SUBLANES - 2:SUBLANES - 2 + tm, :]
                   + cw[1:2, :] * xs_ref[SUBLANES - 1:SUBLANES - 1 + tm, :]
                   + cw[2:3, :] * xs_ref[SUBLANES:SUBLANES + tm, :]
                   + cw[3:4, :] * xs_ref[SUBLANES + 1:SUBLANES + 1 + tm, :]
                   + cb_ref[...])
    logsig = jnp.minimum(pre, 0.0) - jnp.log1p(jnp.exp(-jnp.abs(pre)))
    gfb_ref[...] = logsig * (LOG2_E / GLA_GATE_NORM)

    gt = z[:, 0:512]
    gelu = 0.5 * gt * (1.0 + jnp.tanh(0.7978845608028654 * (gt + 0.044715 * gt * gt * gt)))
    gate_ref[...] = gelu.astype(jnp.bfloat16)
    q = z[:, 512:768] * (GLA_DK ** -0.5)
    qk_ref[...] = jnp.concatenate([q, z[:, 768:1024]], axis=1).astype(jnp.bfloat16)
    v_ref[...] = z[:, 1024:1536].astype(jnp.bfloat16)
    go = z[:, 1536:2048]
    go_ref[...] = (go * _sigmoid(go)).astype(jnp.bfloat16)


def _in_proj(h, g, w_main, w_zg, wg3, bg, conv_w, conv_b):
    rows = h.shape[0]
    tm = ROW_TILE
    nt = rows // tm
    g8 = tm // SUBLANES
    row = lambda w: pl.BlockSpec((tm, w), lambda i: (i, 0))
    before = pl.BlockSpec((SUBLANES, D_MODEL), lambda i: (jnp.maximum(i * g8 - 1, 0), 0))
    after = pl.BlockSpec((SUBLANES, D_MODEL),
                         lambda i: (jnp.minimum((i + 1) * g8, rows // SUBLANES - 1), 0))
    consts = (g, w_main, w_zg, wg3, bg, conv_w, conv_b)
    outs = [(512, jnp.float32)] + [(512, jnp.bfloat16)] * 4 + [(512, jnp.float32)]
    return pl.pallas_call(
        functools.partial(_in_proj_kernel, n_tiles=nt),
        grid=(nt,),
        in_specs=[row(D_MODEL), before, after] + [_const_spec(a) for a in consts],
        out_specs=[row(w) for w, _ in outs],
        out_shape=[jax.ShapeDtypeStruct((rows, w), dt) for w, dt in outs],
        scratch_shapes=[pltpu.VMEM((tm + 2 * SUBLANES, LRU_WIDTH), jnp.float32)],
        compiler_params=pltpu.CompilerParams(
            dimension_semantics=("arbitrary",), vmem_limit_bytes=VMEM_LIMIT),
        name="in_proj",
    )(h, h, h, *map(_operand, consts))


def _lru_kernel(*refs, reverse, n_tiles):
    if reverse:
        (xc_ref, wg_ref, ba_ref, bx_ref, lam_ref, hf_ref, gate_ref,
         out_ref, a_scr, u_scr, carry_ref) = refs
    else:
        xc_ref, wg_ref, ba_ref, bx_ref, lam_ref, out_ref, a_scr, u_scr, carry_ref = refs
    bsz, tt, _ = xc_ref.shape
    step = pl.program_id(0)
    tile = (n_tiles - 1 - step) if reverse else step

    @pl.when(step == 0)
    def _():
        carry_ref[...] = jnp.zeros_like(carry_ref)

    n_slabs = LRU_WIDTH // LANES
    half = LRU_WIDTH // 2
    decay_rate = (-LRU_C) * _softplus(-lam_ref[...])
    for bi in range(bsz):
        xc = xc_ref[bi]
        xcb = xc.astype(jnp.bfloat16)
        p0 = jnp.dot(xcb[:, :half], wg_ref[0], preferred_element_type=jnp.float32)
        p1 = jnp.dot(xcb[:, half:], wg_ref[1], preferred_element_type=jnp.float32)
        r_pre = jnp.concatenate([p0[:, :half], p1[:, :half]], axis=1)
        i_pre = jnp.concatenate([p0[:, half:], p1[:, half:]], axis=1)
        r = _sigmoid(r_pre + ba_ref[...])
        gi = _sigmoid(i_pre + bx_ref[...])
        log_a = r * decay_rate
        a = jnp.exp(log_a)
        w = -jnp.tanh(log_a) * (1.0 + a * a)
        u = (w * lax.rsqrt(jnp.maximum(w, F32_TINY))) * (gi * xc)
        if not reverse:
            row = lax.broadcasted_iota(jnp.int32, xc.shape, 0)
            u = jnp.where(row + tile * tt >= PADF, u, 0.0)
        for k in range(n_slabs):
            a_scr[bi * n_slabs + k] = a[:, k * LANES:(k + 1) * LANES]
            u_scr[bi * n_slabs + k] = u[:, k * LANES:(k + 1) * LANES]

    run = tt // SUBLANES
    chains = bsz * n_slabs
    steps = lambda j: pl.ds(j, SUBLANES, stride=run)

    assert run % SCAN_UNROLL == 0
    n_blocks = run // SCAN_UNROLL
    block_base = lambda i: ((n_blocks - 1 - i) if reverse else i) * SCAN_UNROLL
    offsets = range(SCAN_UNROLL - 1, -1, -1) if reverse else range(SCAN_UNROLL)

    def run_ends(i, carry):
        state, prod = list(carry[0]), list(carry[1])
        base = block_base(i)
        for o in offsets:
            for k in range(chains):
                aj = a_scr[k, steps(base + o), :]
                state[k] = aj * state[k] + u_scr[k, steps(base + o), :]
                prod[k] = prod[k] * aj
        return tuple(state), tuple(prod)

    zeros = tuple(jnp.zeros((SUBLANES, LANES), jnp.float32) for _ in range(chains))
    ones = tuple(z + 1.0 for z in zeros)
    end_state, end_prod = lax.fori_loop(0, n_blocks, run_ends, (zeros, ones))
    end_state, end_prod = jnp.stack(end_state), jnp.stack(end_prod)

    c = carry_ref[...]
    order = range(SUBLANES - 1, -1, -1) if reverse else range(SUBLANES)
    entering = [None] * SUBLANES
    for r in order:
        entering[r] = c
        c = end_prod[:, r:r + 1, :] * c + end_state[:, r:r + 1, :]
    carry_ref[...] = c
    entering = jnp.concatenate(entering, axis=1)

    def scan(i, state):
        state = list(state)
        base = block_base(i)
        for o in offsets:
            for k in range(chains):
                state[k] = a_scr[k, steps(base + o), :] * state[k] + u_scr[k, steps(base + o), :]
                u_scr[k, steps(base + o), :] = state[k]
        return tuple(state)

    lax.fori_loop(0, n_blocks, scan, tuple(entering[k] for k in range(chains)))

    for bi in range(bsz):
        hs = jnp.concatenate([u_scr[bi * n_slabs + k] for k in range(n_slabs)], axis=1)
        if reverse:
            gate = gate_ref[bi].astype(jnp.float32)
            out_ref[bi] = ((hf_ref[bi] + hs) * gate).astype(out_ref.dtype)
        else:
            out_ref[bi] = hs


def _lru_pass(xc, wgate, ba, bx, lam, *, reverse, hf=None, gate=None):
    bsz, tp, _ = xc.shape
    tt = TIME_TILE
    nt = tp // tt
    chains = bsz * (LRU_WIDTH // LANES)
    tsel = (lambda t: nt - 1 - t) if reverse else (lambda t: t)
    cur = pl.BlockSpec((bsz, tt, LRU_WIDTH), lambda t: (0, tsel(t), 0))
    in_specs = [cur] + [_const_spec(a) for a in (wgate, ba, bx, lam)]
    args = [xc, wgate, ba, bx, lam]
    if reverse:
        in_specs += [cur, cur]
        args += [hf, gate]
        out_dtype = jnp.bfloat16
    else:
        out_dtype = jnp.float32
    return pl.pallas_call(
        functools.partial(_lru_kernel, reverse=reverse, n_tiles=nt),
        grid=(nt,),
        in_specs=in_specs,
        out_specs=cur,
        out_shape=jax.ShapeDtypeStruct((bsz, tp, LRU_WIDTH), out_dtype),
        scratch_shapes=[pltpu.VMEM((chains, tt, LANES), jnp.float32),
                        pltpu.VMEM((chains, tt, LANES), jnp.float32),
                        pltpu.VMEM((chains, 1, LANES), jnp.float32)],
        compiler_params=pltpu.CompilerParams(
            dimension_semantics=("arbitrary",), vmem_limit_bytes=VMEM_LIMIT),
        name="lru_bwd" if reverse else "lru_fwd",
    )(*map(_operand, args))


def _gla_constants(reverse):
    c, w = CHUNK, SUBLANES
    t = np.arange(c)[:, None]
    s = np.arange(c)[None, :]
    gt, gs, pt, ps = t // w, s // w, t // 2, s // 2
    pair_code = 2 + (s % w) // 2
    if reverse:
        code = np.where((pt == ps) & (s > t), 1,
                        np.where((gt == gs) & (ps > pt), pair_code, np.where(gs > gt, w + gs, -1)))
        blocks = [s >= t, (t % 2 == 1) & (s == t - 1)]
    else:
        code = np.where((pt == ps) & (s <= t), t - s,
                        np.where((gt == gs) & (ps < pt), pair_code, np.where(gs < gt, w + gs, -1)))
        blocks = [s <= t, (t % 2 == 0) & (s == t + 1)]
    lmat = np.concatenate([blk.astype(np.float32) for blk in blocks], axis=0)
    lmat = np.concatenate([lmat, lmat], axis=1)
    code = np.tile(code.astype(np.int32), (1, GLA_HEADS))
    hmask = np.kron(np.eye(GLA_HEADS, dtype=np.float32), np.ones((c, GLA_DK), np.float32))
    return jnp.asarray(lmat, jnp.bfloat16), jnp.asarray(code), jnp.asarray(hmask, jnp.bfloat16)


def _gla_decays(g, lmat):
    ghi, glo = _split_bf16(g)
    return jnp.dot(lmat, jnp.concatenate([ghi, glo], axis=0), preferred_element_type=jnp.float32)


def _gla_scores(q, k, g, logdec, hmask, reverse):
    n_groups = CHUNK // SUBLANES
    pairs = SUBLANES // 2

    def head_stack(x):
        return jnp.concatenate([x.astype(jnp.bfloat16)] * GLA_HEADS, axis=0) * hmask

    def terms(lhs, rhs):
        return lax.dot_general(jnp.concatenate(lhs, axis=0).astype(jnp.bfloat16), head_stack(rhs),
                               (((1,), (1,)), ((), ())), preferred_element_type=jnp.float32)

    def facing_rows(first):
        return jnp.concatenate([jnp.broadcast_to(b[first + gi * SUBLANES:first + gi * SUBLANES + 1, :],
                                                 (SUBLANES, GLA_QK)) for gi in range(n_groups)], axis=0)

    b = logdec[0:CHUNK, :]

    step = q * jnp.exp2(g)
    p_pair = terms([step] if reverse else [q, step], k)

    k_pair = k * jnp.exp2(logdec[CHUNK:2 * CHUNK, :])
    pair_ids = range(1, pairs) if reverse else range(pairs - 1)
    lhs = [q * jnp.exp2(jnp.minimum(b - facing_rows(2 * p if reverse else 2 * p + 1), 0.0))
           for p in pair_ids]
    p_group = terms(lhs, k_pair)

    b_anchor = facing_rows(0 if reverse else SUBLANES - 1)
    groups = list(range(1, n_groups)) if reverse else list(range(n_groups - 1))
    row_sel = (lambda gi: slice(0, gi * SUBLANES)) if reverse else \
              (lambda gi: slice((gi + 1) * SUBLANES, CHUNK))
    anchor_row = (lambda gi: gi * SUBLANES) if reverse else (lambda gi: gi * SUBLANES + SUBLANES - 1)
    lhs = [q[row_sel(gi), :] * jnp.exp2(b[row_sel(gi), :] - b[anchor_row(gi):anchor_row(gi) + 1, :])
           for gi in groups]
    p_cross = terms(lhs, k * jnp.exp2(b_anchor - b))

    last = 0 if reverse else CHUNK - 1
    b_last = b[last:last + 1, :]
    qe = (q * jnp.exp2(b)).astype(jnp.bfloat16)
    ke = (k * jnp.exp2(b_last - b)).astype(jnp.bfloat16)
    return p_pair, p_group, p_cross, qe, ke, jnp.exp2(b_last)


def _gla_apply(p_pair, p_group, p_cross, qe, ke, e_last, vb, s_ref, code, reverse):
    n_groups = CHUNK // SUBLANES
    pairs = SUBLANES // 2
    groups = list(range(1, n_groups)) if reverse else list(range(n_groups - 1))
    scores = jnp.zeros((CHUNK, GLA_HEADS * CHUNK), jnp.float32)
    for i, d in enumerate([1] if reverse else [0, 1]):
        scores = jnp.where(code == d, p_pair[i * CHUNK:(i + 1) * CHUNK, :], scores)
    for i, p in enumerate(range(1, pairs) if reverse else range(pairs - 1)):
        scores = jnp.where(code == 2 + p, p_group[i * CHUNK:(i + 1) * CHUNK, :], scores)
    off = 0
    for gi in groups:
        rows = slice(0, gi * SUBLANES) if reverse else slice((gi + 1) * SUBLANES, CHUNK)
        n = rows.stop - rows.start
        part = jnp.where(code[rows, :] == SUBLANES + gi, p_cross[off:off + n, :], scores[rows, :])
        scores = jnp.concatenate([part, scores[n:, :]] if reverse else [scores[:CHUNK - n, :], part],
                                 axis=0)
        off += n
    sb = scores.astype(jnp.bfloat16)

    eye = (lax.broadcasted_iota(jnp.int32, (GLA_DK, GLA_DK), 0)
           == lax.broadcasted_iota(jnp.int32, (GLA_DK, GLA_DK), 1))
    zero_v = jnp.zeros((CHUNK, GLA_DV), jnp.bfloat16)
    zero_s = jnp.zeros((GLA_DK, GLA_DV), jnp.bfloat16)
    outs = []
    for pair in range(GLA_HEADS // 2):
        heads = (2 * pair, 2 * pair + 1)
        vh = [vb[:, hd * GLA_DV:(hd + 1) * GLA_DV] for hd in heads]
        st = [s_ref[hd] for hd in heads]
        sh = [s.astype(jnp.bfloat16) for s in st]
        lanes = slice(pair * 2 * CHUNK, (pair + 1) * 2 * CHUNK)
        lhs = jnp.concatenate([sb[:, lanes], qe[:, lanes]], axis=1)
        rhs = jnp.concatenate([jnp.concatenate([vh[0], zero_v], axis=1),
                               jnp.concatenate([zero_v, vh[1]], axis=1),
                               jnp.concatenate([sh[0], zero_s], axis=1),
                               jnp.concatenate([zero_s, sh[1]], axis=1)], axis=0)
        outs.append(jnp.dot(lhs, rhs, preferred_element_type=jnp.float32))
        for i, hd in enumerate(heads):
            ks = slice(hd * GLA_DK, (hd + 1) * GLA_DK)
            kv = lax.dot_general(ke[:, ks], vh[i], (((0,), (0,)), ((), ())),
                                 preferred_element_type=jnp.float32)
            e_col = jnp.sum(jnp.where(eye, e_last[:, ks], 0.0), axis=1, keepdims=True)
            s_ref[hd] = st[i] * e_col + kv
    return jnp.concatenate(outs, axis=1)


def _gla_kernel(*refs, reverse):
    if reverse:
        (qk_ref, v_ref, g_ref, lmat_ref, code_ref, hmask_ref, of_ref, go_ref, hn_ref,
         out_ref, s_ref) = refs
    else:
        qk_ref, v_ref, g_ref, lmat_ref, code_ref, hmask_ref, out_ref, s_ref = refs
    bsz, tt, _ = qk_ref.shape
    n_chunks = tt // CHUNK

    @pl.when(pl.program_id(0) == 0)
    def _():
        s_ref[...] = jnp.zeros_like(s_ref)

    lmat = lmat_ref[...]
    code = code_ref[...]
    hmask = hmask_ref[...]

    def chunk_body(j, carry):
        work = []
        for u in range(GLA_UNROLL):
            jj = j * GLA_UNROLL + u
            cidx = (n_chunks - 1 - jj) if reverse else jj
            rows = pl.ds(pl.multiple_of(cidx * CHUNK, CHUNK), CHUNK)
            work += [(bi, rows) for bi in range(bsz)]
        decs = [_gla_decays(g_ref[bi, rows, :], lmat) for bi, rows in work]
        terms = [_gla_scores(qk_ref[bi, rows, 0:GLA_QK], qk_ref[bi, rows, GLA_QK:2 * GLA_QK],
                             g_ref[bi, rows, :], dec, hmask, reverse)
                 for (bi, rows), dec in zip(work, decs)]
        for (bi, rows), term in zip(work, terms):
            o = _gla_apply(*term, v_ref[bi, rows, :].astype(jnp.bfloat16), s_ref.at[bi], code,
                           reverse)
            if reverse:
                o = o + of_ref[bi, rows, :]
                parts = []
                for hd in range(GLA_HEADS):
                    oh = o[:, hd * GLA_DV:(hd + 1) * GLA_DV]
                    parts.append(oh * _rms_scale(oh))
                y = jnp.concatenate(parts, axis=1) * hn_ref[...] * go_ref[bi, rows, :].astype(jnp.float32)
                out_ref[bi, rows, :] = y.astype(out_ref.dtype)
            else:
                out_ref[bi, rows, :] = o
        return carry

    lax.fori_loop(0, n_chunks // GLA_UNROLL, chunk_body, 0)


def _gla_pass(qk, v, gfb, *, reverse, o_f=None, go=None, head_norm=None):
    bsz, tp, _ = qk.shape
    tt = GLA_TIME_TILE
    nt = tp // tt
    tsel = (lambda t: nt - 1 - t) if reverse else (lambda t: t)
    blk = lambda w, c=0: pl.BlockSpec((bsz, tt, w), lambda t: (0, tsel(t), c))
    consts = _gla_constants(reverse)
    in_specs = [blk(512), blk(512), blk(GLA_QK, 1 if reverse else 0)]
    in_specs += [_const_spec(a) for a in consts]
    args = [qk, v, gfb, *consts]
    if reverse:
        in_specs += [blk(512), blk(512), _const_spec(head_norm)]
        args += [o_f, go, head_norm]
        out_dtype = jnp.bfloat16
    else:
        out_dtype = jnp.float32
    return pl.pallas_call(
        functools.partial(_gla_kernel, reverse=reverse),
        grid=(nt,),
        in_specs=in_specs,
        out_specs=blk(512),
        out_shape=jax.ShapeDtypeStruct((bsz, tp, GLA_WIDTH), out_dtype),
        scratch_shapes=[pltpu.VMEM((bsz, GLA_HEADS, GLA_DK, GLA_DV), jnp.float32)],
        compiler_params=pltpu.CompilerParams(
            dimension_semantics=("arbitrary",), vmem_limit_bytes=VMEM_LIMIT),
        name="gla_bwd" if reverse else "gla_fwd",
    )(*map(_operand, args))


def _out_mlp_kernel(h_ref, yl_ref, yg_ref, wol_ref, wog_ref, gpost_ref, gpre_ref, gpost2_ref,
                    wup_ref, wdn_ref, out_ref, acc_ref, *, rows_per_batch, n_batch, per_batch_grid):
    tm = h_ref.shape[0]
    mix = (jnp.dot(yl_ref[...], wol_ref[...], preferred_element_type=jnp.float32)
           + jnp.dot(yg_ref[...], wog_ref[...], preferred_element_type=jnp.float32))
    tile = pl.program_id(0)
    if per_batch_grid:
        tile = tile * pl.num_programs(1) + pl.program_id(1)
    row = lax.broadcasted_iota(jnp.int32, (tm, 1), 0) + tile * tm
    real = row >= PADF
    for b in range(1, n_batch):
        real = real & ((row < b * rows_per_batch) | (row >= b * rows_per_batch + PADF))
    h1 = h_ref[...] + jnp.where(real, mix * _rms_scale(mix) * gpost_ref[...], 0.0)
    xn = (h1 * _rms_scale(h1) * gpre_ref[...]).astype(jnp.bfloat16)
    for c in range(D_FF // FF_TILE):
        cs = slice(c * FF_TILE, (c + 1) * FF_TILE)
        up = jnp.dot(xn, wup_ref[:, cs], preferred_element_type=jnp.float32)
        act = jnp.square(jnp.maximum(up, 0.0)).astype(jnp.bfloat16)
        part = jnp.dot(act, wdn_ref[cs, :], preferred_element_type=jnp.float32)
        if c == 0:
            acc_ref[...] = part
        else:
            acc_ref[...] += part
    ff = acc_ref[...]
    out_ref[...] = h1 + ff * _rms_scale(ff) * gpost2_ref[...]


def _out_mlp(h, y_lru, y_gla, wo_l, wo_g, g_post, g_pre, g_post2, w_up, w_dn, rows_per_batch,
             drop_lead):
    rows = h.shape[0]
    n_batch = rows // rows_per_batch
    consts = (wo_l, wo_g, g_post, g_pre, g_post2, w_up, w_dn)
    if drop_lead:
        tm = T0
        per_batch = rows_per_batch // tm
        grid = (n_batch, per_batch)
        row = lambda w: pl.BlockSpec((tm, w), lambda b, j: (b * per_batch + j, 0))
        out_spec = pl.BlockSpec(
            (tm, D_MODEL), lambda b, j: (b * (per_batch - 1) + jnp.maximum(j - 1, 0), 0))
        out_rows = rows - n_batch * tm
    else:
        tm = ROW_TILE
        grid = (rows // tm,)
        row = lambda w: pl.BlockSpec((tm, w), lambda i: (i, 0))
        out_spec = row(D_MODEL)
        out_rows = rows
    return pl.pallas_call(
        functools.partial(_out_mlp_kernel, rows_per_batch=rows_per_batch, n_batch=n_batch,
                          per_batch_grid=drop_lead),
        grid=grid,
        in_specs=[row(D_MODEL), row(512), row(512)] + [_const_spec(a) for a in consts],
        out_specs=out_spec,
        out_shape=jax.ShapeDtypeStruct((out_rows, D_MODEL), jnp.float32),
        scratch_shapes=[pltpu.VMEM((tm, D_MODEL), jnp.float32)],
        compiler_params=pltpu.CompilerParams(
            dimension_semantics=("arbitrary",) * len(grid), vmem_limit_bytes=VMEM_LIMIT),
        name="out_mlp",
    )(h, y_lru, y_gla, *map(_operand, consts))


def _block_diag_gates(wa, wx):
    per_half = LRU_HEADS // 2
    eye = jnp.eye(per_half, dtype=wa.dtype)

    def bd(w):
        w5 = w.reshape(-1, 2, per_half, LRU_HEAD_DIM, LRU_HEAD_DIM)
        return jnp.einsum('nhjil,jk->nhjikl', w5, eye).reshape(-1, 2, LRU_WIDTH // 2, LRU_WIDTH // 2)
    return jnp.concatenate([bd(wa), bd(wx)], axis=3).astype(jnp.bfloat16)


def kernel(x, meta_tokens, norm_mix_pre, norm_mix_post, norm_mlp_pre, norm_mlp_post,
           w_in, conv_w, conv_b,
           lru_wa_f, lru_ba_f, lru_wx_f, lru_bx_f, lru_lambda_f,
           lru_wa_b, lru_ba_b, lru_wx_b, lru_bx_b, lru_lambda_b,
           gla_wg_f, gla_bg_f, gla_wg_b, gla_bg_b, gla_head_norm,
           w_out, w_mlp_up, w_mlp_down):
    bsz, seq, d = x.shape
    depth = w_in.shape[0]
    tp = seq + T0
    assert d == D_MODEL and tp % TIME_TILE == 0 and tp % GLA_TIME_TILE == 0
    assert (bsz * tp) % ROW_TILE == 0
    meta = jnp.broadcast_to(meta_tokens.astype(x.dtype)[None], (bsz, N_META, d))
    h = jnp.concatenate([jnp.zeros((bsz, PADF, d), x.dtype), meta, x], axis=1)
    h = h.reshape(bsz * tp, d)
    vec = lambda a: a.reshape(depth, 1, -1)
    bf = lambda a: a.astype(jnp.bfloat16)

    w_main = bf(w_in[:, :, :2560])
    w_zg = bf(jnp.tile(w_in[:, :, 2560:2592], (1, 1, 3)))
    wg = jnp.zeros((depth, 2 * GLA_RANK, 2 * GLA_QK), jnp.float32)
    wg = wg.at[:, :GLA_RANK, :GLA_QK].set(gla_wg_f).at[:, GLA_RANK:, GLA_QK:].set(gla_wg_b)
    wg_hi, wg_lo = _split_bf16(wg)
    wg3 = jnp.concatenate([wg_hi, wg_lo, wg_hi], axis=1)
    bg = vec(jnp.concatenate([gla_bg_f, gla_bg_b], axis=1))
    gates_f = _block_diag_gates(lru_wa_f, lru_wx_f)
    gates_b = _block_diag_gates(lru_wa_b, lru_wx_b)
    params = dict(
        g_pre=vec(norm_mix_pre), w_main=w_main, w_zg=w_zg, wg3=wg3, bg=bg,
        conv_w=conv_w, conv_b=vec(conv_b),
        lru_f=(gates_f, vec(lru_ba_f), vec(lru_bx_f), vec(lru_lambda_f)),
        lru_b=(gates_b, vec(lru_ba_b), vec(lru_bx_b), vec(lru_lambda_b)),
        head_norm=vec(gla_head_norm),
        out=(bf(w_out[:, :LRU_WIDTH]), bf(w_out[:, LRU_WIDTH:]), vec(norm_mix_post),
             vec(norm_mlp_pre), vec(norm_mlp_post), bf(w_mlp_up), bf(w_mlp_down)))

    for l in range(depth):
        at = lambda a: _Layer(a, l)
        p = jax.tree.map(at, params)
        outs = _in_proj(h, p['g_pre'], p['w_main'], p['w_zg'], p['wg3'], p['bg'],
                        p['conv_w'], p['conv_b'])
        xc3, gate3, qk3, v3, go3, gfb3 = (a.reshape(bsz, tp, a.shape[-1]) for a in outs)

        h_f = _lru_pass(xc3, *p['lru_f'], reverse=False)
        y_lru = _lru_pass(xc3, *p['lru_b'], reverse=True, hf=h_f, gate=gate3)
        o_f = _gla_pass(qk3, v3, gfb3, reverse=False)
        y_gla = _gla_pass(qk3, v3, gfb3, reverse=True, o_f=o_f, go=go3, head_norm=p['head_norm'])

        h = _out_mlp(h, y_lru.reshape(bsz * tp, -1), y_gla.reshape(bsz * tp, -1), *p['out'],
                     tp, drop_lead=(l == depth - 1))
    return h.reshape(bsz, seq, d)
```

```python
import functools

import numpy as np
import jax
import jax.numpy as jnp
from jax import lax
from jax.experimental import pallas as pl
from jax.experimental.pallas import tpu as pltpu

D_MODEL = 1024
N_META = 16
CHUNK = 64
LRU_WIDTH = 512
LRU_HEADS = 8
LRU_HEAD_DIM = 64
LRU_C = 8.0
GLA_WIDTH = 512
GLA_HEADS = 4
GLA_DV = 128
GLA_DK = 64
GLA_QK = GLA_HEADS * GLA_DK
GLA_RANK = 16
GLA_GATE_NORM = 16.0
D_FF = 4096
EPS = 1e-6
LOG2_E = 1.4426950408889634
F32_TINY = float(np.finfo(np.float32).tiny)

LEAD_CHUNKS = 4
T0 = LEAD_CHUNKS * CHUNK
PADF = T0 - N_META
ROW_TILE = 768
TIME_TILE = 1056
SCAN_UNROLL = 12
GLA_TIME_TILE = 768
GLA_UNROLL = 3
IN_SPLIT = 3
ROW_SPLIT = 2
FF_TILE = 1024
SUBLANES = 8
LANES = 128
VMEM_LIMIT = 56 * 1024 * 1024


class _Layer:
    def __init__(self, stacked, layer):
        self.stacked, self.layer, self.shape = stacked, layer, tuple(stacked.shape[1:])


def _operand(a):
    return a.stacked if isinstance(a, _Layer) else a


def _const_spec(a):
    nd = len(a.shape)
    if isinstance(a, _Layer):
        layer = a.layer
        return pl.BlockSpec((None,) + a.shape, lambda *_: (layer,) + (0,) * nd,
                            pipeline_mode=pl.Buffered(1))
    return pl.BlockSpec(a.shape, lambda *_: (0,) * nd, pipeline_mode=pl.Buffered(1))


def _rms_scale(x):
    return lax.rsqrt(jnp.mean(x * x, axis=-1, keepdims=True) + EPS)


def _sigmoid(x):
    return 0.5 * jnp.tanh(0.5 * x) + 0.5


def _softplus(x):
    return jnp.maximum(x, 0.0) + jnp.log1p(jnp.exp(-jnp.abs(x)))


def _split_bf16(x):
    hi = x.astype(jnp.bfloat16)
    lo = (x - hi.astype(jnp.float32)).astype(jnp.bfloat16)
    return hi, lo


def _token_tile(x_ref, lead_ref):
    tm = x_ref.shape[0]
    fetched = x_ref[...]
    first = jnp.concatenate([lead_ref[...], fetched[0:tm - T0, :]], axis=0)
    return jnp.where(pl.program_id(1) == 0, first, fetched)


def _in_proj_kernel(*refs, n_tiles, from_tokens):
    if from_tokens:
        h_ref, hp_ref, hn_ref, lead_ref, *refs = refs
    else:
        h_ref, hp_ref, hn_ref, *refs = refs
    (g_ref, w_ref, wzg_ref, wg_ref, bg_ref, cw_ref, cb_ref,
     xc_ref, gate_ref, qk_ref, v_ref, go_ref, gfb_ref, xs_ref) = refs
    tm = h_ref.shape[0]
    normed = lambda x: (x * _rms_scale(x) * g_ref[...]).astype(jnp.bfloat16)
    if from_tokens:
        h_tile = _token_tile(h_ref, lead_ref)
        no_next = pl.program_id(1) == n_tiles - 1
    else:
        h_tile = h_ref[...]
        no_next = pl.program_id(0) == n_tiles - 1
    halo = jnp.concatenate([hp_ref[...], hn_ref[...]], axis=0)
    xn = normed(h_tile)
    xn_ext = jnp.concatenate([xn, normed(halo)], axis=0)

    zg = jnp.dot(xn, wzg_ref[...], preferred_element_type=jnp.float32)
    zx = jnp.dot(xn_ext, w_ref[:, 0:LRU_WIDTH], preferred_element_type=jnp.float32)
    zhi = zg.astype(jnp.bfloat16)
    zlo = (zg - zhi.astype(jnp.float32)).astype(jnp.bfloat16)
    lane = lax.broadcasted_iota(jnp.int32, zg.shape, 1)
    lhs = jnp.where(lane < 4 * GLA_RANK, zhi, zlo)
    pre = jnp.dot(lhs, wg_ref[...], preferred_element_type=jnp.float32) + bg_ref[...]
    n = tm // IN_SPLIT
    parts = [slice(i * n, (i + 1) * n) for i in range(IN_SPLIT)]
    zs = [jnp.dot(xn[p, :], w_ref[:, LRU_WIDTH:], preferred_element_type=jnp.float32)
          for p in parts]

    xs_ref[0:SUBLANES, :] = zx[tm:tm + SUBLANES, :]
    xs_ref[SUBLANES:SUBLANES + tm, :] = zx[0:tm, :]
    xs_ref[SUBLANES + tm:2 * SUBLANES + tm, :] = jnp.where(
        no_next, 0.0, zx[tm + SUBLANES:tm + 2 * SUBLANES, :])
    cw = cw_ref[...]
    xc_ref[...] = (cw[0:1, :] * xs_ref[SUBLANES - 2:SUBLANES - 2 + tm, :]
                   + cw[1:2, :] * xs_ref[SUBLANES - 1:SUBLANES - 1 + tm, :]
                   + cw[2:3, :] * xs_ref[SUBLANES:SUBLANES + tm, :]
                   + cw[3:4, :] * xs_ref[SUBLANES + 1:SUBLANES + 1 + tm, :]
                   + cb_ref[...])
    logsig = jnp.minimum(pre, 0.0) - jnp.log1p(jnp.exp(-jnp.abs(pre)))
    gfb_ref[...] = logsig * (LOG2_E / GLA_GATE_NORM)

    for p, z in zip(parts, zs):
        gt = z[:, 0:512]
        gelu = 0.5 * gt * (1.0 + jnp.tanh(0.7978845608028654 * (gt + 0.044715 * gt * gt * gt)))
        gate_ref[p, :] = gelu.astype(jnp.bfloat16)
        q = z[:, 512:768] * (GLA_DK ** -0.5)
        qk_ref[p, :] = jnp.concatenate([q, z[:, 768:1024]], axis=1).astype(jnp.bfloat16)
        v_ref[p, :] = z[:, 1024:1536].astype(jnp.bfloat16)
        go = z[:, 1536:2048]
        go_ref[p, :] = (go * _sigmoid(go)).astype(jnp.bfloat16)


def _token_specs(n_batch, seq, tm):
    per_batch = (seq + T0) // tm
    total = n_batch * seq
    start = lambda b, j: b * seq + j * tm - T0
    window = lambda rows, at: pl.BlockSpec((pl.Element(rows), pl.Element(D_MODEL)),
                                           lambda b, j: (pl.multiple_of(
                                               jnp.clip(at(b, j), 0, total - rows), SUBLANES), 0))
    specs = [window(tm, lambda b, j: b * seq + jnp.maximum(j * tm - T0, 0)),
             window(SUBLANES, lambda b, j: start(b, j) - SUBLANES),
             window(SUBLANES, lambda b, j: start(b, j) + tm)]
    return (n_batch, per_batch), specs, (lambda w: pl.BlockSpec((tm, w), lambda b, j: (b * per_batch + j, 0)))


def _in_proj(h, g, w_main, w_zg, wg3, bg, conv_w, conv_b, lead=None, n_batch=None):
    tm = ROW_TILE
    consts = (g, w_main, w_zg, wg3, bg, conv_w, conv_b)
    if lead is None:
        rows = h.shape[0]
        grid = (rows // tm,)
        g8 = tm // SUBLANES
        row = lambda w: pl.BlockSpec((tm, w), lambda i: (i, 0))
        h_specs = [row(D_MODEL),
                   pl.BlockSpec((SUBLANES, D_MODEL), lambda i: (jnp.maximum(i * g8 - 1, 0), 0)),
                   pl.BlockSpec((SUBLANES, D_MODEL),
                                lambda i: (jnp.minimum((i + 1) * g8, rows // SUBLANES - 1), 0))]
        h_args = [h, h, h]
    else:
        seq = h.shape[0] // n_batch
        rows = n_batch * (seq + T0)
        grid, h_specs, row = _token_specs(n_batch, seq, tm)
        h_specs.append(_const_spec(lead))
        h_args = [h, h, h, lead]
    outs = [(512, jnp.float32)] + [(512, jnp.bfloat16)] * 4 + [(512, jnp.float32)]
    return pl.pallas_call(
        functools.partial(_in_proj_kernel, n_tiles=grid[-1], from_tokens=lead is not None),
        grid=grid,
        in_specs=h_specs + [_const_spec(a) for a in consts],
        out_specs=[row(w) for w, _ in outs],
        out_shape=[jax.ShapeDtypeStruct((rows, w), dt) for w, dt in outs],
        scratch_shapes=[pltpu.VMEM((tm + 2 * SUBLANES, LRU_WIDTH), jnp.float32)],
        compiler_params=pltpu.CompilerParams(
            dimension_semantics=("arbitrary",) * len(grid), vmem_limit_bytes=VMEM_LIMIT),
        name="in_proj",
    )(*h_args, *map(_operand, consts))


def _lru_kernel(*refs, reverse, n_tiles):
    if reverse:
        (xc_ref, wg_ref, ba_ref, bx_ref, lam_ref, hf_ref, gate_ref,
         out_ref, a_scr, u_scr, carry_ref) = refs
    else:
        xc_ref, wg_ref, ba_ref, bx_ref, lam_ref, out_ref, a_scr, u_scr, carry_ref = refs
    bsz, tt, _ = xc_ref.shape
    step = pl.program_id(0)
    tile = (n_tiles - 1 - step) if reverse else step

    @pl.when(step == 0)
    def _():
        carry_ref[...] = jnp.zeros_like(carry_ref)

    n_slabs = LRU_WIDTH // LANES
    half = LRU_WIDTH // 2
    decay_rate = (-LRU_C) * _softplus(-lam_ref[...])
    for bi in range(bsz):
        xc = xc_ref[bi]
        xcb = xc.astype(jnp.bfloat16)
        p0 = jnp.dot(xcb[:, :half], wg_ref[0], preferred_element_type=jnp.float32)
        p1 = jnp.dot(xcb[:, half:], wg_ref[1], preferred_element_type=jnp.float32)
        r_pre = jnp.concatenate([p0[:, :half], p1[:, :half]], axis=1)
        i_pre = jnp.concatenate([p0[:, half:], p1[:, half:]], axis=1)
        r = _sigmoid(r_pre + ba_ref[...])
        gi = _sigmoid(i_pre + bx_ref[...])
        log_a = r * decay_rate
        a = jnp.exp(log_a)
        w = -jnp.tanh(log_a) * (1.0 + a * a)
        u = (w * lax.rsqrt(jnp.maximum(w, F32_TINY))) * (gi * xc)
        if not reverse:
            row = lax.broadcasted_iota(jnp.int32, xc.shape, 0)
            u = jnp.where(row + tile * tt >= PADF, u, 0.0)
        for k in range(n_slabs):
            a_scr[bi * n_slabs + k] = a[:, k * LANES:(k + 1) * LANES]
            u_scr[bi * n_slabs + k] = u[:, k * LANES:(k + 1) * LANES]

    run = tt // SUBLANES
    chains = bsz * n_slabs
    steps = lambda j: pl.ds(j, SUBLANES, stride=run)

    assert run % SCAN_UNROLL == 0
    n_blocks = run // SCAN_UNROLL
    block_base = lambda i: ((n_blocks - 1 - i) if reverse else i) * SCAN_UNROLL
    offsets = range(SCAN_UNROLL - 1, -1, -1) if reverse else range(SCAN_UNROLL)

    def run_ends(i, carry):
        state, prod = list(carry[0]), list(carry[1])
        base = block_base(i)
        for o in offsets:
            for k in range(chains):
                aj = a_scr[k, steps(base + o), :]
                state[k] = aj * state[k] + u_scr[k, steps(base + o), :]
                prod[k] = prod[k] * aj
        return tuple(state), tuple(prod)

    zeros = tuple(jnp.zeros((SUBLANES, LANES), jnp.float32) for _ in range(chains))
    ones = tuple(z + 1.0 for z in zeros)
    end_state, end_prod = lax.fori_loop(0, n_blocks, run_ends, (zeros, ones))
    end_state, end_prod = jnp.stack(end_state), jnp.stack(end_prod)

    c = carry_ref[...]
    order = range(SUBLANES - 1, -1, -1) if reverse else range(SUBLANES)
    entering = [None] * SUBLANES
    for r in order:
        entering[r] = c
        c = end_prod[:, r:r + 1, :] * c + end_state[:, r:r + 1, :]
    carry_ref[...] = c
    entering = jnp.concatenate(entering, axis=1)

    def scan(i, state):
        state = list(state)
        base = block_base(i)
        for o in offsets:
            for k in range(chains):
                state[k] = a_scr[k, steps(base + o), :] * state[k] + u_scr[k, steps(base + o), :]
                u_scr[k, steps(base + o), :] = state[k]
        return tuple(state)

    lax.fori_loop(0, n_blocks, scan, tuple(entering[k] for k in range(chains)))

    for bi in range(bsz):
        hs = jnp.concatenate([u_scr[bi * n_slabs + k] for k in range(n_slabs)], axis=1)
        if reverse:
            gate = gate_ref[bi].astype(jnp.float32)
            out_ref[bi] = ((hf_ref[bi] + hs) * gate).astype(out_ref.dtype)
        else:
            out_ref[bi] = hs


def _lru_pass(xc, wgate, ba, bx, lam, *, reverse, hf=None, gate=None):
    bsz, tp, _ = xc.shape
    tt = TIME_TILE
    nt = tp // tt
    chains = bsz * (LRU_WIDTH // LANES)
    tsel = (lambda t: nt - 1 - t) if reverse else (lambda t: t)
    cur = pl.BlockSpec((bsz, tt, LRU_WIDTH), lambda t: (0, tsel(t), 0))
    in_specs = [cur] + [_const_spec(a) for a in (wgate, ba, bx, lam)]
    args = [xc, wgate, ba, bx, lam]
    if reverse:
        in_specs += [cur, cur]
        args += [hf, gate]
        out_dtype = jnp.bfloat16
    else:
        out_dtype = jnp.float32
    return pl.pallas_call(
        functools.partial(_lru_kernel, reverse=reverse, n_tiles=nt),
        grid=(nt,),
        in_specs=in_specs,
        out_specs=cur,
        out_shape=jax.ShapeDtypeStruct((bsz, tp, LRU_WIDTH), out_dtype),
        scratch_shapes=[pltpu.VMEM((chains, tt, LANES), jnp.float32),
                        pltpu.VMEM((chains, tt, LANES), jnp.float32),
                        pltpu.VMEM((chains, 1, LANES), jnp.float32)],
        compiler_params=pltpu.CompilerParams(
            dimension_semantics=("arbitrary",), vmem_limit_bytes=VMEM_LIMIT),
        name="lru_bwd" if reverse else "lru_fwd",
    )(*map(_operand, args))


def _gla_constants(reverse):
    c, w = CHUNK, SUBLANES
    t = np.arange(c)[:, None]
    s = np.arange(c)[None, :]
    gt, gs, pt, ps = t // w, s // w, t // 2, s // 2
    pair_code = 2 + (s % w) // 2
    if reverse:
        code = np.where((pt == ps) & (s > t), 1,
                        np.where((gt == gs) & (ps > pt), pair_code, np.where(gs > gt, w + gs, -1)))
        blocks = [s >= t, (t % 2 == 1) & (s == t - 1)]
    else:
        code = np.where((pt == ps) & (s <= t), t - s,
                        np.where((gt == gs) & (ps < pt), pair_code, np.where(gs < gt, w + gs, -1)))
        blocks = [s <= t, (t % 2 == 0) & (s == t + 1)]
    lmat = np.concatenate([blk.astype(np.float32) for blk in blocks], axis=0)
    lmat = np.concatenate([lmat, lmat], axis=1)
    code = np.tile(code.astype(np.int32), (1, GLA_HEADS))
    hmask = np.kron(np.eye(GLA_HEADS, dtype=np.float32), np.ones((c, GLA_DK), np.float32))
    return jnp.asarray(lmat, jnp.bfloat16), jnp.asarray(code), jnp.asarray(hmask, jnp.bfloat16)


def _gla_decays(g, lmat):
    ghi, glo = _split_bf16(g)
    return jnp.dot(lmat, jnp.concatenate([ghi, glo], axis=0), preferred_element_type=jnp.float32)


def _gla_scores(q, k, g, logdec, hmask, reverse):
    n_groups = CHUNK // SUBLANES
    pairs = SUBLANES // 2

    def head_stack(x):
        return jnp.concatenate([x.astype(jnp.bfloat16)] * GLA_HEADS, axis=0) * hmask

    def terms(lhs, rhs):
        return lax.dot_general(jnp.concatenate(lhs, axis=0).astype(jnp.bfloat16), head_stack(rhs),
                               (((1,), (1,)), ((), ())), preferred_element_type=jnp.float32)

    def facing_rows(first):
        return jnp.concatenate([jnp.broadcast_to(b[first + gi * SUBLANES:first + gi * SUBLANES + 1, :],
                                                 (SUBLANES, GLA_QK)) for gi in range(n_groups)], axis=0)

    b = logdec[0:CHUNK, :]

    step = q * jnp.exp2(g)
    p_pair = terms([step] if reverse else [q, step], k)

    k_pair = k * jnp.exp2(logdec[CHUNK:2 * CHUNK, :])
    pair_ids = range(1, pairs) if reverse else range(pairs - 1)
    lhs = [q * jnp.exp2(jnp.minimum(b - facing_rows(2 * p if reverse else 2 * p + 1), 0.0))
           for p in pair_ids]
    p_group = terms(lhs, k_pair)

    b_anchor = facing_rows(0 if reverse else SUBLANES - 1)
    groups = list(range(1, n_groups)) if reverse else list(range(n_groups - 1))
    row_sel = (lambda gi: slice(0, gi * SUBLANES)) if reverse else \
              (lambda gi: slice((gi + 1) * SUBLANES, CHUNK))
    anchor_row = (lambda gi: gi * SUBLANES) if reverse else (lambda gi: gi * SUBLANES + SUBLANES - 1)
    lhs = [q[row_sel(gi), :] * jnp.exp2(b[row_sel(gi), :] - b[anchor_row(gi):anchor_row(gi) + 1, :])
           for gi in groups]
    p_cross = terms(lhs, k * jnp.exp2(b_anchor - b))

    last = 0 if reverse else CHUNK - 1
    b_last = b[last:last + 1, :]
    qe = (q * jnp.exp2(b)).astype(jnp.bfloat16)
    ke = (k * jnp.exp2(b_last - b)).astype(jnp.bfloat16)
    return p_pair, p_group, p_cross, qe, ke, jnp.exp2(b_last)


def _gla_apply(p_pair, p_group, p_cross, qe, ke, e_last, vb, s_ref, code, reverse):
    n_groups = CHUNK // SUBLANES
    pairs = SUBLANES // 2
    groups = list(range(1, n_groups)) if reverse else list(range(n_groups - 1))
    scores = jnp.zeros((CHUNK, GLA_HEADS * CHUNK), jnp.float32)
    for i, d in enumerate([1] if reverse else [0, 1]):
        scores = jnp.where(code == d, p_pair[i * CHUNK:(i + 1) * CHUNK, :], scores)
    for i, p in enumerate(range(1, pairs) if reverse else range(pairs - 1)):
        scores = jnp.where(code == 2 + p, p_group[i * CHUNK:(i + 1) * CHUNK, :], scores)
    off = 0
    for gi in groups:
        rows = slice(0, gi * SUBLANES) if reverse else slice((gi + 1) * SUBLANES, CHUNK)
        n = rows.stop - rows.start
        part = jnp.where(code[rows, :] == SUBLANES + gi, p_cross[off:off + n, :], scores[rows, :])
        scores = jnp.concatenate([part, scores[n:, :]] if reverse else [scores[:CHUNK - n, :], part],
                                 axis=0)
        off += n
    sb = scores.astype(jnp.bfloat16)

    eye = (lax.broadcasted_iota(jnp.int32, (GLA_DK, GLA_DK), 0)
           == lax.broadcasted_iota(jnp.int32, (GLA_DK, GLA_DK), 1))
    zero_v = jnp.zeros((CHUNK, GLA_DV), jnp.bfloat16)
    zero_s = jnp.zeros((GLA_DK, GLA_DV), jnp.bfloat16)
    outs = []
    for pair in range(GLA_HEADS // 2):
        heads = (2 * pair, 2 * pair + 1)
        vh = [vb[:, hd * GLA_DV:(hd + 1) * GLA_DV] for hd in heads]
        st = [s_ref[hd] for hd in heads]
        sh = [s.astype(jnp.bfloat16) for s in st]
        lanes = slice(pair * 2 * CHUNK, (pair + 1) * 2 * CHUNK)
        lhs = jnp.concatenate([sb[:, lanes], qe[:, lanes]], axis=1)
        rhs = jnp.concatenate([jnp.concatenate([vh[0], zero_v], axis=1),
                               jnp.concatenate([zero_v, vh[1]], axis=1),
                               jnp.concatenate([sh[0], zero_s], axis=1),
                               jnp.concatenate([zero_s, sh[1]], axis=1)], axis=0)
        outs.append(jnp.dot(lhs, rhs, preferred_element_type=jnp.float32))
        for i, hd in enumerate(heads):
            ks = slice(hd * GLA_DK, (hd + 1) * GLA_DK)
            kv = lax.dot_general(ke[:, ks], vh[i], (((0,), (0,)), ((), ())),
                                 preferred_element_type=jnp.float32)
            e_col = jnp.sum(jnp.where(eye, e_last[:, ks], 0.0), axis=1, keepdims=True)
            s_ref[hd] = st[i] * e_col + kv
    return jnp.concatenate(outs, axis=1)


def _gla_kernel(*refs, reverse):
    if reverse:
        (qk_ref, v_ref, g_ref, lmat_ref, code_ref, hmask_ref, of_ref, go_ref, hn_ref,
         out_ref, s_ref) = refs
    else:
        qk_ref, v_ref, g_ref, lmat_ref, code_ref, hmask_ref, out_ref, s_ref = refs
    bsz, tt, _ = qk_ref.shape
    n_chunks = tt // CHUNK

    @pl.when(pl.program_id(0) == 0)
    def _():
        s_ref[...] = jnp.zeros_like(s_ref)

    lmat = lmat_ref[...]
    code = code_ref[...]
    hmask = hmask_ref[...]

    def chunk_body(j, carry):
        work = []
        for u in range(GLA_UNROLL):
            jj = j * GLA_UNROLL + u
            cidx = (n_chunks - 1 - jj) if reverse else jj
            rows = pl.ds(pl.multiple_of(cidx * CHUNK, CHUNK), CHUNK)
            work += [(bi, rows) for bi in range(bsz)]
        decs = [_gla_decays(g_ref[bi, rows, :], lmat) for bi, rows in work]
        terms = [_gla_scores(qk_ref[bi, rows, 0:GLA_QK], qk_ref[bi, rows, GLA_QK:2 * GLA_QK],
                             g_ref[bi, rows, :], dec, hmask, reverse)
                 for (bi, rows), dec in zip(work, decs)]
        for (bi, rows), term in zip(work, terms):
            o = _gla_apply(*term, v_ref[bi, rows, :].astype(jnp.bfloat16), s_ref.at[bi], code,
                           reverse)
            if reverse:
                o = o + of_ref[bi, rows, :]
                parts = []
                for hd in range(GLA_HEADS):
                    oh = o[:, hd * GLA_DV:(hd + 1) * GLA_DV]
                    parts.append(oh * _rms_scale(oh))
                y = jnp.concatenate(parts, axis=1) * hn_ref[...] * go_ref[bi, rows, :].astype(jnp.float32)
                out_ref[bi, rows, :] = y.astype(out_ref.dtype)
            else:
                out_ref[bi, rows, :] = o
        return carry

    lax.fori_loop(0, n_chunks // GLA_UNROLL, chunk_body, 0)


def _gla_pass(qk, v, gfb, *, reverse, o_f=None, go=None, head_norm=None):
    bsz, tp, _ = qk.shape
    tt = GLA_TIME_TILE
    nt = tp // tt
    tsel = (lambda t: nt - 1 - t) if reverse else (lambda t: t)
    blk = lambda w, c=0: pl.BlockSpec((bsz, tt, w), lambda t: (0, tsel(t), c))
    consts = _gla_constants(reverse)
    in_specs = [blk(512), blk(512), blk(GLA_QK, 1 if reverse else 0)]
    in_specs += [_const_spec(a) for a in consts]
    args = [qk, v, gfb, *consts]
    if reverse:
        in_specs += [blk(512), blk(512), _const_spec(head_norm)]
        args += [o_f, go, head_norm]
        out_dtype = jnp.bfloat16
    else:
        out_dtype = jnp.float32
    return pl.pallas_call(
        functools.partial(_gla_kernel, reverse=reverse),
        grid=(nt,),
        in_specs=in_specs,
        out_specs=blk(512),
        out_shape=jax.ShapeDtypeStruct((bsz, tp, GLA_WIDTH), out_dtype),
        scratch_shapes=[pltpu.VMEM((bsz, GLA_HEADS, GLA_DK, GLA_DV), jnp.float32)],
        compiler_params=pltpu.CompilerParams(
            dimension_semantics=("arbitrary",), vmem_limit_bytes=VMEM_LIMIT),
        name="gla_bwd" if reverse else "gla_fwd",
    )(*map(_operand, args))


def _out_mlp_kernel(*refs, rows_per_batch, n_batch, per_batch_grid, from_tokens):
    if from_tokens:
        x_ref, lead_ref, *refs = refs
        h_tile = _token_tile(x_ref, lead_ref)
    else:
        h_tile, *refs = refs
    (yl_ref, yg_ref, wol_ref, wog_ref, gpost_ref, gpre_ref, gpost2_ref,
     wup_ref, wdn_ref, out_ref, acc_ref) = refs
    tm = out_ref.shape[0]
    tile = pl.program_id(0)
    if per_batch_grid:
        tile = tile * pl.num_programs(1) + pl.program_id(1)
    n = tm // ROW_SPLIT
    parts = [slice(i * n, (i + 1) * n) for i in range(ROW_SPLIT)]
    mix = [jnp.dot(yl_ref[p, :], wol_ref[...], preferred_element_type=jnp.float32)
           + jnp.dot(yg_ref[p, :], wog_ref[...], preferred_element_type=jnp.float32) for p in parts]
    h1, xn = [], []
    for i, p in enumerate(parts):
        row = lax.broadcasted_iota(jnp.int32, (n, 1), 0) + (tile * tm + i * n)
        real = row >= PADF
        for b in range(1, n_batch):
            real = real & ((row < b * rows_per_batch) | (row >= b * rows_per_batch + PADF))
        h1.append(h_tile[p, :] + jnp.where(real, mix[i] * _rms_scale(mix[i]) * gpost_ref[...], 0.0))
        xn.append((h1[i] * _rms_scale(h1[i]) * gpre_ref[...]).astype(jnp.bfloat16))
    for c in range(D_FF // FF_TILE):
        cs = slice(c * FF_TILE, (c + 1) * FF_TILE)
        for i, p in enumerate(parts):
            up = jnp.dot(xn[i], wup_ref[:, cs], preferred_element_type=jnp.float32)
            act = jnp.square(jnp.maximum(up, 0.0)).astype(jnp.bfloat16)
            part = jnp.dot(act, wdn_ref[cs, :], preferred_element_type=jnp.float32)
            if c == 0:
                acc_ref[p, :] = part
            else:
                acc_ref[p, :] += part
    for i, p in enumerate(parts):
        ff = acc_ref[p, :]
        out_ref[p, :] = h1[i] + ff * _rms_scale(ff) * gpost2_ref[...]


def _out_mlp(h, y_lru, y_gla, wo_l, wo_g, g_post, g_pre, g_post2, w_up, w_dn, rows_per_batch,
             drop_lead, lead=None):
    rows = y_lru.shape[0]
    n_batch = rows // rows_per_batch
    consts = (wo_l, wo_g, g_post, g_pre, g_post2, w_up, w_dn)
    if lead is not None:
        assert not drop_lead
        tm = ROW_TILE
        grid, (h_spec, _, _), row = _token_specs(n_batch, rows_per_batch - T0, tm)
        h_specs, h_args = [h_spec, _const_spec(lead)], [h, lead]
        out_spec = row(D_MODEL)
        out_rows = rows
    elif drop_lead:
        tm = T0
        per_batch = rows_per_batch // tm
        grid = (n_batch, per_batch)
        row = lambda w: pl.BlockSpec((tm, w), lambda b, j: (b * per_batch + j, 0))
        out_spec = pl.BlockSpec(
            (tm, D_MODEL), lambda b, j: (b * (per_batch - 1) + jnp.maximum(j - 1, 0), 0))
        out_rows = rows - n_batch * tm
        h_specs, h_args = [row(D_MODEL)], [h]
    else:
        tm = ROW_TILE
        grid = (rows // tm,)
        row = lambda w: pl.BlockSpec((tm, w), lambda i: (i, 0))
        out_spec = row(D_MODEL)
        out_rows = rows
        h_specs, h_args = [row(D_MODEL)], [h]
    return pl.pallas_call(
        functools.partial(_out_mlp_kernel, rows_per_batch=rows_per_batch, n_batch=n_batch,
                          per_batch_grid=len(grid) == 2, from_tokens=lead is not None),
        grid=grid,
        in_specs=h_specs + [row(512), row(512)] + [_const_spec(a) for a in consts],
        out_specs=out_spec,
        out_shape=jax.ShapeDtypeStruct((out_rows, D_MODEL), jnp.float32),
        scratch_shapes=[pltpu.VMEM((tm, D_MODEL), jnp.float32)],
        compiler_params=pltpu.CompilerParams(
            dimension_semantics=("arbitrary",) * len(grid), vmem_limit_bytes=VMEM_LIMIT),
        name="out_mlp",
    )(*h_args, y_lru, y_gla, *map(_operand, consts))


def _block_diag_gates(wa, wx):
    per_half = LRU_HEADS // 2
    eye = jnp.eye(per_half, dtype=wa.dtype)

    def bd(w):
        w5 = w.reshape(-1, 2, per_half, LRU_HEAD_DIM, LRU_HEAD_DIM)
        return jnp.einsum('nhjil,jk->nhjikl', w5, eye).reshape(-1, 2, LRU_WIDTH // 2, LRU_WIDTH // 2)
    return jnp.concatenate([bd(wa), bd(wx)], axis=3).astype(jnp.bfloat16)


def kernel(x, meta_tokens, norm_mix_pre, norm_mix_post, norm_mlp_pre, norm_mlp_post,
           w_in, conv_w, conv_b,
           lru_wa_f, lru_ba_f, lru_wx_f, lru_bx_f, lru_lambda_f,
           lru_wa_b, lru_ba_b, lru_wx_b, lru_bx_b, lru_lambda_b,
           gla_wg_f, gla_bg_f, gla_wg_b, gla_bg_b, gla_head_norm,
           w_out, w_mlp_up, w_mlp_down):
    bsz, seq, d = x.shape
    depth = w_in.shape[0]
    tp = seq + T0
    assert d == D_MODEL and tp % TIME_TILE == 0 and tp % GLA_TIME_TILE == 0
    assert (bsz * tp) % ROW_TILE == 0
    assert ROW_TILE > T0 and tp % ROW_TILE == 0
    lead = jnp.concatenate([jnp.zeros((PADF, d), x.dtype), meta_tokens.astype(x.dtype)], axis=0)
    h = x.reshape(bsz * seq, d)
    in_place = depth > 1
    if not in_place:
        h = jnp.concatenate([jnp.broadcast_to(lead[None], (bsz, T0, d)), x], axis=1).reshape(-1, d)
    vec = lambda a: a.reshape(depth, 1, -1)
    bf = lambda a: a.astype(jnp.bfloat16)

    w_main = bf(w_in[:, :, :2560])
    w_zg = bf(jnp.tile(w_in[:, :, 2560:2592], (1, 1, 3)))
    wg = jnp.zeros((depth, 2 * GLA_RANK, 2 * GLA_QK), jnp.float32)
    wg = wg.at[:, :GLA_RANK, :GLA_QK].set(gla_wg_f).at[:, GLA_RANK:, GLA_QK:].set(gla_wg_b)
    wg_hi, wg_lo = _split_bf16(wg)
    wg3 = jnp.concatenate([wg_hi, wg_lo, wg_hi], axis=1)
    bg = vec(jnp.concatenate([gla_bg_f, gla_bg_b], axis=1))
    gates_f = _block_diag_gates(lru_wa_f, lru_wx_f)
    gates_b = _block_diag_gates(lru_wa_b, lru_wx_b)
    params = dict(
        g_pre=vec(norm_mix_pre), w_main=w_main, w_zg=w_zg, wg3=wg3, bg=bg,
        conv_w=conv_w, conv_b=vec(conv_b),
        lru_f=(gates_f, vec(lru_ba_f), vec(lru_bx_f), vec(lru_lambda_f)),
        lru_b=(gates_b, vec(lru_ba_b), vec(lru_bx_b), vec(lru_lambda_b)),
        head_norm=vec(gla_head_norm),
        out=(bf(w_out[:, :LRU_WIDTH]), bf(w_out[:, LRU_WIDTH:]), vec(norm_mix_post),
             vec(norm_mlp_pre), vec(norm_mlp_post), bf(w_mlp_up), bf(w_mlp_down)))

    for l in range(depth):
        at = lambda a: _Layer(a, l)
        p = jax.tree.map(at, params)
        tokens = dict(lead=lead) if (l == 0 and in_place) else {}
        outs = _in_proj(h, p['g_pre'], p['w_main'], p['w_zg'], p['wg3'], p['bg'],
                        p['conv_w'], p['conv_b'], n_batch=bsz, **tokens)
        xc3, gate3, qk3, v3, go3, gfb3 = (a.reshape(bsz, tp, a.shape[-1]) for a in outs)

        h_f = _lru_pass(xc3, *p['lru_f'], reverse=False)
        y_lru = _lru_pass(xc3, *p['lru_b'], reverse=True, hf=h_f, gate=gate3)
        o_f = _gla_pass(qk3, v3, gfb3, reverse=False)
        y_gla = _gla_pass(qk3, v3, gfb3, reverse=True, o_f=o_f, go=go3, head_norm=p['head_norm'])

        h = _out_mlp(h, y_lru.reshape(bsz * tp, -1), y_gla.reshape(bsz * tp, -1), *p['out'],
                     tp, drop_lead=(l == depth - 1), **tokens)
    return h.reshape(bsz, seq, d)
```

```python
import functools

import numpy as np
import jax
import jax.numpy as jnp
from jax import lax
from jax.experimental import pallas as pl
from jax.experimental.pallas import tpu as pltpu

D_MODEL = 1024
N_META = 16
CHUNK = 64
LRU_WIDTH = 512
LRU_HEADS = 8
LRU_HEAD_DIM = 64
LRU_C = 8.0
GLA_WIDTH = 512
GLA_HEADS = 4
GLA_DV = 128
GLA_DK = 64
GLA_QK = GLA_HEADS * GLA_DK
GLA_RANK = 16
GLA_GATE_NORM = 16.0
W_IN_MAIN = 2 * LRU_WIDTH + 2 * GLA_QK + 2 * GLA_WIDTH
D_FF = 4096
EPS = 1e-6
LOG2_E = 1.4426950408889634
F32_TINY = float(np.finfo(np.float32).tiny)

LEAD_CHUNKS = 4
T0 = LEAD_CHUNKS * CHUNK
PADF = T0 - N_META
ROW_TILE = 768
TIME_TILE = 1056
SCAN_UNROLL = 12
GLA_TIME_TILE = 768
GLA_UNROLL = 3
IN_SPLIT = 3
ROW_SPLIT = 2
FF_TILE = 1024
SUBLANES = 8
LANES = 128
MXU_ROWS = 256
VMEM_LIMIT = 56 * 1024 * 1024


class _Layer:
    def __init__(self, stacked, layer):
        self.stacked, self.layer, self.shape = stacked, layer, tuple(stacked.shape[1:])


def _operand(a):
    return a.stacked if isinstance(a, _Layer) else a


def _const_spec(a):
    nd = len(a.shape)
    if isinstance(a, _Layer):
        layer = a.layer
        return pl.BlockSpec((None,) + a.shape, lambda *_: (layer,) + (0,) * nd,
                            pipeline_mode=pl.Buffered(1))
    return pl.BlockSpec(a.shape, lambda *_: (0,) * nd, pipeline_mode=pl.Buffered(1))


def _rms_scale(x):
    return lax.rsqrt(jnp.mean(x * x, axis=-1, keepdims=True) + EPS)


def _sigmoid(x):
    return 0.5 * jnp.tanh(0.5 * x) + 0.5


def _softplus(x):
    return jnp.maximum(x, 0.0) + jnp.log1p(jnp.exp(-jnp.abs(x)))


def _split_bf16(x):
    hi = x.astype(jnp.bfloat16)
    lo = (x - hi.astype(jnp.float32)).astype(jnp.bfloat16)
    return hi, lo


def _token_tile(x_ref, lead_ref):
    tm = x_ref.shape[0]
    fetched = x_ref[...]
    first = jnp.concatenate([lead_ref[...], fetched[0:tm - T0, :]], axis=0)
    return jnp.where(pl.program_id(1) == 0, first, fetched)


def _in_proj_kernel(*refs, n_tiles, from_tokens):
    if from_tokens:
        h_ref, hp_ref, hn_ref, lead_ref, *refs = refs
    else:
        h_ref, hp_ref, hn_ref, *refs = refs
    (g_ref, w_ref, wzg_ref, wg_ref, bg_ref, cw_ref, cb_ref,
     xc_ref, gate_ref, qk_ref, v_ref, go_ref, gfb_ref, xs_ref) = refs
    tm = h_ref.shape[0]
    normed = lambda x: (x * _rms_scale(x) * g_ref[...]).astype(jnp.bfloat16)
    if from_tokens:
        h_tile = _token_tile(h_ref, lead_ref)
        no_next = pl.program_id(1) == n_tiles - 1
    else:
        h_tile = h_ref[...]
        no_next = pl.program_id(0) == n_tiles - 1
    halo = jnp.concatenate([hp_ref[...], hn_ref[...]], axis=0)
    xn = normed(h_tile)
    xn_ext = jnp.concatenate([xn, normed(halo)], axis=0)

    zg = jnp.dot(xn, wzg_ref[...], preferred_element_type=jnp.float32)
    zx = jnp.dot(xn_ext, w_ref[:, 0:LRU_WIDTH], preferred_element_type=jnp.float32)
    zhi = zg.astype(jnp.bfloat16)
    zlo = (zg - zhi.astype(jnp.float32)).astype(jnp.bfloat16)
    lane = lax.broadcasted_iota(jnp.int32, zg.shape, 1)
    lhs = jnp.where(lane < 4 * GLA_RANK, zhi, zlo)
    pre = jnp.dot(lhs, wg_ref[...], preferred_element_type=jnp.float32) + bg_ref[...]
    n = tm // IN_SPLIT
    parts = [slice(i * n, (i + 1) * n) for i in range(IN_SPLIT)]
    zs = [jnp.dot(xn[p, :], w_ref[:, LRU_WIDTH:W_IN_MAIN], preferred_element_type=jnp.float32)
          for p in parts]

    xs_ref[0:SUBLANES, :] = zx[tm:tm + SUBLANES, :]
    xs_ref[SUBLANES:SUBLANES + tm, :] = zx[0:tm, :]
    xs_ref[SUBLANES + tm:2 * SUBLANES + tm, :] = jnp.where(
        no_next, 0.0, zx[tm + SUBLANES:tm + 2 * SUBLANES, :])
    cw = cw_ref[...]
    xc_ref[...] = (cw[0:1, :] * xs_ref[SUBLANES - 2:SUBLANES - 2 + tm, :]
                   + cw[1:2, :] * xs_ref[SUBLANES - 1:SUBLANES - 1 + tm, :]
                   + cw[2:3, :] * xs_ref[SUBLANES:SUBLANES + tm, :]
                   + cw[3:4, :] * xs_ref[SUBLANES + 1:SUBLANES + 1 + tm, :]
                   + cb_ref[...])
    logsig = jnp.minimum(pre, 0.0) - jnp.log1p(jnp.exp(-jnp.abs(pre)))
    gfb_ref[...] = logsig * (LOG2_E / GLA_GATE_NORM)

    for p, z in zip(parts, zs):
        gt = z[:, 0:512]
        gelu = 0.5 * gt * (1.0 + jnp.tanh(0.7978845608028654 * (gt + 0.044715 * gt * gt * gt)))
        gate_ref[p, :] = gelu.astype(jnp.bfloat16)
        q = z[:, 512:768] * (GLA_DK ** -0.5)
        qk_ref[p, :] = jnp.concatenate([q, z[:, 768:1024]], axis=1).astype(jnp.bfloat16)
        v_ref[p, :] = z[:, 1024:1536].astype(jnp.bfloat16)
        go = z[:, 1536:2048]
        go_ref[p, :] = (go * _sigmoid(go)).astype(jnp.bfloat16)


def _token_specs(n_batch, seq, tm):
    per_batch = (seq + T0) // tm
    total = n_batch * seq
    start = lambda b, j: b * seq + j * tm - T0
    window = lambda rows, at: pl.BlockSpec((pl.Element(rows), pl.Element(D_MODEL)),
                                           lambda b, j: (pl.multiple_of(
                                               jnp.clip(at(b, j), 0, total - rows), SUBLANES), 0))
    specs = [window(tm, lambda b, j: b * seq + jnp.maximum(j * tm - T0, 0)),
             window(SUBLANES, lambda b, j: start(b, j) - SUBLANES),
             window(SUBLANES, lambda b, j: start(b, j) + tm)]
    return (n_batch, per_batch), specs, (lambda w: pl.BlockSpec((tm, w), lambda b, j: (b * per_batch + j, 0)))


def _in_proj(h, g, w_main, w_zg, wg3, bg, conv_w, conv_b, lead=None, n_batch=None):
    tm = ROW_TILE
    consts = (g, w_main, w_zg, wg3, bg, conv_w, conv_b)
    if lead is None:
        rows = h.shape[0]
        grid = (rows // tm,)
        g8 = tm // SUBLANES
        row = lambda w: pl.BlockSpec((tm, w), lambda i: (i, 0))
        h_specs = [row(D_MODEL),
                   pl.BlockSpec((SUBLANES, D_MODEL), lambda i: (jnp.maximum(i * g8 - 1, 0), 0)),
                   pl.BlockSpec((SUBLANES, D_MODEL),
                                lambda i: (jnp.minimum((i + 1) * g8, rows // SUBLANES - 1), 0))]
        h_args = [h, h, h]
    else:
        seq = h.shape[0] // n_batch
        rows = n_batch * (seq + T0)
        grid, h_specs, row = _token_specs(n_batch, seq, tm)
        h_specs.append(_const_spec(lead))
        h_args = [h, h, h, lead]
    outs = [(512, jnp.float32)] + [(512, jnp.bfloat16)] * 4 + [(512, jnp.float32)]
    return pl.pallas_call(
        functools.partial(_in_proj_kernel, n_tiles=grid[-1], from_tokens=lead is not None),
        grid=grid,
        in_specs=h_specs + [_const_spec(a) for a in consts],
        out_specs=[row(w) for w, _ in outs],
        out_shape=[jax.ShapeDtypeStruct((rows, w), dt) for w, dt in outs],
        scratch_shapes=[pltpu.VMEM((tm + 2 * SUBLANES, LRU_WIDTH), jnp.float32)],
        compiler_params=pltpu.CompilerParams(
            dimension_semantics=("arbitrary",) * len(grid), vmem_limit_bytes=VMEM_LIMIT),
        name="in_proj",
    )(*h_args, *map(_operand, consts))


def _lru_kernel(*refs, reverse, n_tiles):
    if reverse:
        (xc_ref, wg_ref, ba_ref, bx_ref, lam_ref, hf_ref, gate_ref,
         out_ref, a_scr, u_scr, carry_ref) = refs
    else:
        xc_ref, wg_ref, ba_ref, bx_ref, lam_ref, out_ref, a_scr, u_scr, carry_ref = refs
    bsz, tt, _ = xc_ref.shape
    step = pl.program_id(0)
    tile = (n_tiles - 1 - step) if reverse else step

    @pl.when(step == 0)
    def _():
        carry_ref[...] = jnp.zeros_like(carry_ref)

    n_slabs = LRU_WIDTH // LANES
    half = LRU_WIDTH // 2
    decay_rate = (-LRU_C) * _softplus(-lam_ref[...])
    for bi in range(bsz):
        xc = xc_ref[bi]
        xcb = xc.astype(jnp.bfloat16)
        p0 = jnp.dot(xcb[:, :half], wg_ref[0], preferred_element_type=jnp.float32)
        p1 = jnp.dot(xcb[:, half:], wg_ref[1], preferred_element_type=jnp.float32)
        r_pre = jnp.concatenate([p0[:, :half], p1[:, :half]], axis=1)
        i_pre = jnp.concatenate([p0[:, half:], p1[:, half:]], axis=1)
        r = _sigmoid(r_pre + ba_ref[...])
        gi = _sigmoid(i_pre + bx_ref[...])
        log_a = r * decay_rate
        a = jnp.exp(log_a)
        w = -jnp.tanh(log_a) * (1.0 + a * a)
        u = (w * lax.rsqrt(jnp.maximum(w, F32_TINY))) * (gi * xc)
        if not reverse:
            row = lax.broadcasted_iota(jnp.int32, xc.shape, 0)
            u = jnp.where(row + tile * tt >= PADF, u, 0.0)
        for k in range(n_slabs):
            a_scr[bi * n_slabs + k] = a[:, k * LANES:(k + 1) * LANES]
            u_scr[bi * n_slabs + k] = u[:, k * LANES:(k + 1) * LANES]

    run = tt // SUBLANES
    chains = bsz * n_slabs
    steps = lambda j: pl.ds(j, SUBLANES, stride=run)

    assert run % SCAN_UNROLL == 0
    n_blocks = run // SCAN_UNROLL
    block_base = lambda i: ((n_blocks - 1 - i) if reverse else i) * SCAN_UNROLL
    offsets = range(SCAN_UNROLL - 1, -1, -1) if reverse else range(SCAN_UNROLL)

    def run_ends(i, carry):
        state, prod = list(carry[0]), list(carry[1])
        base = block_base(i)
        for o in offsets:
            for k in range(chains):
                aj = a_scr[k, steps(base + o), :]
                state[k] = aj * state[k] + u_scr[k, steps(base + o), :]
                prod[k] = prod[k] * aj
        return tuple(state), tuple(prod)

    zeros = tuple(jnp.zeros((SUBLANES, LANES), jnp.float32) for _ in range(chains))
    ones = tuple(z + 1.0 for z in zeros)
    end_state, end_prod = lax.fori_loop(0, n_blocks, run_ends, (zeros, ones))
    end_state, end_prod = jnp.stack(end_state), jnp.stack(end_prod)

    c = carry_ref[...]
    order = range(SUBLANES - 1, -1, -1) if reverse else range(SUBLANES)
    entering = [None] * SUBLANES
    for r in order:
        entering[r] = c
        c = end_prod[:, r:r + 1, :] * c + end_state[:, r:r + 1, :]
    carry_ref[...] = c
    entering = jnp.concatenate(entering, axis=1)

    def scan(i, state):
        state = list(state)
        base = block_base(i)
        for o in offsets:
            for k in range(chains):
                state[k] = a_scr[k, steps(base + o), :] * state[k] + u_scr[k, steps(base + o), :]
                u_scr[k, steps(base + o), :] = state[k]
        return tuple(state)

    lax.fori_loop(0, n_blocks, scan, tuple(entering[k] for k in range(chains)))

    for bi in range(bsz):
        hs = jnp.concatenate([u_scr[bi * n_slabs + k] for k in range(n_slabs)], axis=1)
        if reverse:
            gate = gate_ref[bi].astype(jnp.float32)
            out_ref[bi] = ((hf_ref[bi] + hs) * gate).astype(out_ref.dtype)
        else:
            out_ref[bi] = hs


def _lru_pass(xc, wgate, ba, bx, lam, *, reverse, hf=None, gate=None):
    bsz, tp, _ = xc.shape
    tt = TIME_TILE
    nt = tp // tt
    chains = bsz * (LRU_WIDTH // LANES)
    tsel = (lambda t: nt - 1 - t) if reverse else (lambda t: t)
    cur = pl.BlockSpec((bsz, tt, LRU_WIDTH), lambda t: (0, tsel(t), 0))
    in_specs = [cur] + [_const_spec(a) for a in (wgate, ba, bx, lam)]
    args = [xc, wgate, ba, bx, lam]
    if reverse:
        in_specs += [cur, cur]
        args += [hf, gate]
        out_dtype = jnp.bfloat16
    else:
        out_dtype = jnp.float32
    return pl.pallas_call(
        functools.partial(_lru_kernel, reverse=reverse, n_tiles=nt),
        grid=(nt,),
        in_specs=in_specs,
        out_specs=cur,
        out_shape=jax.ShapeDtypeStruct((bsz, tp, LRU_WIDTH), out_dtype),
        scratch_shapes=[pltpu.VMEM((chains, tt, LANES), jnp.float32),
                        pltpu.VMEM((chains, tt, LANES), jnp.float32),
                        pltpu.VMEM((chains, 1, LANES), jnp.float32)],
        compiler_params=pltpu.CompilerParams(
            dimension_semantics=("arbitrary",), vmem_limit_bytes=VMEM_LIMIT),
        name="lru_bwd" if reverse else "lru_fwd",
    )(*map(_operand, args))


def _gla_constants(reverse):
    c, w = CHUNK, SUBLANES
    t = np.arange(c)[:, None]
    s = np.arange(c)[None, :]
    gt, gs, pt, ps = t // w, s // w, t // 2, s // 2
    pair_code = 2 + (s % w) // 2
    if reverse:
        code = np.where((pt == ps) & (s > t), 1,
                        np.where((gt == gs) & (ps > pt), pair_code, np.where(gs > gt, w + gs, -1)))
        blocks = [s >= t, (t % 2 == 1) & (s == t - 1)]
    else:
        code = np.where((pt == ps) & (s <= t), t - s,
                        np.where((gt == gs) & (ps < pt), pair_code, np.where(gs < gt, w + gs, -1)))
        blocks = [s <= t, (t % 2 == 0) & (s == t + 1)]
    lmat = np.concatenate([blk.astype(np.float32) for blk in blocks], axis=0)
    lmat = np.concatenate([lmat, lmat], axis=1)
    code = np.tile(code.astype(np.int32), (1, GLA_HEADS))
    hmask = np.kron(np.eye(GLA_HEADS, dtype=np.float32), np.ones((c, GLA_DK), np.float32))
    return jnp.asarray(lmat, jnp.bfloat16), jnp.asarray(code), jnp.asarray(hmask, jnp.bfloat16)


def _gla_decays(g, lmat):
    ghi, glo = _split_bf16(g)
    return jnp.dot(lmat, jnp.concatenate([ghi, glo], axis=0), preferred_element_type=jnp.float32)


def _gla_scores(q, k, g, logdec, hmask, reverse):
    n_groups = CHUNK // SUBLANES
    pairs = SUBLANES // 2

    def head_stack(x):
        return jnp.concatenate([x.astype(jnp.bfloat16)] * GLA_HEADS, axis=0) * hmask

    def terms(lhs, rhs):
        return lax.dot_general(jnp.concatenate(lhs, axis=0).astype(jnp.bfloat16), head_stack(rhs),
                               (((1,), (1,)), ((), ())), preferred_element_type=jnp.float32)

    def facing_rows(first):
        return jnp.concatenate([jnp.broadcast_to(b[first + gi * SUBLANES:first + gi * SUBLANES + 1, :],
                                                 (SUBLANES, GLA_QK)) for gi in range(n_groups)], axis=0)

    b = logdec[0:CHUNK, :]

    step = q * jnp.exp2(g)
    p_pair = terms([step] if reverse else [q, step], k)

    k_pair = k * jnp.exp2(logdec[CHUNK:2 * CHUNK, :])
    pair_ids = range(1, pairs) if reverse else range(pairs - 1)
    lhs = [q * jnp.exp2(jnp.minimum(b - facing_rows(2 * p if reverse else 2 * p + 1), 0.0))
           for p in pair_ids]
    p_group = terms(lhs, k_pair)

    b_anchor = facing_rows(0 if reverse else SUBLANES - 1)
    groups = list(range(1, n_groups)) if reverse else list(range(n_groups - 1))
    row_sel = (lambda gi: slice(0, gi * SUBLANES)) if reverse else \
              (lambda gi: slice((gi + 1) * SUBLANES, CHUNK))
    anchor_row = (lambda gi: gi * SUBLANES) if reverse else (lambda gi: gi * SUBLANES + SUBLANES - 1)
    lhs = [q[row_sel(gi), :] * jnp.exp2(b[row_sel(gi), :] - b[anchor_row(gi):anchor_row(gi) + 1, :])
           for gi in groups]
    p_cross = terms(lhs, k * jnp.exp2(b_anchor - b))

    last = 0 if reverse else CHUNK - 1
    b_last = b[last:last + 1, :]
    qe = (q * jnp.exp2(b)).astype(jnp.bfloat16)
    ke = (k * jnp.exp2(b_last - b)).astype(jnp.bfloat16)
    return p_pair, p_group, p_cross, qe, ke, jnp.exp2(b_last)


def _gla_apply(p_pair, p_group, p_cross, qe, ke, e_last, vb, s_ref, code, reverse):
    n_groups = CHUNK // SUBLANES
    pairs = SUBLANES // 2
    groups = list(range(1, n_groups)) if reverse else list(range(n_groups - 1))
    scores = jnp.zeros((CHUNK, GLA_HEADS * CHUNK), jnp.float32)
    for i, d in enumerate([1] if reverse else [0, 1]):
        scores = jnp.where(code == d, p_pair[i * CHUNK:(i + 1) * CHUNK, :], scores)
    for i, p in enumerate(range(1, pairs) if reverse else range(pairs - 1)):
        scores = jnp.where(code == 2 + p, p_group[i * CHUNK:(i + 1) * CHUNK, :], scores)
    off = 0
    for gi in groups:
        rows = slice(0, gi * SUBLANES) if reverse else slice((gi + 1) * SUBLANES, CHUNK)
        n = rows.stop - rows.start
        part = jnp.where(code[rows, :] == SUBLANES + gi, p_cross[off:off + n, :], scores[rows, :])
        scores = jnp.concatenate([part, scores[n:, :]] if reverse else [scores[:CHUNK - n, :], part],
                                 axis=0)
        off += n
    sb = scores.astype(jnp.bfloat16)

    eye = (lax.broadcasted_iota(jnp.int32, (GLA_DK, GLA_DK), 0)
           == lax.broadcasted_iota(jnp.int32, (GLA_DK, GLA_DK), 1))
    zero_v = jnp.zeros((CHUNK, GLA_DV), jnp.bfloat16)
    zero_s = jnp.zeros((GLA_DK, GLA_DV), jnp.bfloat16)
    outs = []
    for pair in range(GLA_HEADS // 2):
        heads = (2 * pair, 2 * pair + 1)
        vh = [vb[:, hd * GLA_DV:(hd + 1) * GLA_DV] for hd in heads]
        st = [s_ref[hd] for hd in heads]
        sh = [s.astype(jnp.bfloat16) for s in st]
        lanes = slice(pair * 2 * CHUNK, (pair + 1) * 2 * CHUNK)
        lhs = jnp.concatenate([sb[:, lanes], qe[:, lanes]], axis=1)
        rhs = jnp.concatenate([jnp.concatenate([vh[0], zero_v], axis=1),
                               jnp.concatenate([zero_v, vh[1]], axis=1),
                               jnp.concatenate([sh[0], zero_s], axis=1),
                               jnp.concatenate([zero_s, sh[1]], axis=1)], axis=0)
        outs.append(jnp.dot(lhs, rhs, preferred_element_type=jnp.float32))
        for i, hd in enumerate(heads):
            ks = slice(hd * GLA_DK, (hd + 1) * GLA_DK)
            kv = lax.dot_general(ke[:, ks], vh[i], (((0,), (0,)), ((), ())),
                                 preferred_element_type=jnp.float32)
            e_col = jnp.sum(jnp.where(eye, e_last[:, ks], 0.0), axis=1, keepdims=True)
            s_ref[hd] = st[i] * e_col + kv
    return jnp.concatenate(outs, axis=1)


def _gla_kernel(*refs, reverse):
    if reverse:
        (qk_ref, v_ref, g_ref, lmat_ref, code_ref, hmask_ref, of_ref, go_ref, hn_ref,
         out_ref, s_ref) = refs
    else:
        qk_ref, v_ref, g_ref, lmat_ref, code_ref, hmask_ref, out_ref, s_ref = refs
    bsz, tt, _ = qk_ref.shape
    n_chunks = tt // CHUNK

    @pl.when(pl.program_id(0) == 0)
    def _():
        s_ref[...] = jnp.zeros_like(s_ref)

    lmat = lmat_ref[...]
    code = code_ref[...]
    hmask = hmask_ref[...]

    def chunk_body(j, carry):
        work = []
        for u in range(GLA_UNROLL):
            jj = j * GLA_UNROLL + u
            cidx = (n_chunks - 1 - jj) if reverse else jj
            rows = pl.ds(pl.multiple_of(cidx * CHUNK, CHUNK), CHUNK)
            work += [(bi, rows) for bi in range(bsz)]
        decs = [_gla_decays(g_ref[bi, rows, :], lmat) for bi, rows in work]
        terms = [_gla_scores(qk_ref[bi, rows, 0:GLA_QK], qk_ref[bi, rows, GLA_QK:2 * GLA_QK],
                             g_ref[bi, rows, :], dec, hmask, reverse)
                 for (bi, rows), dec in zip(work, decs)]
        for (bi, rows), term in zip(work, terms):
            o = _gla_apply(*term, v_ref[bi, rows, :].astype(jnp.bfloat16), s_ref.at[bi], code,
                           reverse)
            if reverse:
                o = o + of_ref[bi, rows, :]
                parts = []
                for hd in range(GLA_HEADS):
                    oh = o[:, hd * GLA_DV:(hd + 1) * GLA_DV]
                    parts.append(oh * _rms_scale(oh))
                y = jnp.concatenate(parts, axis=1) * hn_ref[...] * go_ref[bi, rows, :].astype(jnp.float32)
                out_ref[bi, rows, :] = y.astype(out_ref.dtype)
            else:
                out_ref[bi, rows, :] = o
        return carry

    lax.fori_loop(0, n_chunks // GLA_UNROLL, chunk_body, 0)


def _gla_pass(qk, v, gfb, *, reverse, o_f=None, go=None, head_norm=None):
    bsz, tp, _ = qk.shape
    tt = GLA_TIME_TILE
    nt = tp // tt
    tsel = (lambda t: nt - 1 - t) if reverse else (lambda t: t)
    blk = lambda w, c=0: pl.BlockSpec((bsz, tt, w), lambda t: (0, tsel(t), c))
    consts = _gla_constants(reverse)
    in_specs = [blk(512), blk(512), blk(GLA_QK, 1 if reverse else 0)]
    in_specs += [_const_spec(a) for a in consts]
    args = [qk, v, gfb, *consts]
    if reverse:
        in_specs += [blk(512), blk(512), _const_spec(head_norm)]
        args += [o_f, go, head_norm]
        out_dtype = jnp.bfloat16
    else:
        out_dtype = jnp.float32
    return pl.pallas_call(
        functools.partial(_gla_kernel, reverse=reverse),
        grid=(nt,),
        in_specs=in_specs,
        out_specs=blk(512),
        out_shape=jax.ShapeDtypeStruct((bsz, tp, GLA_WIDTH), out_dtype),
        scratch_shapes=[pltpu.VMEM((bsz, GLA_HEADS, GLA_DK, GLA_DV), jnp.float32)],
        compiler_params=pltpu.CompilerParams(
            dimension_semantics=("arbitrary",), vmem_limit_bytes=VMEM_LIMIT),
        name="gla_bwd" if reverse else "gla_fwd",
    )(*map(_operand, args))


def _token_rows_copy(buf_ref, out_ref, sem_ref, slot, lead_tile, first_token):
    tm = buf_ref.shape[1]
    if lead_tile:
        return pltpu.make_async_copy(buf_ref.at[slot, pl.ds(T0, tm - T0)],
                                     out_ref.at[pl.ds(first_token, tm - T0)], sem_ref.at[slot])
    return pltpu.make_async_copy(buf_ref.at[slot], out_ref.at[pl.ds(first_token, tm)],
                                 sem_ref.at[slot])


def _out_mlp_kernel(*refs, rows_per_batch, n_batch, per_batch, from_tokens, drop_lead):
    per_batch_grid = per_batch is not None
    if from_tokens:
        x_ref, lead_ref, *refs = refs
        h_tile = _token_tile(x_ref, lead_ref)
    else:
        h_tile, *refs = refs
    if drop_lead:
        *refs, buf_ref, sem_ref = refs
    (yl_ref, yg_ref, wo_ref, gpost_ref, gpre_ref, gpost2_ref,
     wup_ref, wdn_ref, out_ref, acc_ref) = refs
    tm = acc_ref.shape[0]
    tile = pl.program_id(0)
    if per_batch_grid:
        tile = tile * pl.num_programs(1) + pl.program_id(1)
    if drop_lead:
        n_steps = n_batch * per_batch
        slot = lax.rem(tile, 2)

        def wait_step(step, its_slot):
            was_lead = lax.rem(step, per_batch) == 0
            for lead_tile in (True, False):
                @pl.when(was_lead == lead_tile)
                def _():
                    _token_rows_copy(buf_ref, out_ref, sem_ref, its_slot, lead_tile, 0).wait()

        @pl.when(tile >= 2)
        def _():
            wait_step(tile - 2, slot)
        store_ref = buf_ref.at[slot]
    else:
        store_ref = out_ref
    n_split = ROW_SPLIT if tm >= ROW_SPLIT * MXU_ROWS else 1
    n = tm // n_split
    parts = [slice(i * n, (i + 1) * n) for i in range(n_split)]
    mix = [jnp.dot(yl_ref[p, :], wo_ref[0:LRU_WIDTH, :], preferred_element_type=jnp.float32)
           + jnp.dot(yg_ref[p, :], wo_ref[LRU_WIDTH:, :], preferred_element_type=jnp.float32)
           for p in parts]
    h1, xn = [], []
    for i, p in enumerate(parts):
        row = lax.broadcasted_iota(jnp.int32, (n, 1), 0) + (tile * tm + i * n)
        real = row >= PADF
        for b in range(1, n_batch):
            real = real & ((row < b * rows_per_batch) | (row >= b * rows_per_batch + PADF))
        h1.append(h_tile[p, :] + jnp.where(real, mix[i] * _rms_scale(mix[i]) * gpost_ref[...], 0.0))
        xn.append((h1[i] * _rms_scale(h1[i]) * gpre_ref[...]).astype(jnp.bfloat16))
    for c in range(D_FF // FF_TILE):
        cs = slice(c * FF_TILE, (c + 1) * FF_TILE)
        for i, p in enumerate(parts):
            up = jnp.dot(xn[i], wup_ref[:, cs], preferred_element_type=jnp.float32)
            act = jnp.square(jnp.maximum(up, 0.0)).astype(jnp.bfloat16)
            part = jnp.dot(act, wdn_ref[cs, :], preferred_element_type=jnp.float32)
            if c == 0:
                acc_ref[p, :] = part
            else:
                acc_ref[p, :] += part
    for i, p in enumerate(parts):
        ff = acc_ref[p, :]
        store_ref[p, :] = h1[i] + ff * _rms_scale(ff) * gpost2_ref[...]
    if drop_lead:
        b, j = pl.program_id(0), pl.program_id(1)
        seq = rows_per_batch - T0
        for lead_tile in (True, False):
            @pl.when((j == 0) == lead_tile)
            def _():
                first = b * seq + (0 if lead_tile else j * tm - T0)
                _token_rows_copy(buf_ref, out_ref, sem_ref, slot, lead_tile,
                                 pl.multiple_of(first, SUBLANES)).start()

        @pl.when(tile == n_steps - 1)
        def _():
            wait_step(tile, slot)
            if n_steps >= 2:
                wait_step(tile - 1, 1 - slot)


def _out_mlp(h, y_lru, y_gla, wo, g_post, g_pre, g_post2, w_up, w_dn, rows_per_batch,
             drop_lead, lead=None):
    rows = y_lru.shape[0]
    n_batch = rows // rows_per_batch
    consts = (wo, g_post, g_pre, g_post2, w_up, w_dn)
    tm = ROW_TILE
    per_batch = None
    scratch = [pltpu.VMEM((tm, D_MODEL), jnp.float32)]
    if lead is not None:
        assert not drop_lead
        grid, (h_spec, _, _), row = _token_specs(n_batch, rows_per_batch - T0, tm)
        per_batch = grid[1]
        h_specs, h_args = [h_spec, _const_spec(lead)], [h, lead]
        out_spec = row(D_MODEL)
        out_rows = rows
    elif drop_lead:
        per_batch = rows_per_batch // tm
        grid = (n_batch, per_batch)
        row = lambda w: pl.BlockSpec((tm, w), lambda b, j: (b * per_batch + j, 0))
        out_spec = pl.BlockSpec(memory_space=pl.ANY)
        out_rows = rows - n_batch * T0
        h_specs, h_args = [row(D_MODEL)], [h]
        scratch += [pltpu.VMEM((2, tm, D_MODEL), jnp.float32), pltpu.SemaphoreType.DMA((2,))]
    else:
        grid = (rows // tm,)
        row = lambda w: pl.BlockSpec((tm, w), lambda i: (i, 0))
        out_spec = row(D_MODEL)
        out_rows = rows
        h_specs, h_args = [row(D_MODEL)], [h]
    return pl.pallas_call(
        functools.partial(_out_mlp_kernel, rows_per_batch=rows_per_batch, n_batch=n_batch,
                          per_batch=per_batch, from_tokens=lead is not None, drop_lead=drop_lead),
        grid=grid,
        in_specs=h_specs + [row(512), row(512)] + [_const_spec(a) for a in consts],
        out_specs=out_spec,
        out_shape=jax.ShapeDtypeStruct((out_rows, D_MODEL), jnp.float32),
        scratch_shapes=scratch,
        compiler_params=pltpu.CompilerParams(
            dimension_semantics=("arbitrary",) * len(grid), vmem_limit_bytes=VMEM_LIMIT),
        name="out_mlp",
    )(*h_args, y_lru, y_gla, *map(_operand, consts))


def _block_diag_gates(wa, wx):
    per_half = LRU_HEADS // 2
    eye = jnp.eye(per_half, dtype=wa.dtype)

    def bd(w):
        w5 = w.reshape(-1, 2, per_half, LRU_HEAD_DIM, LRU_HEAD_DIM)
        return jnp.einsum('nhjil,jk->nhjikl', w5, eye).reshape(-1, 2, LRU_WIDTH // 2, LRU_WIDTH // 2)
    return jnp.concatenate([bd(wa), bd(wx)], axis=3).astype(jnp.bfloat16)


def kernel(x, meta_tokens, norm_mix_pre, norm_mix_post, norm_mlp_pre, norm_mlp_post,
           w_in, conv_w, conv_b,
           lru_wa_f, lru_ba_f, lru_wx_f, lru_bx_f, lru_lambda_f,
           lru_wa_b, lru_ba_b, lru_wx_b, lru_bx_b, lru_lambda_b,
           gla_wg_f, gla_bg_f, gla_wg_b, gla_bg_b, gla_head_norm,
           w_out, w_mlp_up, w_mlp_down):
    bsz, seq, d = x.shape
    depth = w_in.shape[0]
    tp = seq + T0
    assert d == D_MODEL and tp % TIME_TILE == 0 and tp % GLA_TIME_TILE == 0
    assert (bsz * tp) % ROW_TILE == 0
    assert ROW_TILE > T0 and tp % ROW_TILE == 0
    lead = jnp.concatenate([jnp.zeros((PADF, d), x.dtype), meta_tokens.astype(x.dtype)], axis=0)
    h = x.reshape(bsz * seq, d)
    in_place = depth > 1
    if not in_place:
        h = jnp.concatenate([jnp.broadcast_to(lead[None], (bsz, T0, d)), x], axis=1).reshape(-1, d)
    vec = lambda a: a.reshape(depth, 1, -1)
    bf = lambda a: a.astype(jnp.bfloat16)

    w_main = bf(w_in)
    w_zg = bf(jnp.tile(w_in[:, :, W_IN_MAIN:], (1, 1, 3)))
    wg = jnp.zeros((depth, 2 * GLA_RANK, 2 * GLA_QK), jnp.float32)
    wg = wg.at[:, :GLA_RANK, :GLA_QK].set(gla_wg_f).at[:, GLA_RANK:, GLA_QK:].set(gla_wg_b)
    wg_hi, wg_lo = _split_bf16(wg)
    wg3 = jnp.concatenate([wg_hi, wg_lo, wg_hi], axis=1)
    bg = vec(jnp.concatenate([gla_bg_f, gla_bg_b], axis=1))
    gates_f = _block_diag_gates(lru_wa_f, lru_wx_f)
    gates_b = _block_diag_gates(lru_wa_b, lru_wx_b)
    params = dict(
        g_pre=vec(norm_mix_pre), w_main=w_main, w_zg=w_zg, wg3=wg3, bg=bg,
        conv_w=conv_w, conv_b=vec(conv_b),
        lru_f=(gates_f, vec(lru_ba_f), vec(lru_bx_f), vec(lru_lambda_f)),
        lru_b=(gates_b, vec(lru_ba_b), vec(lru_bx_b), vec(lru_lambda_b)),
        head_norm=vec(gla_head_norm),
        out=(bf(w_out), vec(norm_mix_post),
             vec(norm_mlp_pre), vec(norm_mlp_post), bf(w_mlp_up), bf(w_mlp_down)))

    for l in range(depth):
        at = lambda a: _Layer(a, l)
        p = jax.tree.map(at, params)
        tokens = dict(lead=lead) if (l == 0 and in_place) else {}
        outs = _in_proj(h, p['g_pre'], p['w_main'], p['w_zg'], p['wg3'], p['bg'],
                        p['conv_w'], p['conv_b'], n_batch=bsz, **tokens)
        xc3, gate3, qk3, v3, go3, gfb3 = (a.reshape(bsz, tp, a.shape[-1]) for a in outs)

        h_f = _lru_pass(xc3, *p['lru_f'], reverse=False)
        y_lru = _lru_pass(xc3, *p['lru_b'], reverse=True, hf=h_f, gate=gate3)
        o_f = _gla_pass(qk3, v3, gfb3, reverse=False)
        y_gla = _gla_pass(qk3, v3, gfb3, reverse=True, o_f=o_f, go=go3, head_norm=p['head_norm'])

        h = _out_mlp(h, y_lru.reshape(bsz * tp, -1), y_gla.reshape(bsz * tp, -1), *p['out'],
                     tp, drop_lead=(l == depth - 1), **tokens)
    return h.reshape(bsz, seq, d)
```

```python
import functools

import numpy as np
import jax
import jax.numpy as jnp
from jax import lax
from jax.experimental import pallas as pl
from jax.experimental.pallas import tpu as pltpu

D_MODEL = 1024
N_META = 16
CHUNK = 64
LRU_WIDTH = 512
LRU_HEADS = 8
LRU_HEAD_DIM = 64
LRU_C = 8.0
GLA_WIDTH = 512
GLA_HEADS = 4
GLA_DV = 128
GLA_DK = 64
GLA_QK = GLA_HEADS * GLA_DK
GLA_RANK = 16
GLA_GATE_NORM = 16.0
W_IN_MAIN = 2 * LRU_WIDTH + 2 * GLA_QK + 2 * GLA_WIDTH
D_FF = 4096
EPS = 1e-6
LOG2_E = 1.4426950408889634
F32_TINY = float(np.finfo(np.float32).tiny)

LEAD_CHUNKS = 4
T0 = LEAD_CHUNKS * CHUNK
PADF = T0 - N_META
ROW_TILE = 768
TIME_TILE = 1056
SCAN_UNROLL = 12
GLA_TIME_TILE = 768
GLA_UNROLL = 3
IN_SPLIT = 3
ROW_SPLIT = 2
FF_TILE = 1024
SUBLANES = 8
LANES = 128
MXU_ROWS = 256
VMEM_LIMIT = 56 * 1024 * 1024


class _Layer:
    def __init__(self, stacked, layer):
        self.stacked, self.layer, self.shape = stacked, layer, tuple(stacked.shape[1:])


def _operand(a):
    return a.stacked if isinstance(a, _Layer) else a


def _const_spec(a):
    nd = len(a.shape)
    if isinstance(a, _Layer):
        layer = a.layer
        return pl.BlockSpec((None,) + a.shape, lambda *_: (layer,) + (0,) * nd,
                            pipeline_mode=pl.Buffered(1))
    return pl.BlockSpec(a.shape, lambda *_: (0,) * nd, pipeline_mode=pl.Buffered(1))


def _rms_scale(x):
    return lax.rsqrt(jnp.mean(x * x, axis=-1, keepdims=True) + EPS)


def _sigmoid(x):
    return 0.5 * jnp.tanh(0.5 * x) + 0.5


def _softplus(x):
    return jnp.maximum(x, 0.0) + jnp.log1p(jnp.exp(-jnp.abs(x)))


def _split_bf16(x):
    hi = x.astype(jnp.bfloat16)
    lo = (x - hi.astype(jnp.float32)).astype(jnp.bfloat16)
    return hi, lo


def _token_tile(x_ref, lead_ref):
    tm = x_ref.shape[0]
    fetched = x_ref[...]
    first = jnp.concatenate([lead_ref[...], fetched[0:tm - T0, :]], axis=0)
    return jnp.where(pl.program_id(1) == 0, first, fetched)


def _in_proj_kernel(*refs, n_tiles, from_tokens):
    if from_tokens:
        h_ref, hp_ref, hn_ref, lead_ref, *refs = refs
    else:
        h_ref, hp_ref, hn_ref, *refs = refs
    (g_ref, w_ref, wzg_ref, wg_ref, bg_ref, cw_ref, cb_ref,
     xc_ref, gate_ref, qk_ref, v_ref, go_ref, gfb_ref, xs_ref) = refs
    tm = h_ref.shape[0]
    normed = lambda x: (x * _rms_scale(x) * g_ref[...]).astype(jnp.bfloat16)
    if from_tokens:
        h_tile = _token_tile(h_ref, lead_ref)
        no_next = pl.program_id(1) == n_tiles - 1
    else:
        h_tile = h_ref[...]
        no_next = pl.program_id(0) == n_tiles - 1
    halo = jnp.concatenate([hp_ref[...], hn_ref[...]], axis=0)
    xn = normed(h_tile)
    xn_ext = jnp.concatenate([xn, normed(halo)], axis=0)

    zg = jnp.dot(xn, wzg_ref[...], preferred_element_type=jnp.float32)
    zx = jnp.dot(xn_ext, w_ref[:, 0:LRU_WIDTH], preferred_element_type=jnp.float32)
    zhi = zg.astype(jnp.bfloat16)
    zlo = (zg - zhi.astype(jnp.float32)).astype(jnp.bfloat16)
    lane = lax.broadcasted_iota(jnp.int32, zg.shape, 1)
    lhs = jnp.where(lane < 4 * GLA_RANK, zhi, zlo)
    pre = jnp.dot(lhs, wg_ref[...], preferred_element_type=jnp.float32) + bg_ref[...]
    n = tm // IN_SPLIT
    parts = [slice(i * n, (i + 1) * n) for i in range(IN_SPLIT)]
    zs = [jnp.dot(xn[p, :], w_ref[:, LRU_WIDTH:W_IN_MAIN], preferred_element_type=jnp.float32)
          for p in parts]

    xs_ref[0:SUBLANES, :] = zx[tm:tm + SUBLANES, :]
    xs_ref[SUBLANES:SUBLANES + tm, :] = zx[0:tm, :]
    xs_ref[SUBLANES + tm:2 * SUBLANES + tm, :] = jnp.where(
        no_next, 0.0, zx[tm + SUBLANES:tm + 2 * SUBLANES, :])
    cw = cw_ref[...]
    xc_ref[...] = (cw[0:1, :] * xs_ref[SUBLANES - 2:SUBLANES - 2 + tm, :]
                   + cw[1:2, :] * xs_ref[SUBLANES - 1:SUBLANES - 1 + tm, :]
                   + cw[2:3, :] * xs_ref[SUBLANES:SUBLANES + tm, :]
                   + cw[3:4, :] * xs_ref[SUBLANES + 1:SUBLANES + 1 + tm, :]
                   + cb_ref[...])
    logsig = jnp.minimum(pre, 0.0) - jnp.log(1.0 + jnp.exp(-jnp.abs(pre)))
    gfb_ref[...] = logsig * (LOG2_E / GLA_GATE_NORM)

    for p, z in zip(parts, zs):
        gt = z[:, 0:512]
        gelu = 0.5 * gt * (1.0 + jnp.tanh(0.7978845608028654 * (gt + 0.044715 * gt * gt * gt)))
        gate_ref[p, :] = gelu.astype(jnp.bfloat16)
        q = z[:, 512:768] * (GLA_DK ** -0.5)
        qk_ref[p, :] = jnp.concatenate([q, z[:, 768:1024]], axis=1).astype(jnp.bfloat16)
        v_ref[p, :] = z[:, 1024:1536].astype(jnp.bfloat16)
        go = z[:, 1536:2048]
        go_ref[p, :] = (go * _sigmoid(go)).astype(jnp.bfloat16)


def _token_specs(n_batch, seq, tm):
    per_batch = (seq + T0) // tm
    total = n_batch * seq
    start = lambda b, j: b * seq + j * tm - T0
    window = lambda rows, at: pl.BlockSpec((pl.Element(rows), pl.Element(D_MODEL)),
                                           lambda b, j: (pl.multiple_of(
                                               jnp.clip(at(b, j), 0, total - rows), SUBLANES), 0))
    specs = [window(tm, lambda b, j: b * seq + jnp.maximum(j * tm - T0, 0)),
             window(SUBLANES, lambda b, j: start(b, j) - SUBLANES),
             window(SUBLANES, lambda b, j: start(b, j) + tm)]
    return (n_batch, per_batch), specs, (lambda w: pl.BlockSpec((tm, w), lambda b, j: (b * per_batch + j, 0)))


def _in_proj(h, g, w_main, w_zg, wg3, bg, conv_w, conv_b, lead=None, n_batch=None):
    tm = ROW_TILE
    consts = (g, w_main, w_zg, wg3, bg, conv_w, conv_b)
    if lead is None:
        rows = h.shape[0]
        grid = (rows // tm,)
        g8 = tm // SUBLANES
        row = lambda w: pl.BlockSpec((tm, w), lambda i: (i, 0))
        h_specs = [row(D_MODEL),
                   pl.BlockSpec((SUBLANES, D_MODEL), lambda i: (jnp.maximum(i * g8 - 1, 0), 0)),
                   pl.BlockSpec((SUBLANES, D_MODEL),
                                lambda i: (jnp.minimum((i + 1) * g8, rows // SUBLANES - 1), 0))]
        h_args = [h, h, h]
    else:
        seq = h.shape[0] // n_batch
        rows = n_batch * (seq + T0)
        grid, h_specs, row = _token_specs(n_batch, seq, tm)
        h_specs.append(_const_spec(lead))
        h_args = [h, h, h, lead]
    outs = [(512, jnp.float32)] + [(512, jnp.bfloat16)] * 4 + [(512, jnp.float32)]
    return pl.pallas_call(
        functools.partial(_in_proj_kernel, n_tiles=grid[-1], from_tokens=lead is not None),
        grid=grid,
        in_specs=h_specs + [_const_spec(a) for a in consts],
        out_specs=[row(w) for w, _ in outs],
        out_shape=[jax.ShapeDtypeStruct((rows, w), dt) for w, dt in outs],
        scratch_shapes=[pltpu.VMEM((tm + 2 * SUBLANES, LRU_WIDTH), jnp.float32)],
        compiler_params=pltpu.CompilerParams(
            dimension_semantics=("arbitrary",) * len(grid), vmem_limit_bytes=VMEM_LIMIT),
        name="in_proj",
    )(*h_args, *map(_operand, consts))


def _lru_kernel(*refs, reverse, n_tiles):
    if reverse:
        (xc_ref, wg_ref, ba_ref, bx_ref, lam_ref, hf_ref, gate_ref,
         out_ref, a_scr, u_scr, carry_ref) = refs
    else:
        xc_ref, wg_ref, ba_ref, bx_ref, lam_ref, out_ref, a_scr, u_scr, carry_ref = refs
    bsz, tt, _ = xc_ref.shape
    step = pl.program_id(0)
    tile = (n_tiles - 1 - step) if reverse else step

    @pl.when(step == 0)
    def _():
        carry_ref[...] = jnp.zeros_like(carry_ref)

    n_slabs = LRU_WIDTH // LANES
    half = LRU_WIDTH // 2
    decay_rate = (-LRU_C) * _softplus(-lam_ref[...])
    for bi in range(bsz):
        xc = xc_ref[bi]
        xcb = xc.astype(jnp.bfloat16)
        p0 = jnp.dot(xcb[:, :half], wg_ref[0], preferred_element_type=jnp.float32)
        p1 = jnp.dot(xcb[:, half:], wg_ref[1], preferred_element_type=jnp.float32)
        r_pre = jnp.concatenate([p0[:, :half], p1[:, :half]], axis=1)
        i_pre = jnp.concatenate([p0[:, half:], p1[:, half:]], axis=1)
        r = _sigmoid(r_pre + ba_ref[...])
        gi = _sigmoid(i_pre + bx_ref[...])
        log_a = r * decay_rate
        a = jnp.exp(log_a)
        w = -jnp.tanh(log_a) * (1.0 + a * a)
        u = (w * lax.rsqrt(jnp.maximum(w, F32_TINY))) * (gi * xc)
        if not reverse:
            row = lax.broadcasted_iota(jnp.int32, xc.shape, 0)
            u = jnp.where(row + tile * tt >= PADF, u, 0.0)
        for k in range(n_slabs):
            a_scr[bi * n_slabs + k] = a[:, k * LANES:(k + 1) * LANES]
            u_scr[bi * n_slabs + k] = u[:, k * LANES:(k + 1) * LANES]

    run = tt // SUBLANES
    chains = bsz * n_slabs
    steps = lambda j: pl.ds(j, SUBLANES, stride=run)

    assert run % SCAN_UNROLL == 0
    n_blocks = run // SCAN_UNROLL
    block_base = lambda i: ((n_blocks - 1 - i) if reverse else i) * SCAN_UNROLL
    offsets = range(SCAN_UNROLL - 1, -1, -1) if reverse else range(SCAN_UNROLL)

    def run_ends(i, carry):
        state, prod = list(carry[0]), list(carry[1])
        base = block_base(i)
        for o in offsets:
            for k in range(chains):
                aj = a_scr[k, steps(base + o), :]
                state[k] = aj * state[k] + u_scr[k, steps(base + o), :]
                prod[k] = prod[k] * aj
        return tuple(state), tuple(prod)

    zeros = tuple(jnp.zeros((SUBLANES, LANES), jnp.float32) for _ in range(chains))
    ones = tuple(z + 1.0 for z in zeros)
    end_state, end_prod = lax.fori_loop(0, n_blocks, run_ends, (zeros, ones))
    end_state, end_prod = jnp.stack(end_state), jnp.stack(end_prod)

    c = carry_ref[...]
    order = range(SUBLANES - 1, -1, -1) if reverse else range(SUBLANES)
    entering = [None] * SUBLANES
    for r in order:
        entering[r] = c
        c = end_prod[:, r:r + 1, :] * c + end_state[:, r:r + 1, :]
    carry_ref[...] = c
    entering = jnp.concatenate(entering, axis=1)

    def scan(i, state):
        state = list(state)
        base = block_base(i)
        for o in offsets:
            for k in range(chains):
                state[k] = a_scr[k, steps(base + o), :] * state[k] + u_scr[k, steps(base + o), :]
                u_scr[k, steps(base + o), :] = state[k]
        return tuple(state)

    lax.fori_loop(0, n_blocks, scan, tuple(entering[k] for k in range(chains)))

    for bi in range(bsz):
        hs = jnp.concatenate([u_scr[bi * n_slabs + k] for k in range(n_slabs)], axis=1)
        if reverse:
            gate = gate_ref[bi].astype(jnp.float32)
            out_ref[bi] = ((hf_ref[bi] + hs) * gate).astype(out_ref.dtype)
        else:
            out_ref[bi] = hs


def _lru_pass(xc, wgate, ba, bx, lam, *, reverse, hf=None, gate=None):
    bsz, tp, _ = xc.shape
    tt = TIME_TILE
    nt = tp // tt
    chains = bsz * (LRU_WIDTH // LANES)
    tsel = (lambda t: nt - 1 - t) if reverse else (lambda t: t)
    cur = pl.BlockSpec((bsz, tt, LRU_WIDTH), lambda t: (0, tsel(t), 0))
    in_specs = [cur] + [_const_spec(a) for a in (wgate, ba, bx, lam)]
    args = [xc, wgate, ba, bx, lam]
    if reverse:
        in_specs += [cur, cur]
        args += [hf, gate]
        out_dtype = jnp.bfloat16
    else:
        out_dtype = jnp.float32
    return pl.pallas_call(
        functools.partial(_lru_kernel, reverse=reverse, n_tiles=nt),
        grid=(nt,),
        in_specs=in_specs,
        out_specs=cur,
        out_shape=jax.ShapeDtypeStruct((bsz, tp, LRU_WIDTH), out_dtype),
        scratch_shapes=[pltpu.VMEM((chains, tt, LANES), jnp.float32),
                        pltpu.VMEM((chains, tt, LANES), jnp.float32),
                        pltpu.VMEM((chains, 1, LANES), jnp.float32)],
        compiler_params=pltpu.CompilerParams(
            dimension_semantics=("arbitrary",), vmem_limit_bytes=VMEM_LIMIT),
        name="lru_bwd" if reverse else "lru_fwd",
    )(*map(_operand, args))


def _gla_constants(reverse):
    c, w = CHUNK, SUBLANES
    t = np.arange(c)[:, None]
    s = np.arange(c)[None, :]
    gt, gs, pt, ps = t // w, s // w, t // 2, s // 2
    pair_code = 2 + (s % w) // 2
    if reverse:
        code = np.where((pt == ps) & (s > t), 1,
                        np.where((gt == gs) & (ps > pt), pair_code, np.where(gs > gt, w + gs, -1)))
        blocks = [s >= t, (t % 2 == 1) & (s == t - 1)]
    else:
        code = np.where((pt == ps) & (s <= t), t - s,
                        np.where((gt == gs) & (ps < pt), pair_code, np.where(gs < gt, w + gs, -1)))
        blocks = [s <= t, (t % 2 == 0) & (s == t + 1)]
    lmat = np.concatenate([blk.astype(np.float32) for blk in blocks], axis=0)
    lmat = np.concatenate([lmat, lmat], axis=1)
    code = np.tile(code.astype(np.int32), (1, GLA_HEADS))
    hmask = np.kron(np.eye(GLA_HEADS, dtype=np.float32), np.ones((c, GLA_DK), np.float32))
    return jnp.asarray(lmat, jnp.bfloat16), jnp.asarray(code), jnp.asarray(hmask, jnp.bfloat16)


def _gla_decays(g, lmat):
    ghi, glo = _split_bf16(g)
    return jnp.dot(lmat, jnp.concatenate([ghi, glo], axis=0), preferred_element_type=jnp.float32)


def _gla_scores(q, k, g, logdec, hmask, reverse):
    n_groups = CHUNK // SUBLANES
    pairs = SUBLANES // 2

    def head_stack(x):
        return jnp.concatenate([x.astype(jnp.bfloat16)] * GLA_HEADS, axis=0) * hmask

    def terms(lhs, rhs):
        return lax.dot_general(jnp.concatenate(lhs, axis=0).astype(jnp.bfloat16), head_stack(rhs),
                               (((1,), (1,)), ((), ())), preferred_element_type=jnp.float32)

    def facing_rows(first):
        return jnp.concatenate([jnp.broadcast_to(b[first + gi * SUBLANES:first + gi * SUBLANES + 1, :],
                                                 (SUBLANES, GLA_QK)) for gi in range(n_groups)], axis=0)

    b = logdec[0:CHUNK, :]

    step = q * jnp.exp2(g)
    p_pair = terms([step] if reverse else [q, step], k)

    k_pair = k * jnp.exp2(logdec[CHUNK:2 * CHUNK, :])
    pair_ids = range(1, pairs) if reverse else range(pairs - 1)
    lhs = [q * jnp.exp2(jnp.minimum(b - facing_rows(2 * p if reverse else 2 * p + 1), 0.0))
           for p in pair_ids]
    p_group = terms(lhs, k_pair)

    b_anchor = facing_rows(0 if reverse else SUBLANES - 1)
    groups = list(range(1, n_groups)) if reverse else list(range(n_groups - 1))
    row_sel = (lambda gi: slice(0, gi * SUBLANES)) if reverse else \
              (lambda gi: slice((gi + 1) * SUBLANES, CHUNK))
    anchor_row = (lambda gi: gi * SUBLANES) if reverse else (lambda gi: gi * SUBLANES + SUBLANES - 1)
    lhs = [q[row_sel(gi), :] * jnp.exp2(b[row_sel(gi), :] - b[anchor_row(gi):anchor_row(gi) + 1, :])
           for gi in groups]
    p_cross = terms(lhs, k * jnp.exp2(b_anchor - b))

    last = 0 if reverse else CHUNK - 1
    b_last = b[last:last + 1, :]
    qe = (q * jnp.exp2(b)).astype(jnp.bfloat16)
    ke = (k * jnp.exp2(b_last - b)).astype(jnp.bfloat16)
    return p_pair, p_group, p_cross, qe, ke, jnp.exp2(b_last)


def _gla_apply(p_pair, p_group, p_cross, qe, ke, e_last, vb, s_ref, code, reverse):
    n_groups = CHUNK // SUBLANES
    pairs = SUBLANES // 2
    groups = list(range(1, n_groups)) if reverse else list(range(n_groups - 1))
    scores = jnp.zeros((CHUNK, GLA_HEADS * CHUNK), jnp.float32)
    for i, d in enumerate([1] if reverse else [0, 1]):
        scores = jnp.where(code == d, p_pair[i * CHUNK:(i + 1) * CHUNK, :], scores)
    for i, p in enumerate(range(1, pairs) if reverse else range(pairs - 1)):
        scores = jnp.where(code == 2 + p, p_group[i * CHUNK:(i + 1) * CHUNK, :], scores)
    off = 0
    for gi in groups:
        rows = slice(0, gi * SUBLANES) if reverse else slice((gi + 1) * SUBLANES, CHUNK)
        n = rows.stop - rows.start
        part = jnp.where(code[rows, :] == SUBLANES + gi, p_cross[off:off + n, :], scores[rows, :])
        scores = jnp.concatenate([part, scores[n:, :]] if reverse else [scores[:CHUNK - n, :], part],
                                 axis=0)
        off += n
    sb = scores.astype(jnp.bfloat16)

    eye = (lax.broadcasted_iota(jnp.int32, (GLA_DK, GLA_DK), 0)
           == lax.broadcasted_iota(jnp.int32, (GLA_DK, GLA_DK), 1))
    zero_v = jnp.zeros((CHUNK, GLA_DV), jnp.bfloat16)
    zero_s = jnp.zeros((GLA_DK, GLA_DV), jnp.bfloat16)
    outs = []
    for pair in range(GLA_HEADS // 2):
        heads = (2 * pair, 2 * pair + 1)
        vh = [vb[:, hd * GLA_DV:(hd + 1) * GLA_DV] for hd in heads]
        st = [s_ref[hd] for hd in heads]
        sh = [s.astype(jnp.bfloat16) for s in st]
        lanes = slice(pair * 2 * CHUNK, (pair + 1) * 2 * CHUNK)
        lhs = jnp.concatenate([sb[:, lanes], qe[:, lanes]], axis=1)
        rhs = jnp.concatenate([jnp.concatenate([vh[0], zero_v], axis=1),
                               jnp.concatenate([zero_v, vh[1]], axis=1),
                               jnp.concatenate([sh[0], zero_s], axis=1),
                               jnp.concatenate([zero_s, sh[1]], axis=1)], axis=0)
        outs.append(jnp.dot(lhs, rhs, preferred_element_type=jnp.float32))
        for i, hd in enumerate(heads):
            ks = slice(hd * GLA_DK, (hd + 1) * GLA_DK)
            kv = lax.dot_general(ke[:, ks], vh[i], (((0,), (0,)), ((), ())),
                                 preferred_element_type=jnp.float32)
            e_col = jnp.sum(jnp.where(eye, e_last[:, ks], 0.0), axis=1, keepdims=True)
            s_ref[hd] = st[i] * e_col + kv
    return jnp.concatenate(outs, axis=1)


def _gla_kernel(*refs, reverse):
    if reverse:
        (qk_ref, v_ref, g_ref, lmat_ref, code_ref, hmask_ref, of_ref, go_ref, hn_ref,
         out_ref, s_ref) = refs
    else:
        qk_ref, v_ref, g_ref, lmat_ref, code_ref, hmask_ref, out_ref, s_ref = refs
    bsz, tt, _ = qk_ref.shape
    n_chunks = tt // CHUNK

    @pl.when(pl.program_id(0) == 0)
    def _():
        s_ref[...] = jnp.zeros_like(s_ref)

    lmat = lmat_ref[...]
    code = code_ref[...]
    hmask = hmask_ref[...]

    def chunk_body(j, carry):
        work = []
        for u in range(GLA_UNROLL):
            jj = j * GLA_UNROLL + u
            cidx = (n_chunks - 1 - jj) if reverse else jj
            rows = pl.ds(pl.multiple_of(cidx * CHUNK, CHUNK), CHUNK)
            work += [(bi, rows) for bi in range(bsz)]
        decs = [_gla_decays(g_ref[bi, rows, :], lmat) for bi, rows in work]
        terms = [_gla_scores(qk_ref[bi, rows, 0:GLA_QK], qk_ref[bi, rows, GLA_QK:2 * GLA_QK],
                             g_ref[bi, rows, :], dec, hmask, reverse)
                 for (bi, rows), dec in zip(work, decs)]
        for (bi, rows), term in zip(work, terms):
            o = _gla_apply(*term, v_ref[bi, rows, :].astype(jnp.bfloat16), s_ref.at[bi], code,
                           reverse)
            if reverse:
                o = o + of_ref[bi, rows, :]
                parts = []
                for hd in range(GLA_HEADS):
                    oh = o[:, hd * GLA_DV:(hd + 1) * GLA_DV]
                    parts.append(oh * _rms_scale(oh))
                y = jnp.concatenate(parts, axis=1) * hn_ref[...] * go_ref[bi, rows, :].astype(jnp.float32)
                out_ref[bi, rows, :] = y.astype(out_ref.dtype)
            else:
                out_ref[bi, rows, :] = o
        return carry

    lax.fori_loop(0, n_chunks // GLA_UNROLL, chunk_body, 0)


def _gla_pass(qk, v, gfb, *, reverse, o_f=None, go=None, head_norm=None):
    bsz, tp, _ = qk.shape
    tt = GLA_TIME_TILE
    nt = tp // tt
    tsel = (lambda t: nt - 1 - t) if reverse else (lambda t: t)
    blk = lambda w, c=0: pl.BlockSpec((bsz, tt, w), lambda t: (0, tsel(t), c))
    consts = _gla_constants(reverse)
    in_specs = [blk(512), blk(512), blk(GLA_QK, 1 if reverse else 0)]
    in_specs += [_const_spec(a) for a in consts]
    args = [qk, v, gfb, *consts]
    if reverse:
        in_specs += [blk(512), blk(512), _const_spec(head_norm)]
        args += [o_f, go, head_norm]
        out_dtype = jnp.bfloat16
    else:
        out_dtype = jnp.float32
    return pl.pallas_call(
        functools.partial(_gla_kernel, reverse=reverse),
        grid=(nt,),
        in_specs=in_specs,
        out_specs=blk(512),
        out_shape=jax.ShapeDtypeStruct((bsz, tp, GLA_WIDTH), out_dtype),
        scratch_shapes=[pltpu.VMEM((bsz, GLA_HEADS, GLA_DK, GLA_DV), jnp.float32)],
        compiler_params=pltpu.CompilerParams(
            dimension_semantics=("arbitrary",), vmem_limit_bytes=VMEM_LIMIT),
        name="gla_bwd" if reverse else "gla_fwd",
    )(*map(_operand, args))


def _token_rows_copy(buf_ref, out_ref, sem_ref, slot, lead_tile, first_token):
    tm = buf_ref.shape[1]
    if lead_tile:
        return pltpu.make_async_copy(buf_ref.at[slot, pl.ds(T0, tm - T0)],
                                     out_ref.at[pl.ds(first_token, tm - T0)], sem_ref.at[slot])
    return pltpu.make_async_copy(buf_ref.at[slot], out_ref.at[pl.ds(first_token, tm)],
                                 sem_ref.at[slot])


def _out_mlp_kernel(*refs, rows_per_batch, n_batch, per_batch, from_tokens, drop_lead):
    per_batch_grid = per_batch is not None
    if from_tokens:
        x_ref, lead_ref, *refs = refs
        h_tile = _token_tile(x_ref, lead_ref)
    else:
        h_tile, *refs = refs
    if drop_lead:
        *refs, buf_ref, sem_ref = refs
    (yl_ref, yg_ref, wo_ref, gpost_ref, gpre_ref, gpost2_ref,
     wup_ref, wdn_ref, out_ref, acc_ref) = refs
    tm = acc_ref.shape[0]
    tile = pl.program_id(0)
    if per_batch_grid:
        tile = tile * pl.num_programs(1) + pl.program_id(1)
    if drop_lead:
        n_steps = n_batch * per_batch
        slot = lax.rem(tile, 2)

        def wait_step(step, its_slot):
            was_lead = lax.rem(step, per_batch) == 0
            for lead_tile in (True, False):
                @pl.when(was_lead == lead_tile)
                def _():
                    _token_rows_copy(buf_ref, out_ref, sem_ref, its_slot, lead_tile, 0).wait()

        @pl.when(tile >= 2)
        def _():
            wait_step(tile - 2, slot)
        store_ref = buf_ref.at[slot]
    else:
        store_ref = out_ref
    n_split = ROW_SPLIT if tm >= ROW_SPLIT * MXU_ROWS else 1
    n = tm // n_split
    parts = [slice(i * n, (i + 1) * n) for i in range(n_split)]
    mix = [jnp.dot(yl_ref[p, :], wo_ref[0:LRU_WIDTH, :], preferred_element_type=jnp.float32)
           + jnp.dot(yg_ref[p, :], wo_ref[LRU_WIDTH:, :], preferred_element_type=jnp.float32)
           for p in parts]
    h1, xn = [], []
    for i, p in enumerate(parts):
        row = lax.broadcasted_iota(jnp.int32, (n, 1), 0) + (tile * tm + i * n)
        real = row >= PADF
        for b in range(1, n_batch):
            real = real & ((row < b * rows_per_batch) | (row >= b * rows_per_batch + PADF))
        h1.append(h_tile[p, :] + jnp.where(real, mix[i] * _rms_scale(mix[i]) * gpost_ref[...], 0.0))
        xn.append((h1[i] * _rms_scale(h1[i]) * gpre_ref[...]).astype(jnp.bfloat16))
    for c in range(D_FF // FF_TILE):
        cs = slice(c * FF_TILE, (c + 1) * FF_TILE)
        for i, p in enumerate(parts):
            up = jnp.dot(xn[i], wup_ref[:, cs], preferred_element_type=jnp.float32)
            act = jnp.square(jnp.maximum(up, 0.0)).astype(jnp.bfloat16)
            part = jnp.dot(act, wdn_ref[cs, :], preferred_element_type=jnp.float32)
            if c == 0:
                acc_ref[p, :] = part
            else:
                acc_ref[p, :] += part
    for i, p in enumerate(parts):
        ff = acc_ref[p, :]
        store_ref[p, :] = h1[i] + ff * _rms_scale(ff) * gpost2_ref[...]
    if drop_lead:
        b, j = pl.program_id(0), pl.program_id(1)
        seq = rows_per_batch - T0
        for lead_tile in (True, False):
            @pl.when((j == 0) == lead_tile)
            def _():
                first = b * seq + (0 if lead_tile else j * tm - T0)
                _token_rows_copy(buf_ref, out_ref, sem_ref, slot, lead_tile,
                                 pl.multiple_of(first, SUBLANES)).start()

        @pl.when(tile == n_steps - 1)
        def _():
            wait_step(tile, slot)
            if n_steps >= 2:
                wait_step(tile - 1, 1 - slot)


def _out_mlp(h, y_lru, y_gla, wo, g_post, g_pre, g_post2, w_up, w_dn, rows_per_batch,
             drop_lead, lead=None):
    rows = y_lru.shape[0]
    n_batch = rows // rows_per_batch
    consts = (wo, g_post, g_pre, g_post2, w_up, w_dn)
    tm = ROW_TILE
    per_batch = None
    scratch = [pltpu.VMEM((tm, D_MODEL), jnp.float32)]
    if lead is not None:
        assert not drop_lead
        grid, (h_spec, _, _), row = _token_specs(n_batch, rows_per_batch - T0, tm)
        per_batch = grid[1]
        h_specs, h_args = [h_spec, _const_spec(lead)], [h, lead]
        out_spec = row(D_MODEL)
        out_rows = rows
    elif drop_lead:
        per_batch = rows_per_batch // tm
        grid = (n_batch, per_batch)
        row = lambda w: pl.BlockSpec((tm, w), lambda b, j: (b * per_batch + j, 0))
        out_spec = pl.BlockSpec(memory_space=pl.ANY)
        out_rows = rows - n_batch * T0
        h_specs, h_args = [row(D_MODEL)], [h]
        scratch += [pltpu.VMEM((2, tm, D_MODEL), jnp.float32), pltpu.SemaphoreType.DMA((2,))]
    else:
        grid = (rows // tm,)
        row = lambda w: pl.BlockSpec((tm, w), lambda i: (i, 0))
        out_spec = row(D_MODEL)
        out_rows = rows
        h_specs, h_args = [row(D_MODEL)], [h]
    return pl.pallas_call(
        functools.partial(_out_mlp_kernel, rows_per_batch=rows_per_batch, n_batch=n_batch,
                          per_batch=per_batch, from_tokens=lead is not None, drop_lead=drop_lead),
        grid=grid,
        in_specs=h_specs + [row(512), row(512)] + [_const_spec(a) for a in consts],
        out_specs=out_spec,
        out_shape=jax.ShapeDtypeStruct((out_rows, D_MODEL), jnp.float32),
        scratch_shapes=scratch,
        compiler_params=pltpu.CompilerParams(
            dimension_semantics=("arbitrary",) * len(grid), vmem_limit_bytes=VMEM_LIMIT),
        name="out_mlp",
    )(*h_args, y_lru, y_gla, *map(_operand, consts))


def _block_diag_gates(wa, wx):
    per_half = LRU_HEADS // 2
    eye = jnp.eye(per_half, dtype=wa.dtype)

    def bd(w):
        w5 = w.reshape(-1, 2, per_half, LRU_HEAD_DIM, LRU_HEAD_DIM)
        return jnp.einsum('nhjil,jk->nhjikl', w5, eye).reshape(-1, 2, LRU_WIDTH // 2, LRU_WIDTH // 2)
    return jnp.concatenate([bd(wa), bd(wx)], axis=3).astype(jnp.bfloat16)


def kernel(x, meta_tokens, norm_mix_pre, norm_mix_post, norm_mlp_pre, norm_mlp_post,
           w_in, conv_w, conv_b,
           lru_wa_f, lru_ba_f, lru_wx_f, lru_bx_f, lru_lambda_f,
           lru_wa_b, lru_ba_b, lru_wx_b, lru_bx_b, lru_lambda_b,
           gla_wg_f, gla_bg_f, gla_wg_b, gla_bg_b, gla_head_norm,
           w_out, w_mlp_up, w_mlp_down):
    bsz, seq, d = x.shape
    depth = w_in.shape[0]
    tp = seq + T0
    assert d == D_MODEL and tp % TIME_TILE == 0 and tp % GLA_TIME_TILE == 0
    assert (bsz * tp) % ROW_TILE == 0
    assert ROW_TILE > T0 and tp % ROW_TILE == 0
    lead = jnp.concatenate([jnp.zeros((PADF, d), x.dtype), meta_tokens.astype(x.dtype)], axis=0)
    h = x.reshape(bsz * seq, d)
    in_place = depth > 1
    if not in_place:
        h = jnp.concatenate([jnp.broadcast_to(lead[None], (bsz, T0, d)), x], axis=1).reshape(-1, d)
    vec = lambda a: a.reshape(depth, 1, -1)
    bf = lambda a: a.astype(jnp.bfloat16)

    w_main = bf(w_in)
    w_zg = bf(jnp.tile(w_in[:, :, W_IN_MAIN:], (1, 1, 3)))
    wg = jnp.zeros((depth, 2 * GLA_RANK, 2 * GLA_QK), jnp.float32)
    wg = wg.at[:, :GLA_RANK, :GLA_QK].set(gla_wg_f).at[:, GLA_RANK:, GLA_QK:].set(gla_wg_b)
    wg_hi, wg_lo = _split_bf16(wg)
    wg3 = jnp.concatenate([wg_hi, wg_lo, wg_hi], axis=1)
    bg = vec(jnp.concatenate([gla_bg_f, gla_bg_b], axis=1))
    gates_f = _block_diag_gates(lru_wa_f, lru_wx_f)
    gates_b = _block_diag_gates(lru_wa_b, lru_wx_b)
    params = dict(
        g_pre=vec(norm_mix_pre), w_main=w_main, w_zg=w_zg, wg3=wg3, bg=bg,
        conv_w=conv_w, conv_b=vec(conv_b),
        lru_f=(gates_f, vec(lru_ba_f), vec(lru_bx_f), vec(lru_lambda_f)),
        lru_b=(gates_b, vec(lru_ba_b), vec(lru_bx_b), vec(lru_lambda_b)),
        head_norm=vec(gla_head_norm),
        out=(bf(w_out), vec(norm_mix_post),
             vec(norm_mlp_pre), vec(norm_mlp_post), bf(w_mlp_up), bf(w_mlp_down)))

    for l in range(depth):
        at = lambda a: _Layer(a, l)
        p = jax.tree.map(at, params)
        tokens = dict(lead=lead) if (l == 0 and in_place) else {}
        outs = _in_proj(h, p['g_pre'], p['w_main'], p['w_zg'], p['wg3'], p['bg'],
                        p['conv_w'], p['conv_b'], n_batch=bsz, **tokens)
        xc3, gate3, qk3, v3, go3, gfb3 = (a.reshape(bsz, tp, a.shape[-1]) for a in outs)

        h_f = _lru_pass(xc3, *p['lru_f'], reverse=False)
        y_lru = _lru_pass(xc3, *p['lru_b'], reverse=True, hf=h_f, gate=gate3)
        o_f = _gla_pass(qk3, v3, gfb3, reverse=False)
        y_gla = _gla_pass(qk3, v3, gfb3, reverse=True, o_f=o_f, go=go3, head_norm=p['head_norm'])

        h = _out_mlp(h, y_lru.reshape(bsz * tp, -1), y_gla.reshape(bsz * tp, -1), *p['out'],
                     tp, drop_lead=(l == depth - 1), **tokens)
    return h.reshape(bsz, seq, d)
```

```python
import functools

import numpy as np
import jax
import jax.numpy as jnp
from jax import lax
from jax.experimental import pallas as pl
from jax.experimental.pallas import tpu as pltpu

D_MODEL = 1024
N_META = 16
CHUNK = 64
LRU_WIDTH = 512
LRU_HEADS = 8
LRU_HEAD_DIM = 64
LRU_C = 8.0
GLA_WIDTH = 512
GLA_HEADS = 4
GLA_DV = 128
GLA_DK = 64
GLA_QK = GLA_HEADS * GLA_DK
GLA_RANK = 16
GLA_GATE_NORM = 16.0
W_IN_MAIN = 2 * LRU_WIDTH + 2 * GLA_QK + 2 * GLA_WIDTH
D_FF = 4096
EPS = 1e-6
LOG2_E = 1.4426950408889634
F32_TINY = float(np.finfo(np.float32).tiny)

LEAD_CHUNKS = 4
T0 = LEAD_CHUNKS * CHUNK
PADF = T0 - N_META
ROW_TILE = 768
TIME_TILE = 1056
SCAN_UNROLL = 12
GLA_TIME_TILE = 768
GLA_UNROLL = 3
IN_SPLIT = 3
ROW_SPLIT = 2
FF_TILE = 1024
SUBLANES = 8
LANES = 128
MXU_ROWS = 256
VMEM_LIMIT = 56 * 1024 * 1024


class _Layer:
    def __init__(self, stacked, layer):
        self.stacked, self.layer, self.shape = stacked, layer, tuple(stacked.shape[1:])


def _operand(a):
    return a.stacked if isinstance(a, _Layer) else a


def _const_spec(a):
    nd = len(a.shape)
    if isinstance(a, _Layer):
        layer = a.layer
        return pl.BlockSpec((None,) + a.shape, lambda *_: (layer,) + (0,) * nd,
                            pipeline_mode=pl.Buffered(1))
    return pl.BlockSpec(a.shape, lambda *_: (0,) * nd, pipeline_mode=pl.Buffered(1))


def _rms_scale(x):
    return lax.rsqrt(jnp.mean(x * x, axis=-1, keepdims=True) + EPS)


def _softplus(x):
    return jnp.maximum(x, 0.0) + jnp.log1p(jnp.exp(-jnp.abs(x)))


def _split_bf16(x):
    hi = x.astype(jnp.bfloat16)
    lo = (x - hi.astype(jnp.float32)).astype(jnp.bfloat16)
    return hi, lo


def _token_tile(x_ref, lead_ref):
    tm = x_ref.shape[0]
    fetched = x_ref[...]
    first = jnp.concatenate([lead_ref[...], fetched[0:tm - T0, :]], axis=0)
    return jnp.where(pl.program_id(1) == 0, first, fetched)


def _in_proj_kernel(*refs, n_tiles, from_tokens):
    if from_tokens:
        h_ref, hp_ref, hn_ref, lead_ref, *refs = refs
    else:
        h_ref, hp_ref, hn_ref, *refs = refs
    (g_ref, w_ref, wzg_ref, wg_ref, bg_ref, cw_ref, cb_ref,
     xc_ref, gate_ref, qk_ref, v_ref, go_ref, gfb_ref, xs_ref) = refs
    tm = h_ref.shape[0]
    normed = lambda x: (x * _rms_scale(x) * g_ref[...]).astype(jnp.bfloat16)
    if from_tokens:
        h_tile = _token_tile(h_ref, lead_ref)
        no_next = pl.program_id(1) == n_tiles - 1
    else:
        h_tile = h_ref[...]
        no_next = pl.program_id(0) == n_tiles - 1
    halo = jnp.concatenate([hp_ref[...], hn_ref[...]], axis=0)
    xn = normed(h_tile)
    xn_ext = jnp.concatenate([xn, normed(halo)], axis=0)

    zg = jnp.dot(xn, wzg_ref[...], preferred_element_type=jnp.float32)
    zx = jnp.dot(xn_ext, w_ref[:, 0:LRU_WIDTH], preferred_element_type=jnp.float32)
    zhi = zg.astype(jnp.bfloat16)
    zlo = (zg - zhi.astype(jnp.float32)).astype(jnp.bfloat16)
    lane = lax.broadcasted_iota(jnp.int32, zg.shape, 1)
    lhs = jnp.where(lane < 4 * GLA_RANK, zhi, zlo)
    pre = jnp.dot(lhs, wg_ref[...], preferred_element_type=jnp.float32) + bg_ref[...]
    n = tm // IN_SPLIT
    parts = [slice(i * n, (i + 1) * n) for i in range(IN_SPLIT)]
    zs = [jnp.dot(xn[p, :], w_ref[:, LRU_WIDTH:W_IN_MAIN], preferred_element_type=jnp.float32)
          for p in parts]

    xs_ref[0:SUBLANES, :] = zx[tm:tm + SUBLANES, :]
    xs_ref[SUBLANES:SUBLANES + tm, :] = zx[0:tm, :]
    xs_ref[SUBLANES + tm:2 * SUBLANES + tm, :] = jnp.where(
        no_next, 0.0, zx[tm + SUBLANES:tm + 2 * SUBLANES, :])
    cw = cw_ref[...]
    xc_ref[...] = (cw[0:1, :] * xs_ref[SUBLANES - 2:SUBLANES - 2 + tm, :]
                   + cw[1:2, :] * xs_ref[SUBLANES - 1:SUBLANES - 1 + tm, :]
                   + cw[2:3, :] * xs_ref[SUBLANES:SUBLANES + tm, :]
                   + cw[3:4, :] * xs_ref[SUBLANES + 1:SUBLANES + 1 + tm, :]
                   + cb_ref[...])
    logsig = jnp.minimum(pre, 0.0) - jnp.log(1.0 + jnp.exp(-jnp.abs(pre)))
    gfb_ref[...] = logsig * (LOG2_E / GLA_GATE_NORM)

    for p, z in zip(parts, zs):
        gt = z[:, 0:512]
        gelu = 0.5 * gt * (1.0 + jnp.tanh(0.7978845608028654 * (gt + 0.044715 * gt * gt * gt)))
        gate_ref[p, :] = gelu.astype(jnp.bfloat16)
        q = z[:, 512:768] * (GLA_DK ** -0.5)
        qk_ref[p, :] = jnp.concatenate([q, z[:, 768:1024]], axis=1).astype(jnp.bfloat16)
        v_ref[p, :] = z[:, 1024:1536].astype(jnp.bfloat16)
        half = 0.5 * z[:, 1536:2048]
        go_ref[p, :] = (half * jnp.tanh(half) + half).astype(jnp.bfloat16)


def _token_specs(n_batch, seq, tm):
    per_batch = (seq + T0) // tm
    total = n_batch * seq
    start = lambda b, j: b * seq + j * tm - T0
    window = lambda rows, at: pl.BlockSpec((pl.Element(rows), pl.Element(D_MODEL)),
                                           lambda b, j: (pl.multiple_of(
                                               jnp.clip(at(b, j), 0, total - rows), SUBLANES), 0))
    specs = [window(tm, lambda b, j: b * seq + jnp.maximum(j * tm - T0, 0)),
             window(SUBLANES, lambda b, j: start(b, j) - SUBLANES),
             window(SUBLANES, lambda b, j: start(b, j) + tm)]
    return (n_batch, per_batch), specs, (lambda w: pl.BlockSpec((tm, w), lambda b, j: (b * per_batch + j, 0)))


def _in_proj(h, g, w_main, w_zg, wg3, bg, conv_w, conv_b, lead=None, n_batch=None):
    tm = ROW_TILE
    consts = (g, w_main, w_zg, wg3, bg, conv_w, conv_b)
    if lead is None:
        rows = h.shape[0]
        grid = (rows // tm,)
        g8 = tm // SUBLANES
        row = lambda w: pl.BlockSpec((tm, w), lambda i: (i, 0))
        h_specs = [row(D_MODEL),
                   pl.BlockSpec((SUBLANES, D_MODEL), lambda i: (jnp.maximum(i * g8 - 1, 0), 0)),
                   pl.BlockSpec((SUBLANES, D_MODEL),
                                lambda i: (jnp.minimum((i + 1) * g8, rows // SUBLANES - 1), 0))]
        h_args = [h, h, h]
    else:
        seq = h.shape[0] // n_batch
        rows = n_batch * (seq + T0)
        grid, h_specs, row = _token_specs(n_batch, seq, tm)
        h_specs.append(_const_spec(lead))
        h_args = [h, h, h, lead]
    outs = [(512, jnp.float32)] + [(512, jnp.bfloat16)] * 4 + [(512, jnp.float32)]
    return pl.pallas_call(
        functools.partial(_in_proj_kernel, n_tiles=grid[-1], from_tokens=lead is not None),
        grid=grid,
        in_specs=h_specs + [_const_spec(a) for a in consts],
        out_specs=[row(w) for w, _ in outs],
        out_shape=[jax.ShapeDtypeStruct((rows, w), dt) for w, dt in outs],
        scratch_shapes=[pltpu.VMEM((tm + 2 * SUBLANES, LRU_WIDTH), jnp.float32)],
        compiler_params=pltpu.CompilerParams(
            dimension_semantics=("arbitrary",) * len(grid), vmem_limit_bytes=VMEM_LIMIT),
        name="in_proj",
    )(*h_args, *map(_operand, consts))


def _lru_kernel(*refs, reverse, n_tiles):
    if reverse:
        (xc_ref, wg_ref, ba_ref, bx_ref, lam_ref, hf_ref, gate_ref,
         out_ref, a_scr, u_scr, carry_ref) = refs
    else:
        xc_ref, wg_ref, ba_ref, bx_ref, lam_ref, out_ref, a_scr, u_scr, carry_ref = refs
    bsz, tt, _ = xc_ref.shape
    step = pl.program_id(0)
    tile = (n_tiles - 1 - step) if reverse else step

    @pl.when(step == 0)
    def _():
        carry_ref[...] = jnp.zeros_like(carry_ref)

    n_slabs = LRU_WIDTH // LANES
    half = LRU_WIDTH // 2
    half_rate = (0.5 * LRU_C) * _softplus(-lam_ref[...])
    for bi in range(bsz):
        xc = xc_ref[bi]
        xcb = xc.astype(jnp.bfloat16)
        p0 = jnp.dot(xcb[:, :half], wg_ref[0], preferred_element_type=jnp.float32)
        p1 = jnp.dot(xcb[:, half:], wg_ref[1], preferred_element_type=jnp.float32)
        r_pre = jnp.concatenate([p0[:, :half], p1[:, :half]], axis=1)
        i_pre = jnp.concatenate([p0[:, half:], p1[:, half:]], axis=1)
        tr = jnp.tanh(r_pre + ba_ref[...])
        gi = 0.5 * jnp.tanh(i_pre + bx_ref[...]) + 0.5
        neg_log_a = tr * half_rate + half_rate
        a = jnp.exp(-neg_log_a)
        w = jnp.tanh(neg_log_a) * (1.0 + a * a)
        u = (w * lax.rsqrt(jnp.maximum(w, F32_TINY))) * (gi * xc)
        if not reverse:
            row = lax.broadcasted_iota(jnp.int32, xc.shape, 0)
            u = jnp.where(row + tile * tt >= PADF, u, 0.0)
        for k in range(n_slabs):
            a_scr[bi * n_slabs + k] = a[:, k * LANES:(k + 1) * LANES]
            u_scr[bi * n_slabs + k] = u[:, k * LANES:(k + 1) * LANES]

    run = tt // SUBLANES
    chains = bsz * n_slabs
    steps = lambda j: pl.ds(j, SUBLANES, stride=run)

    assert run % SCAN_UNROLL == 0
    n_blocks = run // SCAN_UNROLL
    block_base = lambda i: ((n_blocks - 1 - i) if reverse else i) * SCAN_UNROLL
    offsets = range(SCAN_UNROLL - 1, -1, -1) if reverse else range(SCAN_UNROLL)

    def run_ends(i, carry):
        state, prod = list(carry[0]), list(carry[1])
        base = block_base(i)
        for o in offsets:
            for k in range(chains):
                aj = a_scr[k, steps(base + o), :]
                state[k] = aj * state[k] + u_scr[k, steps(base + o), :]
                prod[k] = prod[k] * aj
        return tuple(state), tuple(prod)

    zeros = tuple(jnp.zeros((SUBLANES, LANES), jnp.float32) for _ in range(chains))
    ones = tuple(z + 1.0 for z in zeros)
    end_state, end_prod = lax.fori_loop(0, n_blocks, run_ends, (zeros, ones))
    end_state, end_prod = jnp.stack(end_state), jnp.stack(end_prod)

    c = carry_ref[...]
    order = range(SUBLANES - 1, -1, -1) if reverse else range(SUBLANES)
    entering = [None] * SUBLANES
    for r in order:
        entering[r] = c
        c = end_prod[:, r:r + 1, :] * c + end_state[:, r:r + 1, :]
    carry_ref[...] = c
    entering = jnp.concatenate(entering, axis=1)

    def scan(i, state):
        state = list(state)
        base = block_base(i)
        for o in offsets:
            for k in range(chains):
                state[k] = a_scr[k, steps(base + o), :] * state[k] + u_scr[k, steps(base + o), :]
                u_scr[k, steps(base + o), :] = state[k]
        return tuple(state)

    lax.fori_loop(0, n_blocks, scan, tuple(entering[k] for k in range(chains)))

    for bi in range(bsz):
        hs = jnp.concatenate([u_scr[bi * n_slabs + k] for k in range(n_slabs)], axis=1)
        if reverse:
            gate = gate_ref[bi].astype(jnp.float32)
            out_ref[bi] = ((hf_ref[bi] + hs) * gate).astype(out_ref.dtype)
        else:
            out_ref[bi] = hs


def _lru_pass(xc, wgate, ba, bx, lam, *, reverse, hf=None, gate=None):
    bsz, tp, _ = xc.shape
    tt = TIME_TILE
    nt = tp // tt
    chains = bsz * (LRU_WIDTH // LANES)
    tsel = (lambda t: nt - 1 - t) if reverse else (lambda t: t)
    cur = pl.BlockSpec((bsz, tt, LRU_WIDTH), lambda t: (0, tsel(t), 0))
    in_specs = [cur] + [_const_spec(a) for a in (wgate, ba, bx, lam)]
    args = [xc, wgate, ba, bx, lam]
    if reverse:
        in_specs += [cur, cur]
        args += [hf, gate]
        out_dtype = jnp.bfloat16
    else:
        out_dtype = jnp.float32
    return pl.pallas_call(
        functools.partial(_lru_kernel, reverse=reverse, n_tiles=nt),
        grid=(nt,),
        in_specs=in_specs,
        out_specs=cur,
        out_shape=jax.ShapeDtypeStruct((bsz, tp, LRU_WIDTH), out_dtype),
        scratch_shapes=[pltpu.VMEM((chains, tt, LANES), jnp.float32),
                        pltpu.VMEM((chains, tt, LANES), jnp.float32),
                        pltpu.VMEM((chains, 1, LANES), jnp.float32)],
        compiler_params=pltpu.CompilerParams(
            dimension_semantics=("arbitrary",), vmem_limit_bytes=VMEM_LIMIT),
        name="lru_bwd" if reverse else "lru_fwd",
    )(*map(_operand, args))


def _gla_constants(reverse):
    c, w = CHUNK, SUBLANES
    t = np.arange(c)[:, None]
    s = np.arange(c)[None, :]
    gt, gs, pt, ps = t // w, s // w, t // 2, s // 2
    pair_code = 2 + (s % w) // 2
    if reverse:
        code = np.where((pt == ps) & (s > t), 1,
                        np.where((gt == gs) & (ps > pt), pair_code, np.where(gs > gt, w + gs, -1)))
        blocks = [s >= t, (t % 2 == 1) & (s == t - 1)]
    else:
        code = np.where((pt == ps) & (s <= t), t - s,
                        np.where((gt == gs) & (ps < pt), pair_code, np.where(gs < gt, w + gs, -1)))
        blocks = [s <= t, (t % 2 == 0) & (s == t + 1)]
    lmat = np.concatenate([blk.astype(np.float32) for blk in blocks], axis=0)
    lmat = np.concatenate([lmat, lmat], axis=1)
    code = np.tile(code.astype(np.int32), (1, GLA_HEADS))
    hmask = np.kron(np.eye(GLA_HEADS, dtype=np.float32), np.ones((c, GLA_DK), np.float32))
    return jnp.asarray(lmat, jnp.bfloat16), jnp.asarray(code), jnp.asarray(hmask, jnp.bfloat16)


def _gla_decays(g, lmat):
    ghi, glo = _split_bf16(g)
    return jnp.dot(lmat, jnp.concatenate([ghi, glo], axis=0), preferred_element_type=jnp.float32)


def _gla_scores(q, k, g, logdec, hmask, reverse):
    n_groups = CHUNK // SUBLANES
    pairs = SUBLANES // 2

    def head_stack(x):
        return jnp.concatenate([x.astype(jnp.bfloat16)] * GLA_HEADS, axis=0) * hmask

    def terms(lhs, rhs):
        return lax.dot_general(jnp.concatenate(lhs, axis=0).astype(jnp.bfloat16), head_stack(rhs),
                               (((1,), (1,)), ((), ())), preferred_element_type=jnp.float32)

    def facing_rows(first):
        return jnp.concatenate([jnp.broadcast_to(b[first + gi * SUBLANES:first + gi * SUBLANES + 1, :],
                                                 (SUBLANES, GLA_QK)) for gi in range(n_groups)], axis=0)

    b = logdec[0:CHUNK, :]

    step = q * jnp.exp2(g)
    p_pair = terms([step] if reverse else [q, step], k)

    k_pair = k * jnp.exp2(logdec[CHUNK:2 * CHUNK, :])
    pair_ids = range(1, pairs) if reverse else range(pairs - 1)
    lhs = [q * jnp.exp2(jnp.minimum(b - facing_rows(2 * p if reverse else 2 * p + 1), 0.0))
           for p in pair_ids]
    p_group = terms(lhs, k_pair)

    b_anchor = facing_rows(0 if reverse else SUBLANES - 1)
    groups = list(range(1, n_groups)) if reverse else list(range(n_groups - 1))
    row_sel = (lambda gi: slice(0, gi * SUBLANES)) if reverse else \
              (lambda gi: slice((gi + 1) * SUBLANES, CHUNK))
    anchor_row = (lambda gi: gi * SUBLANES) if reverse else (lambda gi: gi * SUBLANES + SUBLANES - 1)
    lhs = [q[row_sel(gi), :] * jnp.exp2(b[row_sel(gi), :] - b[anchor_row(gi):anchor_row(gi) + 1, :])
           for gi in groups]
    p_cross = terms(lhs, k * jnp.exp2(b_anchor - b))

    last = 0 if reverse else CHUNK - 1
    b_last = b[last:last + 1, :]
    qe = (q * jnp.exp2(b)).astype(jnp.bfloat16)
    ke = (k * jnp.exp2(b_last - b)).astype(jnp.bfloat16)
    return p_pair, p_group, p_cross, qe, ke, jnp.exp2(b_last)


def _gla_apply(p_pair, p_group, p_cross, qe, ke, e_last, vb, s_ref, code, reverse):
    n_groups = CHUNK // SUBLANES
    pairs = SUBLANES // 2
    groups = list(range(1, n_groups)) if reverse else list(range(n_groups - 1))
    scores = jnp.zeros((CHUNK, GLA_HEADS * CHUNK), jnp.float32)
    for i, d in enumerate([1] if reverse else [0, 1]):
        scores = jnp.where(code == d, p_pair[i * CHUNK:(i + 1) * CHUNK, :], scores)
    for i, p in enumerate(range(1, pairs) if reverse else range(pairs - 1)):
        scores = jnp.where(code == 2 + p, p_group[i * CHUNK:(i + 1) * CHUNK, :], scores)
    off = 0
    for gi in groups:
        rows = slice(0, gi * SUBLANES) if reverse else slice((gi + 1) * SUBLANES, CHUNK)
        n = rows.stop - rows.start
        part = jnp.where(code[rows, :] == SUBLANES + gi, p_cross[off:off + n, :], scores[rows, :])
        scores = jnp.concatenate([part, scores[n:, :]] if reverse else [scores[:CHUNK - n, :], part],
                                 axis=0)
        off += n
    sb = scores.astype(jnp.bfloat16)

    eye = (lax.broadcasted_iota(jnp.int32, (GLA_DK, GLA_DK), 0)
           == lax.broadcasted_iota(jnp.int32, (GLA_DK, GLA_DK), 1))
    zero_v = jnp.zeros((CHUNK, GLA_DV), jnp.bfloat16)
    zero_s = jnp.zeros((GLA_DK, GLA_DV), jnp.bfloat16)
    outs = []
    for pair in range(GLA_HEADS // 2):
        heads = (2 * pair, 2 * pair + 1)
        vh = [vb[:, hd * GLA_DV:(hd + 1) * GLA_DV] for hd in heads]
        st = [s_ref[hd] for hd in heads]
        sh = [s.astype(jnp.bfloat16) for s in st]
        lanes = slice(pair * 2 * CHUNK, (pair + 1) * 2 * CHUNK)
        lhs = jnp.concatenate([sb[:, lanes], qe[:, lanes]], axis=1)
        rhs = jnp.concatenate([jnp.concatenate([vh[0], zero_v], axis=1),
                               jnp.concatenate([zero_v, vh[1]], axis=1),
                               jnp.concatenate([sh[0], zero_s], axis=1),
                               jnp.concatenate([zero_s, sh[1]], axis=1)], axis=0)
        outs.append(jnp.dot(lhs, rhs, preferred_element_type=jnp.float32))
        for i, hd in enumerate(heads):
            ks = slice(hd * GLA_DK, (hd + 1) * GLA_DK)
            kv = lax.dot_general(ke[:, ks], vh[i], (((0,), (0,)), ((), ())),
                                 preferred_element_type=jnp.float32)
            e_col = jnp.sum(jnp.where(eye, e_last[:, ks], 0.0), axis=1, keepdims=True)
            s_ref[hd] = st[i] * e_col + kv
    return jnp.concatenate(outs, axis=1)


def _gla_kernel(*refs, reverse):
    if reverse:
        (qk_ref, v_ref, g_ref, lmat_ref, code_ref, hmask_ref, of_ref, go_ref, hn_ref,
         out_ref, s_ref) = refs
    else:
        qk_ref, v_ref, g_ref, lmat_ref, code_ref, hmask_ref, out_ref, s_ref = refs
    bsz, tt, _ = qk_ref.shape
    n_chunks = tt // CHUNK

    @pl.when(pl.program_id(0) == 0)
    def _():
        s_ref[...] = jnp.zeros_like(s_ref)

    lmat = lmat_ref[...]
    code = code_ref[...]
    hmask = hmask_ref[...]

    def chunk_body(j, carry):
        work = []
        for u in range(GLA_UNROLL):
            jj = j * GLA_UNROLL + u
            cidx = (n_chunks - 1 - jj) if reverse else jj
            rows = pl.ds(pl.multiple_of(cidx * CHUNK, CHUNK), CHUNK)
            work += [(bi, rows) for bi in range(bsz)]
        decs = [_gla_decays(g_ref[bi, rows, :], lmat) for bi, rows in work]
        terms = [_gla_scores(qk_ref[bi, rows, 0:GLA_QK], qk_ref[bi, rows, GLA_QK:2 * GLA_QK],
                             g_ref[bi, rows, :], dec, hmask, reverse)
                 for (bi, rows), dec in zip(work, decs)]
        for (bi, rows), term in zip(work, terms):
            o = _gla_apply(*term, v_ref[bi, rows, :].astype(jnp.bfloat16), s_ref.at[bi], code,
                           reverse)
            if reverse:
                o = o + of_ref[bi, rows, :]
                parts = []
                for hd in range(GLA_HEADS):
                    oh = o[:, hd * GLA_DV:(hd + 1) * GLA_DV]
                    parts.append(oh * _rms_scale(oh))
                y = jnp.concatenate(parts, axis=1) * hn_ref[...] * go_ref[bi, rows, :].astype(jnp.float32)
                out_ref[bi, rows, :] = y.astype(out_ref.dtype)
            else:
                out_ref[bi, rows, :] = o
        return carry

    lax.fori_loop(0, n_chunks // GLA_UNROLL, chunk_body, 0)


def _gla_pass(qk, v, gfb, *, reverse, o_f=None, go=None, head_norm=None):
    bsz, tp, _ = qk.shape
    tt = GLA_TIME_TILE
    nt = tp // tt
    tsel = (lambda t: nt - 1 - t) if reverse else (lambda t: t)
    blk = lambda w, c=0: pl.BlockSpec((bsz, tt, w), lambda t: (0, tsel(t), c))
    consts = _gla_constants(reverse)
    in_specs = [blk(512), blk(512), blk(GLA_QK, 1 if reverse else 0)]
    in_specs += [_const_spec(a) for a in consts]
    args = [qk, v, gfb, *consts]
    if reverse:
        in_specs += [blk(512), blk(512), _const_spec(head_norm)]
        args += [o_f, go, head_norm]
        out_dtype = jnp.bfloat16
    else:
        out_dtype = jnp.float32
    return pl.pallas_call(
        functools.partial(_gla_kernel, reverse=reverse),
        grid=(nt,),
        in_specs=in_specs,
        out_specs=blk(512),
        out_shape=jax.ShapeDtypeStruct((bsz, tp, GLA_WIDTH), out_dtype),
        scratch_shapes=[pltpu.VMEM((bsz, GLA_HEADS, GLA_DK, GLA_DV), jnp.float32)],
        compiler_params=pltpu.CompilerParams(
            dimension_semantics=("arbitrary",), vmem_limit_bytes=VMEM_LIMIT),
        name="gla_bwd" if reverse else "gla_fwd",
    )(*map(_operand, args))


def _token_rows_copy(buf_ref, out_ref, sem_ref, slot, lead_tile, first_token):
    tm = buf_ref.shape[1]
    if lead_tile:
        return pltpu.make_async_copy(buf_ref.at[slot, pl.ds(T0, tm - T0)],
                                     out_ref.at[pl.ds(first_token, tm - T0)], sem_ref.at[slot])
    return pltpu.make_async_copy(buf_ref.at[slot], out_ref.at[pl.ds(first_token, tm)],
                                 sem_ref.at[slot])


def _out_mlp_kernel(*refs, rows_per_batch, n_batch, per_batch, from_tokens, drop_lead):
    per_batch_grid = per_batch is not None
    if from_tokens:
        x_ref, lead_ref, *refs = refs
        h_tile = _token_tile(x_ref, lead_ref)
    else:
        h_tile, *refs = refs
    if drop_lead:
        *refs, buf_ref, sem_ref = refs
    (yl_ref, yg_ref, wo_ref, gpost_ref, gpre_ref, gpost2_ref,
     wup_ref, wdn_ref, out_ref, acc_ref) = refs
    tm = acc_ref.shape[0]
    tile = pl.program_id(0)
    if per_batch_grid:
        tile = tile * pl.num_programs(1) + pl.program_id(1)
    if drop_lead:
        n_steps = n_batch * per_batch
        slot = lax.rem(tile, 2)

        def wait_step(step, its_slot):
            was_lead = lax.rem(step, per_batch) == 0
            for lead_tile in (True, False):
                @pl.when(was_lead == lead_tile)
                def _():
                    _token_rows_copy(buf_ref, out_ref, sem_ref, its_slot, lead_tile, 0).wait()

        @pl.when(tile >= 2)
        def _():
            wait_step(tile - 2, slot)
        store_ref = buf_ref.at[slot]
    else:
        store_ref = out_ref
    n_split = ROW_SPLIT if tm >= ROW_SPLIT * MXU_ROWS else 1
    n = tm // n_split
    parts = [slice(i * n, (i + 1) * n) for i in range(n_split)]
    mix = [jnp.dot(yl_ref[p, :], wo_ref[0:LRU_WIDTH, :], preferred_element_type=jnp.float32)
           + jnp.dot(yg_ref[p, :], wo_ref[LRU_WIDTH:, :], preferred_element_type=jnp.float32)
           for p in parts]
    h1, xn = [], []
    for i, p in enumerate(parts):
        row = lax.broadcasted_iota(jnp.int32, (n, 1), 0) + (tile * tm + i * n)
        real = row >= PADF
        for b in range(1, n_batch):
            real = real & ((row < b * rows_per_batch) | (row >= b * rows_per_batch + PADF))
        h1.append(h_tile[p, :] + jnp.where(real, mix[i] * _rms_scale(mix[i]) * gpost_ref[...], 0.0))
        xn.append((h1[i] * _rms_scale(h1[i]) * gpre_ref[...]).astype(jnp.bfloat16))
    for c in range(D_FF // FF_TILE):
        cs = slice(c * FF_TILE, (c + 1) * FF_TILE)
        for i, p in enumerate(parts):
            up = jnp.dot(xn[i], wup_ref[:, cs], preferred_element_type=jnp.float32)
            act = jnp.square(jnp.maximum(up, 0.0)).astype(jnp.bfloat16)
            part = jnp.dot(act, wdn_ref[cs, :], preferred_element_type=jnp.float32)
            if c == 0:
                acc_ref[p, :] = part
            else:
                acc_ref[p, :] += part
    for i, p in enumerate(parts):
        ff = acc_ref[p, :]
        store_ref[p, :] = h1[i] + ff * _rms_scale(ff) * gpost2_ref[...]
    if drop_lead:
        b, j = pl.program_id(0), pl.program_id(1)
        seq = rows_per_batch - T0
        for lead_tile in (True, False):
            @pl.when((j == 0) == lead_tile)
            def _():
                first = b * seq + (0 if lead_tile else j * tm - T0)
                _token_rows_copy(buf_ref, out_ref, sem_ref, slot, lead_tile,
                                 pl.multiple_of(first, SUBLANES)).start()

        @pl.when(tile == n_steps - 1)
        def _():
            wait_step(tile, slot)
            if n_steps >= 2:
                wait_step(tile - 1, 1 - slot)


def _out_mlp(h, y_lru, y_gla, wo, g_post, g_pre, g_post2, w_up, w_dn, rows_per_batch,
             drop_lead, lead=None):
    rows = y_lru.shape[0]
    n_batch = rows // rows_per_batch
    consts = (wo, g_post, g_pre, g_post2, w_up, w_dn)
    tm = ROW_TILE
    per_batch = None
    scratch = [pltpu.VMEM((tm, D_MODEL), jnp.float32)]
    if lead is not None:
        assert not drop_lead
        grid, (h_spec, _, _), row = _token_specs(n_batch, rows_per_batch - T0, tm)
        per_batch = grid[1]
        h_specs, h_args = [h_spec, _const_spec(lead)], [h, lead]
        out_spec = row(D_MODEL)
        out_rows = rows
    elif drop_lead:
        per_batch = rows_per_batch // tm
        grid = (n_batch, per_batch)
        row = lambda w: pl.BlockSpec((tm, w), lambda b, j: (b * per_batch + j, 0))
        out_spec = pl.BlockSpec(memory_space=pl.ANY)
        out_rows = rows - n_batch * T0
        h_specs, h_args = [row(D_MODEL)], [h]
        scratch += [pltpu.VMEM((2, tm, D_MODEL), jnp.float32), pltpu.SemaphoreType.DMA((2,))]
    else:
        grid = (rows // tm,)
        row = lambda w: pl.BlockSpec((tm, w), lambda i: (i, 0))
        out_spec = row(D_MODEL)
        out_rows = rows
        h_specs, h_args = [row(D_MODEL)], [h]
    return pl.pallas_call(
        functools.partial(_out_mlp_kernel, rows_per_batch=rows_per_batch, n_batch=n_batch,
                          per_batch=per_batch, from_tokens=lead is not None, drop_lead=drop_lead),
        grid=grid,
        in_specs=h_specs + [row(512), row(512)] + [_const_spec(a) for a in consts],
        out_specs=out_spec,
        out_shape=jax.ShapeDtypeStruct((out_rows, D_MODEL), jnp.float32),
        scratch_shapes=scratch,
        compiler_params=pltpu.CompilerParams(
            dimension_semantics=("arbitrary",) * len(grid), vmem_limit_bytes=VMEM_LIMIT),
        name="out_mlp",
    )(*h_args, y_lru, y_gla, *map(_operand, consts))


def _block_diag_gates(wa, wx):
    per_half = LRU_HEADS // 2
    eye = jnp.eye(per_half, dtype=wa.dtype)

    def bd(w):
        w5 = w.reshape(-1, 2, per_half, LRU_HEAD_DIM, LRU_HEAD_DIM)
        return jnp.einsum('nhjil,jk->nhjikl', w5, eye).reshape(-1, 2, LRU_WIDTH // 2, LRU_WIDTH // 2)
    return (0.5 * jnp.concatenate([bd(wa), bd(wx)], axis=3)).astype(jnp.bfloat16)


def kernel(x, meta_tokens, norm_mix_pre, norm_mix_post, norm_mlp_pre, norm_mlp_post,
           w_in, conv_w, conv_b,
           lru_wa_f, lru_ba_f, lru_wx_f, lru_bx_f, lru_lambda_f,
           lru_wa_b, lru_ba_b, lru_wx_b, lru_bx_b, lru_lambda_b,
           gla_wg_f, gla_bg_f, gla_wg_b, gla_bg_b, gla_head_norm,
           w_out, w_mlp_up, w_mlp_down):
    bsz, seq, d = x.shape
    depth = w_in.shape[0]
    tp = seq + T0
    assert d == D_MODEL and tp % TIME_TILE == 0 and tp % GLA_TIME_TILE == 0
    assert (bsz * tp) % ROW_TILE == 0
    assert ROW_TILE > T0 and tp % ROW_TILE == 0
    lead = jnp.concatenate([jnp.zeros((PADF, d), x.dtype), meta_tokens.astype(x.dtype)], axis=0)
    h = x.reshape(bsz * seq, d)
    in_place = depth > 1
    if not in_place:
        h = jnp.concatenate([jnp.broadcast_to(lead[None], (bsz, T0, d)), x], axis=1).reshape(-1, d)
    vec = lambda a: a.reshape(depth, 1, -1)
    bf = lambda a: a.astype(jnp.bfloat16)

    w_main = bf(w_in)
    w_zg = bf(jnp.tile(w_in[:, :, W_IN_MAIN:], (1, 1, 3)))
    wg = jnp.zeros((depth, 2 * GLA_RANK, 2 * GLA_QK), jnp.float32)
    wg = wg.at[:, :GLA_RANK, :GLA_QK].set(gla_wg_f).at[:, GLA_RANK:, GLA_QK:].set(gla_wg_b)
    wg_hi, wg_lo = _split_bf16(wg)
    wg3 = jnp.concatenate([wg_hi, wg_lo, wg_hi], axis=1)
    bg = vec(jnp.concatenate([gla_bg_f, gla_bg_b], axis=1))
    gates_f = _block_diag_gates(lru_wa_f, lru_wx_f)
    gates_b = _block_diag_gates(lru_wa_b, lru_wx_b)
    params = dict(
        g_pre=vec(norm_mix_pre), w_main=w_main, w_zg=w_zg, wg3=wg3, bg=bg,
        conv_w=conv_w, conv_b=vec(conv_b),
        lru_f=(gates_f, vec(0.5 * lru_ba_f), vec(0.5 * lru_bx_f), vec(lru_lambda_f)),
        lru_b=(gates_b, vec(0.5 * lru_ba_b), vec(0.5 * lru_bx_b), vec(lru_lambda_b)),
        head_norm=vec(gla_head_norm),
        out=(bf(w_out), vec(norm_mix_post),
             vec(norm_mlp_pre), vec(norm_mlp_post), bf(w_mlp_up), bf(w_mlp_down)))

    for l in range(depth):
        at = lambda a: _Layer(a, l)
        p = jax.tree.map(at, params)
        tokens = dict(lead=lead) if (l == 0 and in_place) else {}
        outs = _in_proj(h, p['g_pre'], p['w_main'], p['w_zg'], p['wg3'], p['bg'],
                        p['conv_w'], p['conv_b'], n_batch=bsz, **tokens)
        xc3, gate3, qk3, v3, go3, gfb3 = (a.reshape(bsz, tp, a.shape[-1]) for a in outs)

        h_f = _lru_pass(xc3, *p['lru_f'], reverse=False)
        y_lru = _lru_pass(xc3, *p['lru_b'], reverse=True, hf=h_f, gate=gate3)
        o_f = _gla_pass(qk3, v3, gfb3, reverse=False)
        y_gla = _gla_pass(qk3, v3, gfb3, reverse=True, o_f=o_f, go=go3, head_norm=p['head_norm'])

        h = _out_mlp(h, y_lru.reshape(bsz * tp, -1), y_gla.reshape(bsz * tp, -1), *p['out'],
                     tp, drop_lead=(l == depth - 1), **tokens)
    return h.reshape(bsz, seq, d)
```

```python
import functools

import numpy as np
import jax
import jax.numpy as jnp
from jax import lax
from jax.experimental import pallas as pl
from jax.experimental.pallas import tpu as pltpu

D_MODEL = 1024
N_META = 16
CHUNK = 64
LRU_WIDTH = 512
LRU_HEADS = 8
LRU_HEAD_DIM = 64
LRU_C = 8.0
GLA_WIDTH = 512
GLA_HEADS = 4
GLA_DV = 128
GLA_DK = 64
GLA_QK = GLA_HEADS * GLA_DK
GLA_RANK = 16
GLA_GATE_NORM = 16.0
W_IN_MAIN = 2 * LRU_WIDTH + 2 * GLA_QK + 2 * GLA_WIDTH
D_FF = 4096
EPS = 1e-6
LOG2_E = 1.4426950408889634
GELU_SCALE = 0.7978845608028654
GELU_CUBIC = 0.044715
F32_TINY = float(np.finfo(np.float32).tiny)

LEAD_CHUNKS = 4
T0 = LEAD_CHUNKS * CHUNK
PADF = T0 - N_META
ROW_TILE = 768
TIME_TILE = 1056
SCAN_UNROLL = 12
GLA_TIME_TILE = 768
GLA_UNROLL = 3
IN_SPLIT = 3
ROW_SPLIT = 2
FF_TILE = 1024
SUBLANES = 8
LANES = 128
MXU_ROWS = 256
V7X_VMEM_BYTES = 64 * 1024 * 1024
VMEM_LIMIT = V7X_VMEM_BYTES * 7 // 8


class _Layer:
    def __init__(self, stacked, layer):
        self.stacked, self.layer, self.shape = stacked, layer, tuple(stacked.shape[1:])


def _operand(a):
    return a.stacked if isinstance(a, _Layer) else a


def _const_spec(a):
    nd = len(a.shape)
    if isinstance(a, _Layer):
        layer = a.layer
        return pl.BlockSpec((None,) + a.shape, lambda *_: (layer,) + (0,) * nd,
                            pipeline_mode=pl.Buffered(1))
    return pl.BlockSpec(a.shape, lambda *_: (0,) * nd, pipeline_mode=pl.Buffered(1))


def _rms_scale(x):
    return lax.rsqrt(jnp.mean(x * x, axis=-1, keepdims=True) + EPS)


def _softplus(x):
    return jnp.maximum(x, 0.0) + jnp.log1p(jnp.exp(-jnp.abs(x)))


def _split_bf16(x):
    hi = x.astype(jnp.bfloat16)
    lo = (x - hi.astype(jnp.float32)).astype(jnp.bfloat16)
    return hi, lo


def _token_tile(x_ref, lead_ref):
    tm = x_ref.shape[0]
    fetched = x_ref[...]
    first = jnp.concatenate([lead_ref[...], fetched[0:tm - T0, :]], axis=0)
    return jnp.where(pl.program_id(1) == 0, first, fetched)


def _in_proj_kernel(*refs, n_tiles, from_tokens):
    if from_tokens:
        h_ref, hp_ref, hn_ref, lead_ref, *refs = refs
    else:
        h_ref, hp_ref, hn_ref, *refs = refs
    (g_ref, w_ref, wzg_ref, wg_ref, bg_ref, cw_ref, cb_ref,
     xc_ref, gate_ref, qk_ref, v_ref, go_ref, gfb_ref, xs_ref) = refs
    tm = h_ref.shape[0]
    normed = lambda x: (x * _rms_scale(x) * g_ref[...]).astype(jnp.bfloat16)
    if from_tokens:
        h_tile = _token_tile(h_ref, lead_ref)
        no_next = pl.program_id(1) == n_tiles - 1
    else:
        h_tile = h_ref[...]
        no_next = pl.program_id(0) == n_tiles - 1
    halo = jnp.concatenate([hp_ref[...], hn_ref[...]], axis=0)
    xn = normed(h_tile)
    xn_ext = jnp.concatenate([xn, normed(halo)], axis=0)

    zg = jnp.dot(xn, wzg_ref[...], preferred_element_type=jnp.float32)
    zx = jnp.dot(xn_ext, w_ref[:, 0:LRU_WIDTH], preferred_element_type=jnp.float32)
    zhi = zg.astype(jnp.bfloat16)
    zlo = (zg - zhi.astype(jnp.float32)).astype(jnp.bfloat16)
    lane = lax.broadcasted_iota(jnp.int32, zg.shape, 1)
    lhs = jnp.where(lane < 4 * GLA_RANK, zhi, zlo)
    pre = jnp.dot(lhs, wg_ref[...], preferred_element_type=jnp.float32) + bg_ref[...]
    n = tm // IN_SPLIT
    parts = [slice(i * n, (i + 1) * n) for i in range(IN_SPLIT)]
    zs = [jnp.dot(xn[p, :], w_ref[:, LRU_WIDTH:W_IN_MAIN], preferred_element_type=jnp.float32)
          for p in parts]

    xs_ref[0:SUBLANES, :] = zx[tm:tm + SUBLANES, :]
    xs_ref[SUBLANES:SUBLANES + tm, :] = zx[0:tm, :]
    xs_ref[SUBLANES + tm:2 * SUBLANES + tm, :] = jnp.where(
        no_next, 0.0, zx[tm + SUBLANES:tm + 2 * SUBLANES, :])
    cw = cw_ref[...]
    xc_ref[...] = (cw[0:1, :] * xs_ref[SUBLANES - 2:SUBLANES - 2 + tm, :]
                   + cw[1:2, :] * xs_ref[SUBLANES - 1:SUBLANES - 1 + tm, :]
                   + cw[2:3, :] * xs_ref[SUBLANES:SUBLANES + tm, :]
                   + cw[3:4, :] * xs_ref[SUBLANES + 1:SUBLANES + 1 + tm, :]
                   + cb_ref[...])
    logsig = jnp.minimum(pre, 0.0) - jnp.log(1.0 + jnp.exp(-jnp.abs(pre)))
    gfb_ref[...] = logsig * (LOG2_E / GLA_GATE_NORM)

    for p, z in zip(parts, zs):
        c0, c1, c2, c3 = LRU_WIDTH, LRU_WIDTH + GLA_QK, LRU_WIDTH + 2 * GLA_QK, \
            LRU_WIDTH + 2 * GLA_QK + GLA_WIDTH
        gt = z[:, 0:c0]
        gelu = 0.5 * gt * (1.0 + jnp.tanh(GELU_SCALE * (gt + GELU_CUBIC * gt * gt * gt)))
        gate_ref[p, :] = gelu.astype(jnp.bfloat16)
        q = z[:, c0:c1] * (GLA_DK ** -0.5)
        qk_ref[p, :] = jnp.concatenate([q, z[:, c1:c2]], axis=1).astype(jnp.bfloat16)
        v_ref[p, :] = z[:, c2:c3].astype(jnp.bfloat16)
        half = 0.5 * z[:, c3:]
        go_ref[p, :] = (half * jnp.tanh(half) + half).astype(jnp.bfloat16)


def _token_specs(n_batch, seq, tm):
    per_batch = (seq + T0) // tm
    total = n_batch * seq
    start = lambda b, j: b * seq + j * tm - T0
    window = lambda rows, at: pl.BlockSpec((pl.Element(rows), pl.Element(D_MODEL)),
                                           lambda b, j: (pl.multiple_of(
                                               jnp.clip(at(b, j), 0, total - rows), SUBLANES), 0))
    specs = [window(tm, lambda b, j: b * seq + jnp.maximum(j * tm - T0, 0)),
             window(SUBLANES, lambda b, j: start(b, j) - SUBLANES),
             window(SUBLANES, lambda b, j: start(b, j) + tm)]
    return (n_batch, per_batch), specs, (lambda w: pl.BlockSpec((tm, w), lambda b, j: (b * per_batch + j, 0)))


def _in_proj(h, g, w_main, w_zg, wg3, bg, conv_w, conv_b, lead=None, n_batch=None):
    tm = ROW_TILE
    consts = (g, w_main, w_zg, wg3, bg, conv_w, conv_b)
    if lead is None:
        rows = h.shape[0]
        grid = (rows // tm,)
        g8 = tm // SUBLANES
        row = lambda w: pl.BlockSpec((tm, w), lambda i: (i, 0))
        h_specs = [row(D_MODEL),
                   pl.BlockSpec((SUBLANES, D_MODEL), lambda i: (jnp.maximum(i * g8 - 1, 0), 0)),
                   pl.BlockSpec((SUBLANES, D_MODEL),
                                lambda i: (jnp.minimum((i + 1) * g8, rows // SUBLANES - 1), 0))]
        h_args = [h, h, h]
    else:
        seq = h.shape[0] // n_batch
        rows = n_batch * (seq + T0)
        grid, h_specs, row = _token_specs(n_batch, seq, tm)
        h_specs.append(_const_spec(lead))
        h_args = [h, h, h, lead]
    outs = [(LRU_WIDTH, jnp.float32), (LRU_WIDTH, jnp.bfloat16), (2 * GLA_QK, jnp.bfloat16),
            (GLA_WIDTH, jnp.bfloat16), (GLA_WIDTH, jnp.bfloat16), (2 * GLA_QK, jnp.float32)]
    return pl.pallas_call(
        functools.partial(_in_proj_kernel, n_tiles=grid[-1], from_tokens=lead is not None),
        grid=grid,
        in_specs=h_specs + [_const_spec(a) for a in consts],
        out_specs=[row(w) for w, _ in outs],
        out_shape=[jax.ShapeDtypeStruct((rows, w), dt) for w, dt in outs],
        scratch_shapes=[pltpu.VMEM((tm + 2 * SUBLANES, LRU_WIDTH), jnp.float32)],
        compiler_params=pltpu.CompilerParams(
            dimension_semantics=("arbitrary",) * len(grid), vmem_limit_bytes=VMEM_LIMIT),
        name="in_proj",
    )(*h_args, *map(_operand, consts))


def _lru_kernel(*refs, reverse, n_tiles):
    if reverse:
        (xc_ref, wg_ref, ba_ref, bx_ref, lam_ref, hf_ref, gate_ref,
         out_ref, a_scr, u_scr, carry_ref) = refs
    else:
        xc_ref, wg_ref, ba_ref, bx_ref, lam_ref, out_ref, a_scr, u_scr, carry_ref = refs
    bsz, tt, _ = xc_ref.shape
    step = pl.program_id(0)
    tile = (n_tiles - 1 - step) if reverse else step

    @pl.when(step == 0)
    def _():
        carry_ref[...] = jnp.zeros_like(carry_ref)

    n_slabs = LRU_WIDTH // LANES
    half = LRU_WIDTH // 2
    half_rate = (0.5 * LRU_C) * _softplus(-lam_ref[...])
    for bi in range(bsz):
        xc = xc_ref[bi]
        xcb = xc.astype(jnp.bfloat16)
        p0 = jnp.dot(xcb[:, :half], wg_ref[0], preferred_element_type=jnp.float32)
        p1 = jnp.dot(xcb[:, half:], wg_ref[1], preferred_element_type=jnp.float32)
        r_pre = jnp.concatenate([p0[:, :half], p1[:, :half]], axis=1)
        i_pre = jnp.concatenate([p0[:, half:], p1[:, half:]], axis=1)
        tr = jnp.tanh(r_pre + ba_ref[...])
        gi = 0.5 * jnp.tanh(i_pre + bx_ref[...]) + 0.5
        neg_log_a = tr * half_rate + half_rate
        a = jnp.exp(-neg_log_a)
        w = jnp.tanh(neg_log_a) * (1.0 + a * a)
        u = (w * lax.rsqrt(jnp.maximum(w, F32_TINY))) * (gi * xc)
        if not reverse:
            row = lax.broadcasted_iota(jnp.int32, xc.shape, 0)
            u = jnp.where(row + tile * tt >= PADF, u, 0.0)
        for k in range(n_slabs):
            a_scr[bi * n_slabs + k] = a[:, k * LANES:(k + 1) * LANES]
            u_scr[bi * n_slabs + k] = u[:, k * LANES:(k + 1) * LANES]

    run = tt // SUBLANES
    chains = bsz * n_slabs
    steps = lambda j: pl.ds(j, SUBLANES, stride=run)

    assert run % SCAN_UNROLL == 0
    n_blocks = run // SCAN_UNROLL
    block_base = lambda i: ((n_blocks - 1 - i) if reverse else i) * SCAN_UNROLL
    offsets = range(SCAN_UNROLL - 1, -1, -1) if reverse else range(SCAN_UNROLL)

    def run_ends(i, carry):
        state, prod = list(carry[0]), list(carry[1])
        base = block_base(i)
        for o in offsets:
            for k in range(chains):
                aj = a_scr[k, steps(base + o), :]
                state[k] = aj * state[k] + u_scr[k, steps(base + o), :]
                prod[k] = prod[k] * aj
        return tuple(state), tuple(prod)

    zeros = tuple(jnp.zeros((SUBLANES, LANES), jnp.float32) for _ in range(chains))
    ones = tuple(z + 1.0 for z in zeros)
    end_state, end_prod = lax.fori_loop(0, n_blocks, run_ends, (zeros, ones))
    end_state, end_prod = jnp.stack(end_state), jnp.stack(end_prod)

    c = carry_ref[...]
    order = range(SUBLANES - 1, -1, -1) if reverse else range(SUBLANES)
    entering = [None] * SUBLANES
    for r in order:
        entering[r] = c
        c = end_prod[:, r:r + 1, :] * c + end_state[:, r:r + 1, :]
    carry_ref[...] = c
    entering = jnp.concatenate(entering, axis=1)

    def scan(i, state):
        state = list(state)
        base = block_base(i)
        for o in offsets:
            for k in range(chains):
                state[k] = a_scr[k, steps(base + o), :] * state[k] + u_scr[k, steps(base + o), :]
                u_scr[k, steps(base + o), :] = state[k]
        return tuple(state)

    lax.fori_loop(0, n_blocks, scan, tuple(entering[k] for k in range(chains)))

    for bi in range(bsz):
        hs = jnp.concatenate([u_scr[bi * n_slabs + k] for k in range(n_slabs)], axis=1)
        if reverse:
            gate = gate_ref[bi].astype(jnp.float32)
            out_ref[bi] = ((hf_ref[bi] + hs) * gate).astype(out_ref.dtype)
        else:
            out_ref[bi] = hs


def _lru_pass(xc, wgate, ba, bx, lam, *, reverse, hf=None, gate=None):
    bsz, tp, _ = xc.shape
    tt = TIME_TILE
    nt = tp // tt
    chains = bsz * (LRU_WIDTH // LANES)
    tsel = (lambda t: nt - 1 - t) if reverse else (lambda t: t)
    cur = pl.BlockSpec((bsz, tt, LRU_WIDTH), lambda t: (0, tsel(t), 0))
    in_specs = [cur] + [_const_spec(a) for a in (wgate, ba, bx, lam)]
    args = [xc, wgate, ba, bx, lam]
    if reverse:
        in_specs += [cur, cur]
        args += [hf, gate]
        out_dtype = jnp.bfloat16
    else:
        out_dtype = jnp.float32
    return pl.pallas_call(
        functools.partial(_lru_kernel, reverse=reverse, n_tiles=nt),
        grid=(nt,),
        in_specs=in_specs,
        out_specs=cur,
        out_shape=jax.ShapeDtypeStruct((bsz, tp, LRU_WIDTH), out_dtype),
        scratch_shapes=[pltpu.VMEM((chains, tt, LANES), jnp.float32),
                        pltpu.VMEM((chains, tt, LANES), jnp.float32),
                        pltpu.VMEM((chains, 1, LANES), jnp.float32)],
        compiler_params=pltpu.CompilerParams(
            dimension_semantics=("arbitrary",), vmem_limit_bytes=VMEM_LIMIT),
        name="lru_bwd" if reverse else "lru_fwd",
    )(*map(_operand, args))


def _gla_constants(reverse):
    c, w = CHUNK, SUBLANES
    t = np.arange(c)[:, None]
    s = np.arange(c)[None, :]
    gt, gs, pt, ps = t // w, s // w, t // 2, s // 2
    pair_code = 2 + (s % w) // 2
    if reverse:
        code = np.where((pt == ps) & (s > t), 1,
                        np.where((gt == gs) & (ps > pt), pair_code, np.where(gs > gt, w + gs, -1)))
        blocks = [s >= t, (t % 2 == 1) & (s == t - 1)]
    else:
        code = np.where((pt == ps) & (s <= t), t - s,
                        np.where((gt == gs) & (ps < pt), pair_code, np.where(gs < gt, w + gs, -1)))
        blocks = [s <= t, (t % 2 == 0) & (s == t + 1)]
    lmat = np.concatenate([blk.astype(np.float32) for blk in blocks], axis=0)
    lmat = np.concatenate([lmat, lmat], axis=1)
    code = np.tile(code.astype(np.int32), (1, GLA_HEADS))
    hmask = np.kron(np.eye(GLA_HEADS, dtype=np.float32), np.ones((c, GLA_DK), np.float32))
    return jnp.asarray(lmat, jnp.bfloat16), jnp.asarray(code), jnp.asarray(hmask, jnp.bfloat16)


def _gla_decays(g, lmat):
    ghi, glo = _split_bf16(g)
    return jnp.dot(lmat, jnp.concatenate([ghi, glo], axis=0), preferred_element_type=jnp.float32)


def _gla_scores(q, k, g, logdec, hmask, reverse):
    n_groups = CHUNK // SUBLANES
    pairs = SUBLANES // 2

    def head_stack(x):
        return jnp.concatenate([x.astype(jnp.bfloat16)] * GLA_HEADS, axis=0) * hmask

    def terms(lhs, rhs):
        return lax.dot_general(jnp.concatenate(lhs, axis=0).astype(jnp.bfloat16), head_stack(rhs),
                               (((1,), (1,)), ((), ())), preferred_element_type=jnp.float32)

    def facing_rows(first):
        return jnp.concatenate([jnp.broadcast_to(b[first + gi * SUBLANES:first + gi * SUBLANES + 1, :],
                                                 (SUBLANES, GLA_QK)) for gi in range(n_groups)], axis=0)

    b = logdec[0:CHUNK, :]

    step = q * jnp.exp2(g)
    p_pair = terms([step] if reverse else [q, step], k)

    k_pair = k * jnp.exp2(logdec[CHUNK:2 * CHUNK, :])
    pair_ids = range(1, pairs) if reverse else range(pairs - 1)
    lhs = [q * jnp.exp2(jnp.minimum(b - facing_rows(2 * p if reverse else 2 * p + 1), 0.0))
           for p in pair_ids]
    p_group = terms(lhs, k_pair)

    b_anchor = facing_rows(0 if reverse else SUBLANES - 1)
    groups = list(range(1, n_groups)) if reverse else list(range(n_groups - 1))
    row_sel = (lambda gi: slice(0, gi * SUBLANES)) if reverse else \
              (lambda gi: slice((gi + 1) * SUBLANES, CHUNK))
    anchor_row = (lambda gi: gi * SUBLANES) if reverse else (lambda gi: gi * SUBLANES + SUBLANES - 1)
    lhs = [q[row_sel(gi), :] * jnp.exp2(b[row_sel(gi), :] - b[anchor_row(gi):anchor_row(gi) + 1, :])
           for gi in groups]
    p_cross = terms(lhs, k * jnp.exp2(b_anchor - b))

    last = 0 if reverse else CHUNK - 1
    b_last = b[last:last + 1, :]
    qe = (q * jnp.exp2(b)).astype(jnp.bfloat16)
    ke = (k * jnp.exp2(b_last - b)).astype(jnp.bfloat16)
    return p_pair, p_group, p_cross, qe, ke, jnp.exp2(b_last)


def _gla_apply(p_pair, p_group, p_cross, qe, ke, e_last, vb, s_ref, code, reverse):
    n_groups = CHUNK // SUBLANES
    pairs = SUBLANES // 2
    groups = list(range(1, n_groups)) if reverse else list(range(n_groups - 1))
    scores = jnp.zeros((CHUNK, GLA_HEADS * CHUNK), jnp.float32)
    for i, d in enumerate([1] if reverse else [0, 1]):
        scores = jnp.where(code == d, p_pair[i * CHUNK:(i + 1) * CHUNK, :], scores)
    for i, p in enumerate(range(1, pairs) if reverse else range(pairs - 1)):
        scores = jnp.where(code == 2 + p, p_group[i * CHUNK:(i + 1) * CHUNK, :], scores)
    off = 0
    for gi in groups:
        rows = slice(0, gi * SUBLANES) if reverse else slice((gi + 1) * SUBLANES, CHUNK)
        n = rows.stop - rows.start
        part = jnp.where(code[rows, :] == SUBLANES + gi, p_cross[off:off + n, :], scores[rows, :])
        scores = jnp.concatenate([part, scores[n:, :]] if reverse else [scores[:CHUNK - n, :], part],
                                 axis=0)
        off += n
    sb = scores.astype(jnp.bfloat16)

    eye = (lax.broadcasted_iota(jnp.int32, (GLA_DK, GLA_DK), 0)
           == lax.broadcasted_iota(jnp.int32, (GLA_DK, GLA_DK), 1))
    zero_v = jnp.zeros((CHUNK, GLA_DV), jnp.bfloat16)
    zero_s = jnp.zeros((GLA_DK, GLA_DV), jnp.bfloat16)
    outs = []
    for pair in range(GLA_HEADS // 2):
        heads = (2 * pair, 2 * pair + 1)
        vh = [vb[:, hd * GLA_DV:(hd + 1) * GLA_DV] for hd in heads]
        st = [s_ref[hd] for hd in heads]
        sh = [s.astype(jnp.bfloat16) for s in st]
        lanes = slice(pair * 2 * CHUNK, (pair + 1) * 2 * CHUNK)
        lhs = jnp.concatenate([sb[:, lanes], qe[:, lanes]], axis=1)
        rhs = jnp.concatenate([jnp.concatenate([vh[0], zero_v], axis=1),
                               jnp.concatenate([zero_v, vh[1]], axis=1),
                               jnp.concatenate([sh[0], zero_s], axis=1),
                               jnp.concatenate([zero_s, sh[1]], axis=1)], axis=0)
        outs.append(jnp.dot(lhs, rhs, preferred_element_type=jnp.float32))
        for i, hd in enumerate(heads):
            ks = slice(hd * GLA_DK, (hd + 1) * GLA_DK)
            kv = lax.dot_general(ke[:, ks], vh[i], (((0,), (0,)), ((), ())),
                                 preferred_element_type=jnp.float32)
            e_col = jnp.sum(jnp.where(eye, e_last[:, ks], 0.0), axis=1, keepdims=True)
            s_ref[hd] = st[i] * e_col + kv
    return jnp.concatenate(outs, axis=1)


def _gla_kernel(*refs, reverse):
    if reverse:
        (qk_ref, v_ref, g_ref, lmat_ref, code_ref, hmask_ref, of_ref, go_ref, hn_ref,
         out_ref, s_ref) = refs
    else:
        qk_ref, v_ref, g_ref, lmat_ref, code_ref, hmask_ref, out_ref, s_ref = refs
    bsz, tt, _ = qk_ref.shape
    n_chunks = tt // CHUNK

    @pl.when(pl.program_id(0) == 0)
    def _():
        s_ref[...] = jnp.zeros_like(s_ref)

    lmat = lmat_ref[...]
    code = code_ref[...]
    hmask = hmask_ref[...]

    def chunk_body(j, carry):
        work = []
        for u in range(GLA_UNROLL):
            jj = j * GLA_UNROLL + u
            cidx = (n_chunks - 1 - jj) if reverse else jj
            rows = pl.ds(pl.multiple_of(cidx * CHUNK, CHUNK), CHUNK)
            work += [(bi, rows) for bi in range(bsz)]
        decs = [_gla_decays(g_ref[bi, rows, :], lmat) for bi, rows in work]
        terms = [_gla_scores(qk_ref[bi, rows, 0:GLA_QK], qk_ref[bi, rows, GLA_QK:2 * GLA_QK],
                             g_ref[bi, rows, :], dec, hmask, reverse)
                 for (bi, rows), dec in zip(work, decs)]
        for (bi, rows), term in zip(work, terms):
            o = _gla_apply(*term, v_ref[bi, rows, :].astype(jnp.bfloat16), s_ref.at[bi], code,
                           reverse)
            if reverse:
                o = o + of_ref[bi, rows, :]
                parts = []
                for hd in range(GLA_HEADS):
                    oh = o[:, hd * GLA_DV:(hd + 1) * GLA_DV]
                    parts.append(oh * _rms_scale(oh))
                y = jnp.concatenate(parts, axis=1) * hn_ref[...] * go_ref[bi, rows, :].astype(jnp.float32)
                out_ref[bi, rows, :] = y.astype(out_ref.dtype)
            else:
                out_ref[bi, rows, :] = o
        return carry

    lax.fori_loop(0, n_chunks // GLA_UNROLL, chunk_body, 0)


def _gla_pass(qk, v, gfb, *, reverse, o_f=None, go=None, head_norm=None):
    bsz, tp, _ = qk.shape
    tt = GLA_TIME_TILE
    nt = tp // tt
    tsel = (lambda t: nt - 1 - t) if reverse else (lambda t: t)
    blk = lambda w, c=0: pl.BlockSpec((bsz, tt, w), lambda t: (0, tsel(t), c))
    consts = _gla_constants(reverse)
    in_specs = [blk(2 * GLA_QK), blk(GLA_WIDTH), blk(GLA_QK, 1 if reverse else 0)]
    in_specs += [_const_spec(a) for a in consts]
    args = [qk, v, gfb, *consts]
    if reverse:
        in_specs += [blk(GLA_WIDTH), blk(GLA_WIDTH), _const_spec(head_norm)]
        args += [o_f, go, head_norm]
        out_dtype = jnp.bfloat16
    else:
        out_dtype = jnp.float32
    return pl.pallas_call(
        functools.partial(_gla_kernel, reverse=reverse),
        grid=(nt,),
        in_specs=in_specs,
        out_specs=blk(GLA_WIDTH),
        out_shape=jax.ShapeDtypeStruct((bsz, tp, GLA_WIDTH), out_dtype),
        scratch_shapes=[pltpu.VMEM((bsz, GLA_HEADS, GLA_DK, GLA_DV), jnp.float32)],
        compiler_params=pltpu.CompilerParams(
            dimension_semantics=("arbitrary",), vmem_limit_bytes=VMEM_LIMIT),
        name="gla_bwd" if reverse else "gla_fwd",
    )(*map(_operand, args))


def _token_rows_copy(buf_ref, out_ref, sem_ref, slot, lead_tile, first_token):
    tm = buf_ref.shape[1]
    if lead_tile:
        return pltpu.make_async_copy(buf_ref.at[slot, pl.ds(T0, tm - T0)],
                                     out_ref.at[pl.ds(first_token, tm - T0)], sem_ref.at[slot])
    return pltpu.make_async_copy(buf_ref.at[slot], out_ref.at[pl.ds(first_token, tm)],
                                 sem_ref.at[slot])


def _out_mlp_kernel(*refs, rows_per_batch, n_batch, per_batch, from_tokens, drop_lead):
    per_batch_grid = per_batch is not None
    if from_tokens:
        x_ref, lead_ref, *refs = refs
        h_tile = _token_tile(x_ref, lead_ref)
    else:
        h_tile, *refs = refs
    if drop_lead:
        *refs, buf_ref, sem_ref = refs
    (yl_ref, yg_ref, wo_ref, gpost_ref, gpre_ref, gpost2_ref,
     wup_ref, wdn_ref, out_ref, acc_ref) = refs
    tm = acc_ref.shape[0]
    tile = pl.program_id(0)
    if per_batch_grid:
        tile = tile * pl.num_programs(1) + pl.program_id(1)
    if drop_lead:
        n_steps = n_batch * per_batch
        slot = lax.rem(tile, 2)

        def wait_step(step, its_slot):
            was_lead = lax.rem(step, per_batch) == 0
            for lead_tile in (True, False):
                @pl.when(was_lead == lead_tile)
                def _():
                    _token_rows_copy(buf_ref, out_ref, sem_ref, its_slot, lead_tile, 0).wait()

        @pl.when(tile >= 2)
        def _():
            wait_step(tile - 2, slot)
        store_ref = buf_ref.at[slot]
    else:
        store_ref = out_ref
    n_split = ROW_SPLIT if tm >= ROW_SPLIT * MXU_ROWS else 1
    n = tm // n_split
    parts = [slice(i * n, (i + 1) * n) for i in range(n_split)]
    mix = [jnp.dot(yl_ref[p, :], wo_ref[0:LRU_WIDTH, :], preferred_element_type=jnp.float32)
           + jnp.dot(yg_ref[p, :], wo_ref[LRU_WIDTH:, :], preferred_element_type=jnp.float32)
           for p in parts]
    h1, xn = [], []
    for i, p in enumerate(parts):
        row = lax.broadcasted_iota(jnp.int32, (n, 1), 0) + (tile * tm + i * n)
        real = row >= PADF
        for b in range(1, n_batch):
            real = real & ((row < b * rows_per_batch) | (row >= b * rows_per_batch + PADF))
        h1.append(h_tile[p, :] + jnp.where(real, mix[i] * _rms_scale(mix[i]) * gpost_ref[...], 0.0))
        xn.append((h1[i] * _rms_scale(h1[i]) * gpre_ref[...]).astype(jnp.bfloat16))
    for c in range(D_FF // FF_TILE):
        cs = slice(c * FF_TILE, (c + 1) * FF_TILE)
        for i, p in enumerate(parts):
            up = jnp.dot(xn[i], wup_ref[:, cs], preferred_element_type=jnp.float32)
            act = jnp.square(jnp.maximum(up, 0.0)).astype(jnp.bfloat16)
            part = jnp.dot(act, wdn_ref[cs, :], preferred_element_type=jnp.float32)
            if c == 0:
                acc_ref[p, :] = part
            else:
                acc_ref[p, :] += part
    for i, p in enumerate(parts):
        ff = acc_ref[p, :]
        store_ref[p, :] = h1[i] + ff * _rms_scale(ff) * gpost2_ref[...]
    if drop_lead:
        b, j = pl.program_id(0), pl.program_id(1)
        seq = rows_per_batch - T0
        for lead_tile in (True, False):
            @pl.when((j == 0) == lead_tile)
            def _():
                first = b * seq + (0 if lead_tile else j * tm - T0)
                _token_rows_copy(buf_ref, out_ref, sem_ref, slot, lead_tile,
                                 pl.multiple_of(first, SUBLANES)).start()

        @pl.when(tile == n_steps - 1)
        def _():
            wait_step(tile, slot)
            if n_steps >= 2:
                wait_step(tile - 1, 1 - slot)


def _out_mlp(h, y_lru, y_gla, wo, g_post, g_pre, g_post2, w_up, w_dn, rows_per_batch,
             drop_lead, lead=None):
    rows = y_lru.shape[0]
    n_batch = rows // rows_per_batch
    consts = (wo, g_post, g_pre, g_post2, w_up, w_dn)
    tm = ROW_TILE
    per_batch = None
    scratch = [pltpu.VMEM((tm, D_MODEL), jnp.float32)]
    if lead is not None:
        assert not drop_lead
        grid, (h_spec, _, _), row = _token_specs(n_batch, rows_per_batch - T0, tm)
        per_batch = grid[1]
        h_specs, h_args = [h_spec, _const_spec(lead)], [h, lead]
        out_spec = row(D_MODEL)
        out_rows = rows
    elif drop_lead:
        per_batch = rows_per_batch // tm
        grid = (n_batch, per_batch)
        row = lambda w: pl.BlockSpec((tm, w), lambda b, j: (b * per_batch + j, 0))
        out_spec = pl.BlockSpec(memory_space=pl.ANY)
        out_rows = rows - n_batch * T0
        h_specs, h_args = [row(D_MODEL)], [h]
        scratch += [pltpu.VMEM((2, tm, D_MODEL), jnp.float32), pltpu.SemaphoreType.DMA((2,))]
    else:
        grid = (rows // tm,)
        row = lambda w: pl.BlockSpec((tm, w), lambda i: (i, 0))
        out_spec = row(D_MODEL)
        out_rows = rows
        h_specs, h_args = [row(D_MODEL)], [h]
    return pl.pallas_call(
        functools.partial(_out_mlp_kernel, rows_per_batch=rows_per_batch, n_batch=n_batch,
                          per_batch=per_batch, from_tokens=lead is not None, drop_lead=drop_lead),
        grid=grid,
        in_specs=h_specs + [row(LRU_WIDTH), row(GLA_WIDTH)] + [_const_spec(a) for a in consts],
        out_specs=out_spec,
        out_shape=jax.ShapeDtypeStruct((out_rows, D_MODEL), jnp.float32),
        scratch_shapes=scratch,
        compiler_params=pltpu.CompilerParams(
            dimension_semantics=("arbitrary",) * len(grid), vmem_limit_bytes=VMEM_LIMIT),
        name="out_mlp",
    )(*h_args, y_lru, y_gla, *map(_operand, consts))


def _block_diag_gates(wa, wx):
    per_half = LRU_HEADS // 2
    eye = jnp.eye(per_half, dtype=wa.dtype)

    def bd(w):
        w5 = w.reshape(-1, 2, per_half, LRU_HEAD_DIM, LRU_HEAD_DIM)
        return jnp.einsum('nhjil,jk->nhjikl', w5, eye).reshape(-1, 2, LRU_WIDTH // 2, LRU_WIDTH // 2)
    return (0.5 * jnp.concatenate([bd(wa), bd(wx)], axis=3)).astype(jnp.bfloat16)


def kernel(x, meta_tokens, norm_mix_pre, norm_mix_post, norm_mlp_pre, norm_mlp_post,
           w_in, conv_w, conv_b,
           lru_wa_f, lru_ba_f, lru_wx_f, lru_bx_f, lru_lambda_f,
           lru_wa_b, lru_ba_b, lru_wx_b, lru_bx_b, lru_lambda_b,
           gla_wg_f, gla_bg_f, gla_wg_b, gla_bg_b, gla_head_norm,
           w_out, w_mlp_up, w_mlp_down):
    bsz, seq, d = x.shape
    depth = w_in.shape[0]
    tp = seq + T0
    assert d == D_MODEL and tp % TIME_TILE == 0 and tp % GLA_TIME_TILE == 0
    assert (bsz * tp) % ROW_TILE == 0
    assert ROW_TILE > T0 and tp % ROW_TILE == 0
    lead = jnp.concatenate([jnp.zeros((PADF, d), x.dtype), meta_tokens.astype(x.dtype)], axis=0)
    h = x.reshape(bsz * seq, d)
    in_place = depth > 1
    if not in_place:
        h = jnp.concatenate([jnp.broadcast_to(lead[None], (bsz, T0, d)), x], axis=1).reshape(-1, d)
    vec = lambda a: a.reshape(depth, 1, -1)
    bf = lambda a: a.astype(jnp.bfloat16)

    w_main = bf(w_in)
    w_zg = bf(jnp.tile(w_in[:, :, W_IN_MAIN:], (1, 1, 3)))
    wg = jnp.zeros((depth, 2 * GLA_RANK, 2 * GLA_QK), jnp.float32)
    wg = wg.at[:, :GLA_RANK, :GLA_QK].set(gla_wg_f).at[:, GLA_RANK:, GLA_QK:].set(gla_wg_b)
    wg_hi, wg_lo = _split_bf16(wg)
    wg3 = jnp.concatenate([wg_hi, wg_lo, wg_hi], axis=1)
    bg = vec(jnp.concatenate([gla_bg_f, gla_bg_b], axis=1))
    gates_f = _block_diag_gates(lru_wa_f, lru_wx_f)
    gates_b = _block_diag_gates(lru_wa_b, lru_wx_b)
    params = dict(
        g_pre=vec(norm_mix_pre), w_main=w_main, w_zg=w_zg, wg3=wg3, bg=bg,
        conv_w=conv_w, conv_b=vec(conv_b),
        lru_f=(gates_f, vec(0.5 * lru_ba_f), vec(0.5 * lru_bx_f), vec(lru_lambda_f)),
        lru_b=(gates_b, vec(0.5 * lru_ba_b), vec(0.5 * lru_bx_b), vec(lru_lambda_b)),
        head_norm=vec(gla_head_norm),
        out=(bf(w_out), vec(norm_mix_post),
             vec(norm_mlp_pre), vec(norm_mlp_post), bf(w_mlp_up), bf(w_mlp_down)))

    for l in range(depth):
        at = lambda a: _Layer(a, l)
        p = jax.tree.map(at, params)
        tokens = dict(lead=lead) if (l == 0 and in_place) else {}
        outs = _in_proj(h, p['g_pre'], p['w_main'], p['w_zg'], p['wg3'], p['bg'],
                        p['conv_w'], p['conv_b'], n_batch=bsz, **tokens)
        xc3, gate3, qk3, v3, go3, gfb3 = (a.reshape(bsz, tp, a.shape[-1]) for a in outs)

        h_f = _lru_pass(xc3, *p['lru_f'], reverse=False)
        y_lru = _lru_pass(xc3, *p['lru_b'], reverse=True, hf=h_f, gate=gate3)
        o_f = _gla_pass(qk3, v3, gfb3, reverse=False)
        y_gla = _gla_pass(qk3, v3, gfb3, reverse=True, o_f=o_f, go=go3, head_norm=p['head_norm'])

        h = _out_mlp(h, y_lru.reshape(bsz * tp, -1), y_gla.reshape(bsz * tp, -1), *p['out'],
                     tp, drop_lead=(l == depth - 1), **tokens)
    return h.reshape(bsz, seq, d)
```

```python
import functools

import numpy as np
import jax
import jax.numpy as jnp
from jax import lax
from jax.experimental import pallas as pl
from jax.experimental.pallas import tpu as pltpu

D_MODEL = 1024
N_META = 16
CHUNK = 64
LRU_WIDTH = 512
LRU_HEADS = 8
LRU_HEAD_DIM = 64
LRU_C = 8.0
GLA_WIDTH = 512
GLA_HEADS = 4
GLA_DV = 128
GLA_DK = 64
GLA_QK = GLA_HEADS * GLA_DK
GLA_RANK = 16
GLA_GATE_NORM = 16.0
W_IN_MAIN = 2 * LRU_WIDTH + 2 * GLA_QK + 2 * GLA_WIDTH
D_FF = 4096
EPS = 1e-6
LOG2_E = 1.4426950408889634
GELU_SCALE = 0.7978845608028654
GELU_CUBIC = 0.044715
F32_TINY = float(np.finfo(np.float32).tiny)

LEAD_CHUNKS = 4
T0 = LEAD_CHUNKS * CHUNK
PADF = T0 - N_META
ROW_TILE = 1056
TIME_TILE = 1056
SCAN_UNROLL = 12
GLA_TIME_TILE = 768
GLA_UNROLL = 3
IN_SPLIT = 3
ROW_SPLIT = 2
FF_TILE = 1024
SUBLANES = 8
LANES = 128
MXU_ROWS = 256
V7X_VMEM_BYTES = 64 * 1024 * 1024
VMEM_LIMIT = V7X_VMEM_BYTES * 7 // 8


class _Layer:
    def __init__(self, stacked, layer):
        self.stacked, self.layer, self.shape = stacked, layer, tuple(stacked.shape[1:])


def _operand(a):
    return a.stacked if isinstance(a, _Layer) else a


def _const_spec(a):
    nd = len(a.shape)
    if isinstance(a, _Layer):
        layer = a.layer
        return pl.BlockSpec((None,) + a.shape, lambda *_: (layer,) + (0,) * nd,
                            pipeline_mode=pl.Buffered(1))
    return pl.BlockSpec(a.shape, lambda *_: (0,) * nd, pipeline_mode=pl.Buffered(1))


def _rms_scale(x):
    return lax.rsqrt(jnp.mean(x * x, axis=-1, keepdims=True) + EPS)


def _softplus(x):
    return jnp.maximum(x, 0.0) + jnp.log1p(jnp.exp(-jnp.abs(x)))


def _split_bf16(x):
    hi = x.astype(jnp.bfloat16)
    lo = (x - hi.astype(jnp.float32)).astype(jnp.bfloat16)
    return hi, lo


def _token_tile(x_ref, lead_ref):
    tm = x_ref.shape[0]
    fetched = x_ref[...]
    first = jnp.concatenate([lead_ref[...], fetched[0:tm - T0, :]], axis=0)
    return jnp.where(pl.program_id(1) == 0, first, fetched)


def _in_proj_kernel(*refs, n_tiles, from_tokens):
    if from_tokens:
        h_ref, hp_ref, hn_ref, lead_ref, *refs = refs
    else:
        h_ref, hp_ref, hn_ref, *refs = refs
    (g_ref, w_ref, wzg_ref, wg_ref, bg_ref, cw_ref, cb_ref,
     xc_ref, gate_ref, qk_ref, v_ref, go_ref, gfb_ref, xs_ref) = refs
    tm = h_ref.shape[0]
    normed = lambda x: (x * _rms_scale(x) * g_ref[...]).astype(jnp.bfloat16)
    if from_tokens:
        h_tile = _token_tile(h_ref, lead_ref)
        no_next = pl.program_id(1) == n_tiles - 1
    else:
        h_tile = h_ref[...]
        no_next = pl.program_id(0) == n_tiles - 1
    halo = jnp.concatenate([hp_ref[...], hn_ref[...]], axis=0)
    xn = normed(h_tile)
    xn_ext = jnp.concatenate([xn, normed(halo)], axis=0)

    zg = jnp.dot(xn, wzg_ref[...], preferred_element_type=jnp.float32)
    zx = jnp.dot(xn_ext, w_ref[:, 0:LRU_WIDTH], preferred_element_type=jnp.float32)
    zhi = zg.astype(jnp.bfloat16)
    zlo = (zg - zhi.astype(jnp.float32)).astype(jnp.bfloat16)
    lane = lax.broadcasted_iota(jnp.int32, zg.shape, 1)
    lhs = jnp.where(lane < 4 * GLA_RANK, zhi, zlo)
    pre = jnp.dot(lhs, wg_ref[...], preferred_element_type=jnp.float32) + bg_ref[...]
    n = tm // IN_SPLIT
    parts = [slice(i * n, (i + 1) * n) for i in range(IN_SPLIT)]
    zs = [jnp.dot(xn[p, :], w_ref[:, LRU_WIDTH:W_IN_MAIN], preferred_element_type=jnp.float32)
          for p in parts]

    xs_ref[0:SUBLANES, :] = zx[tm:tm + SUBLANES, :]
    xs_ref[SUBLANES:SUBLANES + tm, :] = zx[0:tm, :]
    xs_ref[SUBLANES + tm:2 * SUBLANES + tm, :] = jnp.where(
        no_next, 0.0, zx[tm + SUBLANES:tm + 2 * SUBLANES, :])
    cw = cw_ref[...]
    xc_ref[...] = (cw[0:1, :] * xs_ref[SUBLANES - 2:SUBLANES - 2 + tm, :]
                   + cw[1:2, :] * xs_ref[SUBLANES - 1:SUBLANES - 1 + tm, :]
                   + cw[2:3, :] * xs_ref[SUBLANES:SUBLANES + tm, :]
                   + cw[3:4, :] * xs_ref[SUBLANES + 1:SUBLANES + 1 + tm, :]
                   + cb_ref[...])
    logsig = jnp.minimum(pre, 0.0) - jnp.log(1.0 + jnp.exp(-jnp.abs(pre)))
    gfb_ref[...] = logsig * (LOG2_E / GLA_GATE_NORM)

    for p, z in zip(parts, zs):
        c0, c1, c2, c3 = LRU_WIDTH, LRU_WIDTH + GLA_QK, LRU_WIDTH + 2 * GLA_QK, \
            LRU_WIDTH + 2 * GLA_QK + GLA_WIDTH
        gt = z[:, 0:c0]
        gelu = 0.5 * gt * (1.0 + jnp.tanh(GELU_SCALE * (gt + GELU_CUBIC * gt * gt * gt)))
        gate_ref[p, :] = gelu.astype(jnp.bfloat16)
        q = z[:, c0:c1] * (GLA_DK ** -0.5)
        qk_ref[p, :] = jnp.concatenate([q, z[:, c1:c2]], axis=1).astype(jnp.bfloat16)
        v_ref[p, :] = z[:, c2:c3].astype(jnp.bfloat16)
        half = 0.5 * z[:, c3:]
        go_ref[p, :] = (half * jnp.tanh(half) + half).astype(jnp.bfloat16)


def _token_specs(n_batch, seq, tm):
    per_batch = (seq + T0) // tm
    total = n_batch * seq
    start = lambda b, j: b * seq + j * tm - T0
    window = lambda rows, at: pl.BlockSpec((pl.Element(rows), pl.Element(D_MODEL)),
                                           lambda b, j: (pl.multiple_of(
                                               jnp.clip(at(b, j), 0, total - rows), SUBLANES), 0))
    specs = [window(tm, lambda b, j: b * seq + jnp.maximum(j * tm - T0, 0)),
             window(SUBLANES, lambda b, j: start(b, j) - SUBLANES),
             window(SUBLANES, lambda b, j: start(b, j) + tm)]
    return (n_batch, per_batch), specs, (lambda w: pl.BlockSpec((tm, w), lambda b, j: (b * per_batch + j, 0)))


def _in_proj(h, g, w_main, w_zg, wg3, bg, conv_w, conv_b, lead=None, n_batch=None):
    tm = ROW_TILE
    consts = (g, w_main, w_zg, wg3, bg, conv_w, conv_b)
    if lead is None:
        rows = h.shape[0]
        grid = (rows // tm,)
        g8 = tm // SUBLANES
        row = lambda w: pl.BlockSpec((tm, w), lambda i: (i, 0))
        h_specs = [row(D_MODEL),
                   pl.BlockSpec((SUBLANES, D_MODEL), lambda i: (jnp.maximum(i * g8 - 1, 0), 0)),
                   pl.BlockSpec((SUBLANES, D_MODEL),
                                lambda i: (jnp.minimum((i + 1) * g8, rows // SUBLANES - 1), 0))]
        h_args = [h, h, h]
    else:
        seq = h.shape[0] // n_batch
        rows = n_batch * (seq + T0)
        grid, h_specs, row = _token_specs(n_batch, seq, tm)
        h_specs.append(_const_spec(lead))
        h_args = [h, h, h, lead]
    outs = [(LRU_WIDTH, jnp.float32), (LRU_WIDTH, jnp.bfloat16), (2 * GLA_QK, jnp.bfloat16),
            (GLA_WIDTH, jnp.bfloat16), (GLA_WIDTH, jnp.bfloat16), (2 * GLA_QK, jnp.float32)]
    return pl.pallas_call(
        functools.partial(_in_proj_kernel, n_tiles=grid[-1], from_tokens=lead is not None),
        grid=grid,
        in_specs=h_specs + [_const_spec(a) for a in consts],
        out_specs=[row(w) for w, _ in outs],
        out_shape=[jax.ShapeDtypeStruct((rows, w), dt) for w, dt in outs],
        scratch_shapes=[pltpu.VMEM((tm + 2 * SUBLANES, LRU_WIDTH), jnp.float32)],
        compiler_params=pltpu.CompilerParams(
            dimension_semantics=("arbitrary",) * len(grid), vmem_limit_bytes=VMEM_LIMIT),
        name="in_proj",
    )(*h_args, *map(_operand, consts))


def _lru_kernel(*refs, reverse, n_tiles):
    if reverse:
        (xc_ref, wg_ref, ba_ref, bx_ref, lam_ref, hf_ref, gate_ref,
         out_ref, a_scr, u_scr, carry_ref) = refs
    else:
        xc_ref, wg_ref, ba_ref, bx_ref, lam_ref, out_ref, a_scr, u_scr, carry_ref = refs
    bsz, tt, _ = xc_ref.shape
    step = pl.program_id(0)
    tile = (n_tiles - 1 - step) if reverse else step

    @pl.when(step == 0)
    def _():
        carry_ref[...] = jnp.zeros_like(carry_ref)

    n_slabs = LRU_WIDTH // LANES
    half = LRU_WIDTH // 2
    half_rate = (0.5 * LRU_C) * _softplus(-lam_ref[...])
    for bi in range(bsz):
        xc = xc_ref[bi]
        xcb = xc.astype(jnp.bfloat16)
        p0 = jnp.dot(xcb[:, :half], wg_ref[0], preferred_element_type=jnp.float32)
        p1 = jnp.dot(xcb[:, half:], wg_ref[1], preferred_element_type=jnp.float32)
        r_pre = jnp.concatenate([p0[:, :half], p1[:, :half]], axis=1)
        i_pre = jnp.concatenate([p0[:, half:], p1[:, half:]], axis=1)
        tr = jnp.tanh(r_pre + ba_ref[...])
        gi = 0.5 * jnp.tanh(i_pre + bx_ref[...]) + 0.5
        neg_log_a = tr * half_rate + half_rate
        a = jnp.exp(-neg_log_a)
        w = jnp.tanh(neg_log_a) * (1.0 + a * a)
        u = (w * lax.rsqrt(jnp.maximum(w, F32_TINY))) * (gi * xc)
        if not reverse:
            row = lax.broadcasted_iota(jnp.int32, xc.shape, 0)
            u = jnp.where(row + tile * tt >= PADF, u, 0.0)
        for k in range(n_slabs):
            a_scr[bi * n_slabs + k] = a[:, k * LANES:(k + 1) * LANES]
            u_scr[bi * n_slabs + k] = u[:, k * LANES:(k + 1) * LANES]

    run = tt // SUBLANES
    chains = bsz * n_slabs
    steps = lambda j: pl.ds(j, SUBLANES, stride=run)

    assert run % SCAN_UNROLL == 0
    n_blocks = run // SCAN_UNROLL
    block_base = lambda i: ((n_blocks - 1 - i) if reverse else i) * SCAN_UNROLL
    offsets = range(SCAN_UNROLL - 1, -1, -1) if reverse else range(SCAN_UNROLL)

    def run_ends(i, carry):
        state, prod = list(carry[0]), list(carry[1])
        base = block_base(i)
        for o in offsets:
            for k in range(chains):
                aj = a_scr[k, steps(base + o), :]
                state[k] = aj * state[k] + u_scr[k, steps(base + o), :]
                prod[k] = prod[k] * aj
        return tuple(state), tuple(prod)

    zeros = tuple(jnp.zeros((SUBLANES, LANES), jnp.float32) for _ in range(chains))
    ones = tuple(z + 1.0 for z in zeros)
    end_state, end_prod = lax.fori_loop(0, n_blocks, run_ends, (zeros, ones))
    end_state, end_prod = jnp.stack(end_state), jnp.stack(end_prod)

    c = carry_ref[...]
    order = range(SUBLANES - 1, -1, -1) if reverse else range(SUBLANES)
    entering = [None] * SUBLANES
    for r in order:
        entering[r] = c
        c = end_prod[:, r:r + 1, :] * c + end_state[:, r:r + 1, :]
    carry_ref[...] = c
    entering = jnp.concatenate(entering, axis=1)

    def scan(i, state):
        state = list(state)
        base = block_base(i)
        for o in offsets:
            for k in range(chains):
                state[k] = a_scr[k, steps(base + o), :] * state[k] + u_scr[k, steps(base + o), :]
                u_scr[k, steps(base + o), :] = state[k]
        return tuple(state)

    lax.fori_loop(0, n_blocks, scan, tuple(entering[k] for k in range(chains)))

    for bi in range(bsz):
        hs = jnp.concatenate([u_scr[bi * n_slabs + k] for k in range(n_slabs)], axis=1)
        if reverse:
            gate = gate_ref[bi].astype(jnp.float32)
            out_ref[bi] = ((hf_ref[bi] + hs) * gate).astype(out_ref.dtype)
        else:
            out_ref[bi] = hs


def _lru_pass(xc, wgate, ba, bx, lam, *, reverse, hf=None, gate=None):
    bsz, tp, _ = xc.shape
    tt = TIME_TILE
    nt = tp // tt
    chains = bsz * (LRU_WIDTH // LANES)
    tsel = (lambda t: nt - 1 - t) if reverse else (lambda t: t)
    cur = pl.BlockSpec((bsz, tt, LRU_WIDTH), lambda t: (0, tsel(t), 0))
    in_specs = [cur] + [_const_spec(a) for a in (wgate, ba, bx, lam)]
    args = [xc, wgate, ba, bx, lam]
    if reverse:
        in_specs += [cur, cur]
        args += [hf, gate]
        out_dtype = jnp.bfloat16
    else:
        out_dtype = jnp.float32
    return pl.pallas_call(
        functools.partial(_lru_kernel, reverse=reverse, n_tiles=nt),
        grid=(nt,),
        in_specs=in_specs,
        out_specs=cur,
        out_shape=jax.ShapeDtypeStruct((bsz, tp, LRU_WIDTH), out_dtype),
        scratch_shapes=[pltpu.VMEM((chains, tt, LANES), jnp.float32),
                        pltpu.VMEM((chains, tt, LANES), jnp.float32),
                        pltpu.VMEM((chains, 1, LANES), jnp.float32)],
        compiler_params=pltpu.CompilerParams(
            dimension_semantics=("arbitrary",), vmem_limit_bytes=VMEM_LIMIT),
        name="lru_bwd" if reverse else "lru_fwd",
    )(*map(_operand, args))


def _gla_constants(reverse):
    c, w = CHUNK, SUBLANES
    t = np.arange(c)[:, None]
    s = np.arange(c)[None, :]
    gt, gs, pt, ps = t // w, s // w, t // 2, s // 2
    pair_code = 2 + (s % w) // 2
    if reverse:
        code = np.where((pt == ps) & (s > t), 1,
                        np.where((gt == gs) & (ps > pt), pair_code, np.where(gs > gt, w + gs, -1)))
        blocks = [s >= t, (t % 2 == 1) & (s == t - 1)]
    else:
        code = np.where((pt == ps) & (s <= t), t - s,
                        np.where((gt == gs) & (ps < pt), pair_code, np.where(gs < gt, w + gs, -1)))
        blocks = [s <= t, (t % 2 == 0) & (s == t + 1)]
    lmat = np.concatenate([blk.astype(np.float32) for blk in blocks], axis=0)
    lmat = np.concatenate([lmat, lmat], axis=1)
    code = np.tile(code.astype(np.int32), (1, GLA_HEADS))
    hmask = np.kron(np.eye(GLA_HEADS, dtype=np.float32), np.ones((c, GLA_DK), np.float32))
    return jnp.asarray(lmat, jnp.bfloat16), jnp.asarray(code), jnp.asarray(hmask, jnp.bfloat16)


def _gla_decays(g, lmat):
    ghi, glo = _split_bf16(g)
    return jnp.dot(lmat, jnp.concatenate([ghi, glo], axis=0), preferred_element_type=jnp.float32)


def _gla_scores(q, k, g, logdec, hmask, reverse):
    n_groups = CHUNK // SUBLANES
    pairs = SUBLANES // 2

    def head_stack(x):
        return jnp.concatenate([x.astype(jnp.bfloat16)] * GLA_HEADS, axis=0) * hmask

    def terms(lhs, rhs):
        return lax.dot_general(jnp.concatenate(lhs, axis=0).astype(jnp.bfloat16), head_stack(rhs),
                               (((1,), (1,)), ((), ())), preferred_element_type=jnp.float32)

    def facing_rows(first):
        return jnp.concatenate([jnp.broadcast_to(b[first + gi * SUBLANES:first + gi * SUBLANES + 1, :],
                                                 (SUBLANES, GLA_QK)) for gi in range(n_groups)], axis=0)

    b = logdec[0:CHUNK, :]

    step = q * jnp.exp2(g)
    p_pair = terms([step] if reverse else [q, step], k)

    k_pair = k * jnp.exp2(logdec[CHUNK:2 * CHUNK, :])
    pair_ids = range(1, pairs) if reverse else range(pairs - 1)
    lhs = [q * jnp.exp2(jnp.minimum(b - facing_rows(2 * p if reverse else 2 * p + 1), 0.0))
           for p in pair_ids]
    p_group = terms(lhs, k_pair)

    b_anchor = facing_rows(0 if reverse else SUBLANES - 1)
    groups = list(range(1, n_groups)) if reverse else list(range(n_groups - 1))
    row_sel = (lambda gi: slice(0, gi * SUBLANES)) if reverse else \
              (lambda gi: slice((gi + 1) * SUBLANES, CHUNK))
    anchor_row = (lambda gi: gi * SUBLANES) if reverse else (lambda gi: gi * SUBLANES + SUBLANES - 1)
    lhs = [q[row_sel(gi), :] * jnp.exp2(b[row_sel(gi), :] - b[anchor_row(gi):anchor_row(gi) + 1, :])
           for gi in groups]
    p_cross = terms(lhs, k * jnp.exp2(b_anchor - b))

    last = 0 if reverse else CHUNK - 1
    b_last = b[last:last + 1, :]
    qe = (q * jnp.exp2(b)).astype(jnp.bfloat16)
    ke = (k * jnp.exp2(b_last - b)).astype(jnp.bfloat16)
    return p_pair, p_group, p_cross, qe, ke, jnp.exp2(b_last)


def _gla_apply(p_pair, p_group, p_cross, qe, ke, e_last, vb, s_ref, code, reverse):
    n_groups = CHUNK // SUBLANES
    pairs = SUBLANES // 2
    groups = list(range(1, n_groups)) if reverse else list(range(n_groups - 1))
    scores = jnp.zeros((CHUNK, GLA_HEADS * CHUNK), jnp.float32)
    for i, d in enumerate([1] if reverse else [0, 1]):
        scores = jnp.where(code == d, p_pair[i * CHUNK:(i + 1) * CHUNK, :], scores)
    for i, p in enumerate(range(1, pairs) if reverse else range(pairs - 1)):
        scores = jnp.where(code == 2 + p, p_group[i * CHUNK:(i + 1) * CHUNK, :], scores)
    off = 0
    for gi in groups:
        rows = slice(0, gi * SUBLANES) if reverse else slice((gi + 1) * SUBLANES, CHUNK)
        n = rows.stop - rows.start
        part = jnp.where(code[rows, :] == SUBLANES + gi, p_cross[off:off + n, :], scores[rows, :])
        scores = jnp.concatenate([part, scores[n:, :]] if reverse else [scores[:CHUNK - n, :], part],
                                 axis=0)
        off += n
    sb = scores.astype(jnp.bfloat16)

    eye = (lax.broadcasted_iota(jnp.int32, (GLA_DK, GLA_DK), 0)
           == lax.broadcasted_iota(jnp.int32, (GLA_DK, GLA_DK), 1))
    zero_v = jnp.zeros((CHUNK, GLA_DV), jnp.bfloat16)
    zero_s = jnp.zeros((GLA_DK, GLA_DV), jnp.bfloat16)
    outs = []
    for pair in range(GLA_HEADS // 2):
        heads = (2 * pair, 2 * pair + 1)
        vh = [vb[:, hd * GLA_DV:(hd + 1) * GLA_DV] for hd in heads]
        st = [s_ref[hd] for hd in heads]
        sh = [s.astype(jnp.bfloat16) for s in st]
        lanes = slice(pair * 2 * CHUNK, (pair + 1) * 2 * CHUNK)
        lhs = jnp.concatenate([sb[:, lanes], qe[:, lanes]], axis=1)
        rhs = jnp.concatenate([jnp.concatenate([vh[0], zero_v], axis=1),
                               jnp.concatenate([zero_v, vh[1]], axis=1),
                               jnp.concatenate([sh[0], zero_s], axis=1),
                               jnp.concatenate([zero_s, sh[1]], axis=1)], axis=0)
        outs.append(jnp.dot(lhs, rhs, preferred_element_type=jnp.float32))
        for i, hd in enumerate(heads):
            ks = slice(hd * GLA_DK, (hd + 1) * GLA_DK)
            kv = lax.dot_general(ke[:, ks], vh[i], (((0,), (0,)), ((), ())),
                                 preferred_element_type=jnp.float32)
            e_col = jnp.sum(jnp.where(eye, e_last[:, ks], 0.0), axis=1, keepdims=True)
            s_ref[hd] = st[i] * e_col + kv
    return jnp.concatenate(outs, axis=1)


def _gla_kernel(*refs, reverse):
    if reverse:
        (qk_ref, v_ref, g_ref, lmat_ref, code_ref, hmask_ref, of_ref, go_ref, hn_ref,
         out_ref, s_ref) = refs
    else:
        qk_ref, v_ref, g_ref, lmat_ref, code_ref, hmask_ref, out_ref, s_ref = refs
    bsz, tt, _ = qk_ref.shape
    n_chunks = tt // CHUNK

    @pl.when(pl.program_id(0) == 0)
    def _():
        s_ref[...] = jnp.zeros_like(s_ref)

    lmat = lmat_ref[...]
    code = code_ref[...]
    hmask = hmask_ref[...]

    def chunk_body(j, carry):
        work = []
        for u in range(GLA_UNROLL):
            jj = j * GLA_UNROLL + u
            cidx = (n_chunks - 1 - jj) if reverse else jj
            rows = pl.ds(pl.multiple_of(cidx * CHUNK, CHUNK), CHUNK)
            work += [(bi, rows) for bi in range(bsz)]
        decs = [_gla_decays(g_ref[bi, rows, :], lmat) for bi, rows in work]
        terms = [_gla_scores(qk_ref[bi, rows, 0:GLA_QK], qk_ref[bi, rows, GLA_QK:2 * GLA_QK],
                             g_ref[bi, rows, :], dec, hmask, reverse)
                 for (bi, rows), dec in zip(work, decs)]
        for (bi, rows), term in zip(work, terms):
            o = _gla_apply(*term, v_ref[bi, rows, :].astype(jnp.bfloat16), s_ref.at[bi], code,
                           reverse)
            if reverse:
                o = o + of_ref[bi, rows, :]
                parts = []
                for hd in range(GLA_HEADS):
                    oh = o[:, hd * GLA_DV:(hd + 1) * GLA_DV]
                    parts.append(oh * _rms_scale(oh))
                y = jnp.concatenate(parts, axis=1) * hn_ref[...] * go_ref[bi, rows, :].astype(jnp.float32)
                out_ref[bi, rows, :] = y.astype(out_ref.dtype)
            else:
                out_ref[bi, rows, :] = o
        return carry

    lax.fori_loop(0, n_chunks // GLA_UNROLL, chunk_body, 0)


def _gla_pass(qk, v, gfb, *, reverse, o_f=None, go=None, head_norm=None):
    bsz, tp, _ = qk.shape
    tt = GLA_TIME_TILE
    nt = tp // tt
    tsel = (lambda t: nt - 1 - t) if reverse else (lambda t: t)
    blk = lambda w, c=0: pl.BlockSpec((bsz, tt, w), lambda t: (0, tsel(t), c))
    consts = _gla_constants(reverse)
    in_specs = [blk(2 * GLA_QK), blk(GLA_WIDTH), blk(GLA_QK, 1 if reverse else 0)]
    in_specs += [_const_spec(a) for a in consts]
    args = [qk, v, gfb, *consts]
    if reverse:
        in_specs += [blk(GLA_WIDTH), blk(GLA_WIDTH), _const_spec(head_norm)]
        args += [o_f, go, head_norm]
        out_dtype = jnp.bfloat16
    else:
        out_dtype = jnp.float32
    return pl.pallas_call(
        functools.partial(_gla_kernel, reverse=reverse),
        grid=(nt,),
        in_specs=in_specs,
        out_specs=blk(GLA_WIDTH),
        out_shape=jax.ShapeDtypeStruct((bsz, tp, GLA_WIDTH), out_dtype),
        scratch_shapes=[pltpu.VMEM((bsz, GLA_HEADS, GLA_DK, GLA_DV), jnp.float32)],
        compiler_params=pltpu.CompilerParams(
            dimension_semantics=("arbitrary",), vmem_limit_bytes=VMEM_LIMIT),
        name="gla_bwd" if reverse else "gla_fwd",
    )(*map(_operand, args))


def _token_rows_copy(buf_ref, out_ref, sem_ref, slot, lead_tile, first_token):
    tm = buf_ref.shape[1]
    if lead_tile:
        return pltpu.make_async_copy(buf_ref.at[slot, pl.ds(T0, tm - T0)],
                                     out_ref.at[pl.ds(first_token, tm - T0)], sem_ref.at[slot])
    return pltpu.make_async_copy(buf_ref.at[slot], out_ref.at[pl.ds(first_token, tm)],
                                 sem_ref.at[slot])


def _out_mlp_kernel(*refs, rows_per_batch, n_batch, per_batch, from_tokens, drop_lead):
    per_batch_grid = per_batch is not None
    if from_tokens:
        x_ref, lead_ref, *refs = refs
        h_tile = _token_tile(x_ref, lead_ref)
    else:
        h_tile, *refs = refs
    if drop_lead:
        *refs, buf_ref, sem_ref = refs
    (yl_ref, yg_ref, wo_ref, gpost_ref, gpre_ref, gpost2_ref,
     wup_ref, wdn_ref, out_ref, acc_ref) = refs
    tm = acc_ref.shape[0]
    tile = pl.program_id(0)
    if per_batch_grid:
        tile = tile * pl.num_programs(1) + pl.program_id(1)
    if drop_lead:
        n_steps = n_batch * per_batch
        slot = lax.rem(tile, 2)

        def wait_step(step, its_slot):
            was_lead = lax.rem(step, per_batch) == 0
            for lead_tile in (True, False):
                @pl.when(was_lead == lead_tile)
                def _():
                    _token_rows_copy(buf_ref, out_ref, sem_ref, its_slot, lead_tile, 0).wait()

        @pl.when(tile >= 2)
        def _():
            wait_step(tile - 2, slot)
        store_ref = buf_ref.at[slot]
    else:
        store_ref = out_ref
    n_split = ROW_SPLIT if tm >= ROW_SPLIT * MXU_ROWS else 1
    n = tm // n_split
    parts = [slice(i * n, (i + 1) * n) for i in range(n_split)]
    mix = [jnp.dot(yl_ref[p, :], wo_ref[0:LRU_WIDTH, :], preferred_element_type=jnp.float32)
           + jnp.dot(yg_ref[p, :], wo_ref[LRU_WIDTH:, :], preferred_element_type=jnp.float32)
           for p in parts]
    h1, xn = [], []
    for i, p in enumerate(parts):
        row = lax.broadcasted_iota(jnp.int32, (n, 1), 0) + (tile * tm + i * n)
        real = row >= PADF
        for b in range(1, n_batch):
            real = real & ((row < b * rows_per_batch) | (row >= b * rows_per_batch + PADF))
        h1.append(h_tile[p, :] + jnp.where(real, mix[i] * _rms_scale(mix[i]) * gpost_ref[...], 0.0))
        xn.append((h1[i] * _rms_scale(h1[i]) * gpre_ref[...]).astype(jnp.bfloat16))
    for c in range(D_FF // FF_TILE):
        cs = slice(c * FF_TILE, (c + 1) * FF_TILE)
        for i, p in enumerate(parts):
            up = jnp.dot(xn[i], wup_ref[:, cs], preferred_element_type=jnp.float32)
            act = jnp.square(jnp.maximum(up, 0.0)).astype(jnp.bfloat16)
            part = jnp.dot(act, wdn_ref[cs, :], preferred_element_type=jnp.float32)
            if c == 0:
                acc_ref[p, :] = part
            else:
                acc_ref[p, :] += part
    for i, p in enumerate(parts):
        ff = acc_ref[p, :]
        store_ref[p, :] = h1[i] + ff * _rms_scale(ff) * gpost2_ref[...]
    if drop_lead:
        b, j = pl.program_id(0), pl.program_id(1)
        seq = rows_per_batch - T0
        for lead_tile in (True, False):
            @pl.when((j == 0) == lead_tile)
            def _():
                first = b * seq + (0 if lead_tile else j * tm - T0)
                _token_rows_copy(buf_ref, out_ref, sem_ref, slot, lead_tile,
                                 pl.multiple_of(first, SUBLANES)).start()

        @pl.when(tile == n_steps - 1)
        def _():
            wait_step(tile, slot)
            if n_steps >= 2:
                wait_step(tile - 1, 1 - slot)


def _out_mlp(h, y_lru, y_gla, wo, g_post, g_pre, g_post2, w_up, w_dn, rows_per_batch,
             drop_lead, lead=None):
    rows = y_lru.shape[0]
    n_batch = rows // rows_per_batch
    consts = (wo, g_post, g_pre, g_post2, w_up, w_dn)
    tm = ROW_TILE
    per_batch = None
    scratch = [pltpu.VMEM((tm, D_MODEL), jnp.float32)]
    if lead is not None:
        assert not drop_lead
        grid, (h_spec, _, _), row = _token_specs(n_batch, rows_per_batch - T0, tm)
        per_batch = grid[1]
        h_specs, h_args = [h_spec, _const_spec(lead)], [h, lead]
        out_spec = row(D_MODEL)
        out_rows = rows
    elif drop_lead:
        per_batch = rows_per_batch // tm
        grid = (n_batch, per_batch)
        row = lambda w: pl.BlockSpec((tm, w), lambda b, j: (b * per_batch + j, 0))
        out_spec = pl.BlockSpec(memory_space=pl.ANY)
        out_rows = rows - n_batch * T0
        h_specs, h_args = [row(D_MODEL)], [h]
        scratch += [pltpu.VMEM((2, tm, D_MODEL), jnp.float32), pltpu.SemaphoreType.DMA((2,))]
    else:
        grid = (rows // tm,)
        row = lambda w: pl.BlockSpec((tm, w), lambda i: (i, 0))
        out_spec = row(D_MODEL)
        out_rows = rows
        h_specs, h_args = [row(D_MODEL)], [h]
    return pl.pallas_call(
        functools.partial(_out_mlp_kernel, rows_per_batch=rows_per_batch, n_batch=n_batch,
                          per_batch=per_batch, from_tokens=lead is not None, drop_lead=drop_lead),
        grid=grid,
        in_specs=h_specs + [row(LRU_WIDTH), row(GLA_WIDTH)] + [_const_spec(a) for a in consts],
        out_specs=out_spec,
        out_shape=jax.ShapeDtypeStruct((out_rows, D_MODEL), jnp.float32),
        scratch_shapes=scratch,
        compiler_params=pltpu.CompilerParams(
            dimension_semantics=("arbitrary",) * len(grid), vmem_limit_bytes=VMEM_LIMIT),
        name="out_mlp",
    )(*h_args, y_lru, y_gla, *map(_operand, consts))


def _block_diag_gates(wa, wx):
    per_half = LRU_HEADS // 2
    eye = jnp.eye(per_half, dtype=wa.dtype)

    def bd(w):
        w5 = w.reshape(-1, 2, per_half, LRU_HEAD_DIM, LRU_HEAD_DIM)
        return jnp.einsum('nhjil,jk->nhjikl', w5, eye).reshape(-1, 2, LRU_WIDTH // 2, LRU_WIDTH // 2)
    return (0.5 * jnp.concatenate([bd(wa), bd(wx)], axis=3)).astype(jnp.bfloat16)


def kernel(x, meta_tokens, norm_mix_pre, norm_mix_post, norm_mlp_pre, norm_mlp_post,
           w_in, conv_w, conv_b,
           lru_wa_f, lru_ba_f, lru_wx_f, lru_bx_f, lru_lambda_f,
           lru_wa_b, lru_ba_b, lru_wx_b, lru_bx_b, lru_lambda_b,
           gla_wg_f, gla_bg_f, gla_wg_b, gla_bg_b, gla_head_norm,
           w_out, w_mlp_up, w_mlp_down):
    bsz, seq, d = x.shape
    depth = w_in.shape[0]
    tp = seq + T0
    assert d == D_MODEL and tp % TIME_TILE == 0 and tp % GLA_TIME_TILE == 0
    assert (bsz * tp) % ROW_TILE == 0
    assert ROW_TILE > T0 and tp % ROW_TILE == 0
    lead = jnp.concatenate([jnp.zeros((PADF, d), x.dtype), meta_tokens.astype(x.dtype)], axis=0)
    h = x.reshape(bsz * seq, d)
    in_place = depth > 1
    if not in_place:
        h = jnp.concatenate([jnp.broadcast_to(lead[None], (bsz, T0, d)), x], axis=1).reshape(-1, d)
    vec = lambda a: a.reshape(depth, 1, -1)
    bf = lambda a: a.astype(jnp.bfloat16)

    w_main = bf(w_in)
    w_zg = bf(jnp.tile(w_in[:, :, W_IN_MAIN:], (1, 1, 3)))
    wg = jnp.zeros((depth, 2 * GLA_RANK, 2 * GLA_QK), jnp.float32)
    wg = wg.at[:, :GLA_RANK, :GLA_QK].set(gla_wg_f).at[:, GLA_RANK:, GLA_QK:].set(gla_wg_b)
    wg_hi, wg_lo = _split_bf16(wg)
    wg3 = jnp.concatenate([wg_hi, wg_lo, wg_hi], axis=1)
    bg = vec(jnp.concatenate([gla_bg_f, gla_bg_b], axis=1))
    gates_f = _block_diag_gates(lru_wa_f, lru_wx_f)
    gates_b = _block_diag_gates(lru_wa_b, lru_wx_b)
    params = dict(
        g_pre=vec(norm_mix_pre), w_main=w_main, w_zg=w_zg, wg3=wg3, bg=bg,
        conv_w=conv_w, conv_b=vec(conv_b),
        lru_f=(gates_f, vec(0.5 * lru_ba_f), vec(0.5 * lru_bx_f), vec(lru_lambda_f)),
        lru_b=(gates_b, vec(0.5 * lru_ba_b), vec(0.5 * lru_bx_b), vec(lru_lambda_b)),
        head_norm=vec(gla_head_norm),
        out=(bf(w_out), vec(norm_mix_post),
             vec(norm_mlp_pre), vec(norm_mlp_post), bf(w_mlp_up), bf(w_mlp_down)))

    for l in range(depth):
        at = lambda a: _Layer(a, l)
        p = jax.tree.map(at, params)
        tokens = dict(lead=lead) if (l == 0 and in_place) else {}
        outs = _in_proj(h, p['g_pre'], p['w_main'], p['w_zg'], p['wg3'], p['bg'],
                        p['conv_w'], p['conv_b'], n_batch=bsz, **tokens)
        xc3, gate3, qk3, v3, go3, gfb3 = (a.reshape(bsz, tp, a.shape[-1]) for a in outs)

        h_f = _lru_pass(xc3, *p['lru_f'], reverse=False)
        y_lru = _lru_pass(xc3, *p['lru_b'], reverse=True, hf=h_f, gate=gate3)
        o_f = _gla_pass(qk3, v3, gfb3, reverse=False)
        y_gla = _gla_pass(qk3, v3, gfb3, reverse=True, o_f=o_f, go=go3, head_norm=p['head_norm'])

        h = _out_mlp(h, y_lru.reshape(bsz * tp, -1), y_gla.reshape(bsz * tp, -1), *p['out'],
                     tp, drop_lead=(l == depth - 1), **tokens)
    return h.reshape(bsz, seq, d)
```
